```python
import jax, jax.numpy as jnp
from jax import lax
import numpy as np

D_MODEL = 1024
BATCH = 4
SEQ = 8192
DEPTH = 1

CHUNK = 64
N_META = 16
D_MIX = D_MODEL
D_ATTN = D_MIX // 2
D_CONV = D_MIX - D_ATTN
ATTN_HEADS = 8
HEAD_DIM = D_ATTN // ATTN_HEADS
CONV_WIDTH = 3
D_FF = 2816
Q_BLOCK = 128
EPS = 1e-6
MIX_IN_SIZES = (D_ATTN, D_ATTN, D_ATTN, ATTN_HEADS, D_CONV, D_CONV, D_CONV)
MIX_IN_WIDTH = sum(MIX_IN_SIZES)
MIX_IN_SPLITS = tuple(int(s) for s in np.cumsum(MIX_IN_SIZES)[:-1])

kernel_name = "hymba_fox_shortconv_macaron_block"


def rmsnorm(x, g):
    xf = x.astype(jnp.float32)
    y = xf * lax.rsqrt(jnp.mean(xf * xf, axis=-1, keepdims=True) + EPS)
    return (y * g.astype(jnp.float32)).astype(x.dtype)


def swiglu(h, w_in, w_out):
    gate, up = jnp.split(h @ w_in, 2, axis=-1)
    return (jax.nn.silu(gate) * up) @ w_out


def forgetting_attention(q, k, v, f_logit, b_forget, q_norm, k_norm):
    b, l, _ = q.shape
    q = rmsnorm(q.reshape(b, l, ATTN_HEADS, HEAD_DIM), q_norm)
    k = rmsnorm(k.reshape(b, l, ATTN_HEADS, HEAD_DIM), k_norm)
    v = v.reshape(b, l, ATTN_HEADS, HEAD_DIM)
    log_f = jax.nn.log_sigmoid((f_logit + b_forget).astype(jnp.float32))
    cum = jnp.cumsum(log_f, axis=1)
    lp = -(-l // Q_BLOCK) * Q_BLOCK
    pad = lp - l
    nb = lp // Q_BLOCK
    pad4 = ((0, 0), (0, pad), (0, 0), (0, 0))
    qh = jnp.pad(q, pad4).transpose(0, 2, 1, 3).astype(jnp.float32)
    kh = jnp.pad(k, pad4).transpose(0, 2, 1, 3).astype(jnp.float32)
    vh = jnp.pad(v, pad4).transpose(0, 2, 1, 3)
    cum_h = jnp.pad(cum, ((0, 0), (0, pad), (0, 0))).transpose(0, 2, 1)
    q_blocks = qh.reshape(b, ATTN_HEADS, nb, Q_BLOCK, HEAD_DIM).transpose(2, 0, 1, 3, 4)
    cq_blocks = cum_h.reshape(b, ATTN_HEADS, nb, Q_BLOCK).transpose(2, 0, 1, 3)
    kpos = jnp.arange(lp)
    scale = HEAD_DIM ** -0.5

    def one_block(args):
        qb, cqb, blk = args
        s = jnp.einsum('bhqd,bhkd->bhqk', qb, kh) * scale \
            + cqb[..., :, None] - cum_h[..., None, :]
        qpos = blk * Q_BLOCK + jnp.arange(Q_BLOCK)
        s = jnp.where(kpos[None, :] <= qpos[:, None], s, -jnp.inf)
        p = jax.nn.softmax(s, axis=-1)
        return jnp.einsum('bhqk,bhkd->bhqd', p.astype(vh.dtype), vh)

    o = lax.map(one_block, (q_blocks, cq_blocks, jnp.arange(nb)))
    o = o.transpose(1, 0, 3, 2, 4).reshape(b, lp, D_ATTN)[:, :l]
    return o.astype(q.dtype)


def gated_short_conv(gate_b, gate_c, u, conv_w):
    z = gate_c * u
    l = z.shape[1]
    zp = jnp.pad(z, ((0, 0), (CONV_WIDTH - 1, 0), (0, 0)))
    y = sum(zp[:, i:i + l] * conv_w[i] for i in range(CONV_WIDTH))
    return gate_b * y


def token_mixing(h, w_mix_in, b_forget, q_norm, k_norm, conv_w, attn_out_norm, conv_out_norm, w_mix_out):
    proj = h @ w_mix_in
    q, k, v, f_logit, gate_b, gate_c, u = jnp.split(proj, MIX_IN_SPLITS, axis=-1)
    o_attn = forgetting_attention(q, k, v, f_logit, b_forget, q_norm, k_norm)
    o_conv = gated_short_conv(gate_b, gate_c, u, conv_w)
    merged = jnp.concatenate([rmsnorm(o_attn, attn_out_norm), rmsnorm(o_conv, conv_out_norm)], axis=-1)
    return merged @ w_mix_out


def setup_inputs(seed: int = 0) -> dict:
    key = jax.random.key(seed)
    ks = jax.random.split(key, 20)
    f32 = jnp.float32
    nrm = lambda k, shape, s: jax.random.normal(k, shape, f32) * s
    gain = lambda k, shape: 1.0 + 0.02 * jax.random.normal(k, shape, f32)
    L = DEPTH
    return {
        "x": jax.random.normal(ks[0], (BATCH, SEQ, D_MODEL), f32),
        "meta_tokens": nrm(ks[1], (N_META, D_MODEL), 1.0),
        "ffn1_norm": gain(ks[2], (L, D_MODEL)),
        "ffn1_w_in": nrm(ks[3], (L, D_MODEL, 2 * D_FF), D_MODEL ** -0.5),
        "ffn1_w_out": nrm(ks[4], (L, D_FF, D_MODEL), D_FF ** -0.5),
        "mix_norm": gain(ks[5], (L, D_MODEL)),
        "w_mix_in": nrm(ks[6], (L, D_MODEL, MIX_IN_WIDTH), D_MODEL ** -0.5),
        "b_forget": nrm(ks[7], (L, ATTN_HEADS), 0.1),
        "q_norm": gain(ks[8], (L, HEAD_DIM)),
        "k_norm": gain(ks[9], (L, HEAD_DIM)),
        "conv_w": nrm(ks[10], (L, CONV_WIDTH, D_CONV), CONV_WIDTH ** -0.5),
        "attn_out_norm": gain(ks[11], (L, D_ATTN)),
        "conv_out_norm": gain(ks[12], (L, D_CONV)),
        "w_mix_out": nrm(ks[13], (L, D_MIX, D_MODEL), D_MIX ** -0.5),
        "ffn2_norm": gain(ks[14], (L, D_MODEL)),
        "ffn2_w_in": nrm(ks[15], (L, D_MODEL, 2 * D_FF), D_MODEL ** -0.5),
        "ffn2_w_out": nrm(ks[16], (L, D_FF, D_MODEL), D_FF ** -0.5),
        "final_norm": gain(ks[17], (L, D_MODEL)),
    }


def reference(x, meta_tokens, ffn1_norm, ffn1_w_in, ffn1_w_out, mix_norm, w_mix_in, b_forget,
              q_norm, k_norm, conv_w, attn_out_norm, conv_out_norm, w_mix_out,
              ffn2_norm, ffn2_w_in, ffn2_w_out, final_norm):
    b = x.shape[0]
    meta = jnp.broadcast_to(meta_tokens[None].astype(x.dtype), (b, N_META, D_MODEL))
    h = jnp.concatenate([meta, x], axis=1)
    for i in range(DEPTH):
        h = h + 0.5 * swiglu(rmsnorm(h, ffn1_norm[i]), ffn1_w_in[i], ffn1_w_out[i])
        h = h + token_mixing(rmsnorm(h, mix_norm[i]), w_mix_in[i], b_forget[i], q_norm[i], k_norm[i],
                             conv_w[i], attn_out_norm[i], conv_out_norm[i], w_mix_out[i])
        h = h + 0.5 * swiglu(rmsnorm(h, ffn2_norm[i]), ffn2_w_in[i], ffn2_w_out[i])
        h = rmsnorm(h, final_norm[i])
    return h[:, N_META:]
```

```python
import functools

import jax
import jax.numpy as jnp
from jax import lax
from jax.experimental import pallas as pl
from jax.experimental.pallas import tpu as pltpu

D_MODEL = 1024
N_META = 16
D_ATTN = 512
D_CONV = 512
HEADS = 8
HEAD_DIM = 64
CONV_WIDTH = 3
D_FF = 2816
EPS = 1e-6

F32 = jnp.float32
BF16 = jnp.bfloat16

LANES = 128
SUBLANES = 8
MXU_DIM = 256
VMEM_LIMIT_BYTES = 60000 * 1024

TOKEN_TILE = 512
META_TILE = LANES
FF_CHUNK = MXU_DIM
N_FF_CHUNKS = D_FF // FF_CHUNK
Q_BLOCK = MXU_DIM
K_BLOCK = MXU_DIM
AUG_DIM = LANES
F_ROWS = 2 * SUBLANES

EXP_UNDERFLOW = -104.0

assert D_FF % FF_CHUNK == 0


def _dot(a, b):
    return jnp.dot(a, b, preferred_element_type=F32)


def _dot_nt(a, b):
    return lax.dot_general(a, b, (((1,), (1,)), ((), ())), preferred_element_type=F32)


def _rmsnorm_rows(x, gain):
    ms = jnp.mean(x * x, axis=-1, keepdims=True)
    return x * lax.rsqrt(ms + EPS) * gain


def _split3(x):
    hi = x.astype(BF16)
    r1 = x - hi.astype(F32)
    mid = r1.astype(BF16)
    lo = (r1 - mid.astype(F32)).astype(BF16)
    return hi, mid, lo


def _swiglu(xn_ref, wg_ref, wu_ref, wo_ref, acc_ref):
    acc_ref[...] = jnp.zeros_like(acc_ref)

    def body(c, carry):
        xn = xn_ref[...]
        g = _dot(xn, wg_ref[c])
        u = _dot(xn, wu_ref[c])
        a = (g * jax.nn.sigmoid(g) * u).astype(BF16)
        acc_ref[...] += _dot(a, wo_ref[c])
        return carry

    lax.fori_loop(0, N_FF_CHUNKS, body, 0)


def _ffn1_mix_in_kernel(
        npad_ref,
        x_ref, zc_in_ref, fc_in_ref,
        g1_ref, wg_ref, wu_ref, wo_ref,
        gmix_ref, wqkvT_ref, wfT_ref, wbcu_ref, bf_ref, tri_ref,
        gq_ref, gk_ref, cw_ref, gconv_ref,
        h1_ref, qT_ref, k_ref, vT_ref, f_ref, oconv_ref, zc_out_ref, fc_out_ref,
        xn_ref, acc_ref, zc_ref, fc_ref):
    t = pl.program_id(1)
    tm = x_ref.shape[1]

    @pl.when(t == 0)
    def _():
        zc_ref[...] = zc_in_ref[...]
        fc_ref[...] = fc_in_ref[...]

    x = x_ref[0]
    xn_ref[...] = _rmsnorm_rows(x, g1_ref[...]).astype(BF16)
    _swiglu(xn_ref, wg_ref, wu_ref, wo_ref, acc_ref)
    h1 = x + 0.5 * acc_ref[...]
    h1_ref[0] = h1

    xn2 = _rmsnorm_rows(h1, gmix_ref[...]).astype(BF16)
    qkvT = _dot_nt(wqkvT_ref[...], xn2)
    fT = _dot_nt(wfT_ref[...], xn2)
    bcu = _dot(xn2, wbcu_ref[...])

    fl = fT + bf_ref[...]
    logf = jnp.minimum(fl, 0.0) - jnp.log(1.0 + jnp.exp(-jnp.abs(fl)))
    pos = t * tm + lax.broadcasted_iota(jnp.int32, logf.shape, 1)
    logf = jnp.where(pos >= npad_ref[0], logf, 0.0)
    tri = tri_ref[...]
    hi, mid, lo = _split3(logf)
    csum = _dot(hi, tri) + _dot(mid, tri) + _dot(lo, tri)
    f_all = csum + jnp.concatenate([fc_ref[...]] * (tm // LANES), axis=1)
    fc_new = jnp.broadcast_to(f_all[:, tm - 1:tm], fc_ref.shape)
    fc_ref[...] = fc_new
    fc_out_ref[...] = fc_new
    f_ref[0] = f_all[:HEADS]

    q3 = qkvT[0:D_ATTN].reshape(HEADS, HEAD_DIM, tm)
    k3 = qkvT[D_ATTN:2 * D_ATTN].reshape(HEADS, HEAD_DIM, tm)
    v3 = qkvT[2 * D_ATTN:3 * D_ATTN].reshape(HEADS, HEAD_DIM, tm)
    scale = HEAD_DIM ** -0.5
    qn = q3 * lax.rsqrt(jnp.mean(q3 * q3, axis=1, keepdims=True) + EPS) * (gq_ref[...] * scale)[None]
    kn = k3 * lax.rsqrt(jnp.mean(k3 * k3, axis=1, keepdims=True) + EPS) * gk_ref[...][None]

    f_hi, f_mid, f_lo = (p.astype(F32) for p in _split3(f_all))
    row = lax.broadcasted_iota(jnp.int32, (SUBLANES, tm), 0)
    pad_rows = jnp.zeros((AUG_DIM - HEAD_DIM - SUBLANES, tm), F32)
    for h in range(HEADS):
        fh = jnp.broadcast_to(f_hi[h:h + 1], (SUBLANES, tm))
        fm = jnp.broadcast_to(f_mid[h:h + 1], (SUBLANES, tm))
        fo = jnp.broadcast_to(f_lo[h:h + 1], (SUBLANES, tm))
        ones_mid = jnp.where(row < 6, 1.0, 0.0)
        aug_q = jnp.where(row == 0, fh, jnp.where(row == 1, fm, jnp.where(row == 2, fo, ones_mid)))
        aug_k = jnp.where(row == 3, -fh, jnp.where(row == 4, -fm, jnp.where(row == 5, -fo, ones_mid)))
        qT_ref[0, h] = jnp.concatenate([qn[h], aug_q, pad_rows], axis=0).astype(BF16)
        k_aug_t = jnp.concatenate([kn[h], aug_k, pad_rows], axis=0)
        k_ref[0, h] = k_aug_t.T.astype(BF16)
        vT_ref[0, h] = v3[h].astype(BF16)

    gate_b = bcu[:, 0:D_CONV]
    z = bcu[:, D_CONV:2 * D_CONV] * bcu[:, 2 * D_CONV:3 * D_CONV]
    zc = zc_ref[...]
    rowz = lax.broadcasted_iota(jnp.int32, z.shape, 0)
    prev1 = jnp.broadcast_to(zc[7:8], z.shape)
    prev2 = jnp.broadcast_to(zc[6:7], z.shape)
    z1 = jnp.where(rowz == 0, prev1, pltpu.roll(z, 1, axis=0))
    z2 = jnp.where(rowz == 0, prev2, jnp.where(rowz == 1, prev1, pltpu.roll(z, 2, axis=0)))
    cw = cw_ref[...]
    y = cw[0:1] * z2 + cw[1:2] * z1 + cw[2:3] * z
    oconv_ref[0] = _rmsnorm_rows(gate_b * y, gconv_ref[...]).astype(BF16)
    zc_new = z[tm - SUBLANES:tm]
    zc_ref[...] = zc_new
    zc_out_ref[...] = zc_new


def _const_spec(shape):
    nd = len(shape)
    return pl.BlockSpec(shape, lambda *_: (0,) * nd, pipeline_mode=pl.Buffered(1))


def _ffn1_mix_in(x, n_pad, zc_in, fc_in, p, tm):
    nb, seq, _ = x.shape
    nt = seq // tm
    tri = jnp.triu(jnp.ones((tm, tm), BF16))
    gq_t = jnp.broadcast_to(p["gq"][:, None], (HEAD_DIM, tm))
    gk_t = jnp.broadcast_to(p["gk"][:, None], (HEAD_DIM, tm))

    def tile3(last):
        return pl.BlockSpec((1, tm, last), lambda b, t, *_: (b, t, 0))

    in_specs = [
        tile3(D_MODEL),
        _const_spec((SUBLANES, D_CONV)), _const_spec((F_ROWS, LANES)),
        _const_spec((1, D_MODEL)),
        _const_spec((N_FF_CHUNKS, D_MODEL, FF_CHUNK)), _const_spec((N_FF_CHUNKS, D_MODEL, FF_CHUNK)),
        _const_spec((N_FF_CHUNKS, FF_CHUNK, D_MODEL)),
        _const_spec((1, D_MODEL)), _const_spec((3 * D_ATTN, D_MODEL)), _const_spec((F_ROWS, D_MODEL)),
        _const_spec((D_MODEL, 3 * D_CONV)), _const_spec((F_ROWS, 1)), _const_spec((tm, tm)),
        _const_spec((HEAD_DIM, tm)), _const_spec((HEAD_DIM, tm)),
        _const_spec((SUBLANES, D_CONV)), _const_spec((1, D_CONV)),
    ]
    out_shape = [
        jax.ShapeDtypeStruct((nb, seq, D_MODEL), F32),
        jax.ShapeDtypeStruct((nb, HEADS, AUG_DIM, seq), BF16),
        jax.ShapeDtypeStruct((nb, HEADS, seq, AUG_DIM), BF16),
        jax.ShapeDtypeStruct((nb, HEADS, HEAD_DIM, seq), BF16),
        jax.ShapeDtypeStruct((nb, HEADS, seq), F32),
        jax.ShapeDtypeStruct((nb, seq, D_CONV), BF16),
        jax.ShapeDtypeStruct((SUBLANES, D_CONV), F32),
        jax.ShapeDtypeStruct((F_ROWS, LANES), F32),
    ]
    out_specs = [
        tile3(D_MODEL),
        pl.BlockSpec((1, HEADS, AUG_DIM, tm), lambda b, t, *_: (b, 0, 0, t)),
        pl.BlockSpec((1, HEADS, tm, AUG_DIM), lambda b, t, *_: (b, 0, t, 0)),
        pl.BlockSpec((1, HEADS, HEAD_DIM, tm), lambda b, t, *_: (b, 0, 0, t)),
        pl.BlockSpec((1, HEADS, tm), lambda b, t, *_: (b, 0, t)),
        tile3(D_CONV),
        pl.BlockSpec((SUBLANES, D_CONV), lambda b, t, *_: (0, 0)),
        pl.BlockSpec((F_ROWS, LANES), lambda b, t, *_: (0, 0)),
    ]
    grid_spec = pltpu.PrefetchScalarGridSpec(
        num_scalar_prefetch=1, grid=(nb, nt), in_specs=in_specs, out_specs=out_specs,
        scratch_shapes=[
            pltpu.VMEM((tm, D_MODEL), BF16),
            pltpu.VMEM((tm, D_MODEL), F32),
            pltpu.VMEM((SUBLANES, D_CONV), F32),
            pltpu.VMEM((F_ROWS, LANES), F32),
        ])
    return pl.pallas_call(
        _ffn1_mix_in_kernel, out_shape=out_shape, grid_spec=grid_spec, name="ffn1_mix_in",
        compiler_params=pltpu.CompilerParams(
            dimension_semantics=("arbitrary", "arbitrary"), vmem_limit_bytes=VMEM_LIMIT_BYTES),
    )(n_pad, x, zc_in, fc_in,
      p["g1"], p["wg1"], p["wu1"], p["wo1"],
      p["gmix"], p["wqkvT"], p["wfT"], p["wbcu"], p["bf"], tri,
      gq_t, gk_t, p["cw"], p["gconv"])


def _fox_attention_kernel(fq0_ref, flast_ref, cb_ref,
                          qT_ref, k_ref, vT_ref, km_ref, vmT_ref,
                          o_ref):
    bh = pl.program_id(0) * HEADS + pl.program_id(1)
    nq = qT_ref.shape[3] // Q_BLOCK
    c_bound = cb_ref[0]
    key_idx = lax.broadcasted_iota(jnp.int32, (K_BLOCK, Q_BLOCK), 0)
    qry_idx = lax.broadcasted_iota(jnp.int32, (K_BLOCK, Q_BLOCK), 1)
    causal = key_idx <= qry_idx

    def online_step(carry, s_t, v_t):
        m, l, acc = carry
        m_new = jnp.maximum(m, jnp.max(s_t, axis=0, keepdims=True))
        alpha = jnp.exp(m - m_new)
        p = jnp.exp(s_t - m_new)
        l = alpha * l + jnp.sum(p, axis=0, keepdims=True)
        acc = alpha * acc + _dot(v_t, p.astype(BF16))
        return m_new, l, acc

    def q_block(i, _):
        q0 = pl.multiple_of(i * Q_BLOCK, Q_BLOCK)
        q_t = qT_ref[0, 0, :, pl.ds(q0, Q_BLOCK)]

        s_m = _dot(km_ref[0], q_t)
        m = jnp.max(s_m, axis=0, keepdims=True)
        p_m = jnp.exp(s_m - m)
        l = jnp.sum(p_m, axis=0, keepdims=True)
        acc = _dot(vmT_ref[0], p_m.astype(BF16))

        s_d = _dot(k_ref[0, 0, pl.ds(q0, K_BLOCK), :], q_t)
        s_d = jnp.where(causal, s_d, -1e30)
        carry = online_step((m, l, acc), s_d, vT_ref[0, 0, :, pl.ds(q0, K_BLOCK)])

        fq0 = fq0_ref[bh, i]

        def cond(state):
            j = state[0]
            jc = jnp.maximum(j, 0)
            return jnp.logical_and(j >= 0, fq0 - flast_ref[bh, jc] + c_bound >= EXP_UNDERFLOW)

        def body(state):
            j, m, l, acc = state
            k0 = pl.multiple_of(j * K_BLOCK, K_BLOCK)
            s_t = _dot(k_ref[0, 0, pl.ds(k0, K_BLOCK), :], q_t)
            m, l, acc = online_step((m, l, acc), s_t, vT_ref[0, 0, :, pl.ds(k0, K_BLOCK)])
            return j - 1, m, l, acc

        _, m, l, acc = lax.while_loop(cond, body, (i - 1,) + carry)
        o_ref[0, 0, :, pl.ds(q0, Q_BLOCK)] = acc / l
        return 0

    lax.fori_loop(0, nq, q_block, 0)


def _fox_attention(qT, k, vT, k_meta, vT_meta, fq0, flast, c_bound):
    nb, _, _, seq = qT.shape
    smem = pl.BlockSpec(memory_space=pltpu.SMEM)
    return pl.pallas_call(
        _fox_attention_kernel,
        out_shape=jax.ShapeDtypeStruct((nb, HEADS, HEAD_DIM, seq), F32),
        grid=(nb, HEADS),
        in_specs=[
            smem, smem, smem,
            pl.BlockSpec((1, 1, AUG_DIM, seq), lambda b, h: (b, h, 0, 0)),
            pl.BlockSpec((1, 1, seq, AUG_DIM), lambda b, h: (b, h, 0, 0)),
            pl.BlockSpec((1, 1, HEAD_DIM, seq), lambda b, h: (b, h, 0, 0)),
            pl.BlockSpec((1, N_META, AUG_DIM), lambda b, h: (h, 0, 0)),
            pl.BlockSpec((1, HEAD_DIM, N_META), lambda b, h: (h, 0, 0)),
        ],
        out_specs=pl.BlockSpec((1, 1, HEAD_DIM, seq), lambda b, h: (b, h, 0, 0)),
        name="fox_attention",
        compiler_params=pltpu.CompilerParams(
            dimension_semantics=("arbitrary", "arbitrary"), vmem_limit_bytes=VMEM_LIMIT_BYTES),
    )(fq0, flast, c_bound, qT, k, vT, k_meta, vT_meta)


def _mix_out_ffn2_kernel(h1_ref, oT_ref, oconv_ref,
                         gattn_ref, woa_ref, woc_ref,
                         g2_ref, wg_ref, wu_ref, wo_ref, gfin_ref,
                         out_ref,
                         xn_ref, acc_ref):
    o_t = oT_ref[0]
    ms = jnp.mean(o_t * o_t, axis=0, keepdims=True)
    o_n = (o_t * lax.rsqrt(ms + EPS) * gattn_ref[...]).T.astype(BF16)
    mix = _dot(o_n, woa_ref[...]) + _dot(oconv_ref[0], woc_ref[...])
    h2 = h1_ref[0] + mix
    xn_ref[...] = _rmsnorm_rows(h2, g2_ref[...]).astype(BF16)
    _swiglu(xn_ref, wg_ref, wu_ref, wo_ref, acc_ref)
    h3 = h2 + 0.5 * acc_ref[...]
    out_ref[0] = _rmsnorm_rows(h3, gfin_ref[...])


def _mix_out_ffn2(h1, o_t, oconv, p, tm):
    nb, seq, _ = h1.shape
    gattn_t = jnp.broadcast_to(p["gattn"][:, None], (D_ATTN, tm))
    in_specs = [
        pl.BlockSpec((1, tm, D_MODEL), lambda b, t: (b, t, 0)),
        pl.BlockSpec((1, D_ATTN, tm), lambda b, t: (b, 0, t)),
        pl.BlockSpec((1, tm, D_CONV), lambda b, t: (b, t, 0)),
        _const_spec((D_ATTN, tm)), _const_spec((D_ATTN, D_MODEL)), _const_spec((D_CONV, D_MODEL)),
        _const_spec((1, D_MODEL)),
        _const_spec((N_FF_CHUNKS, D_MODEL, FF_CHUNK)), _const_spec((N_FF_CHUNKS, D_MODEL, FF_CHUNK)),
        _const_spec((N_FF_CHUNKS, FF_CHUNK, D_MODEL)),
        _const_spec((1, D_MODEL)),
    ]
    return pl.pallas_call(
        _mix_out_ffn2_kernel,
        out_shape=jax.ShapeDtypeStruct((nb, seq, D_MODEL), F32),
        grid=(nb, seq // tm),
        in_specs=in_specs,
        out_specs=pl.BlockSpec((1, tm, D_MODEL), lambda b, t: (b, t, 0)),
        scratch_shapes=[pltpu.VMEM((tm, D_MODEL), BF16), pltpu.VMEM((tm, D_MODEL), F32)],
        name="mix_out_ffn2",
        compiler_params=pltpu.CompilerParams(
            dimension_semantics=("arbitrary", "arbitrary"), vmem_limit_bytes=VMEM_LIMIT_BYTES),
    )(h1, o_t, oconv, gattn_t, p["woa"], p["woc"], p["g2"], p["wg2"], p["wu2"], p["wo2"], p["gfin"])


def _ffn_weights(w_in, w_out):
    def chunks(w):
        return w.reshape(D_MODEL, N_FF_CHUNKS, FF_CHUNK).transpose(1, 0, 2).astype(BF16)
    return chunks(w_in[:, :D_FF]), chunks(w_in[:, D_FF:]), w_out.reshape(N_FF_CHUNKS, FF_CHUNK, D_MODEL).astype(BF16)


def kernel(x, meta_tokens, ffn1_norm, ffn1_w_in, ffn1_w_out, mix_norm, w_mix_in, b_forget, q_norm, k_norm, conv_w, attn_out_norm, conv_out_norm, w_mix_out, ffn2_norm, ffn2_w_in, ffn2_w_out, final_norm):
    nb, seq, _ = x.shape
    wg1, wu1, wo1 = _ffn_weights(ffn1_w_in[0], ffn1_w_out[0])
    wg2, wu2, wo2 = _ffn_weights(ffn2_w_in[0], ffn2_w_out[0])
    wmix = w_mix_in[0]
    n_qkv = 3 * D_ATTN
    p = {
        "g1": ffn1_norm, "wg1": wg1, "wu1": wu1, "wo1": wo1,
        "gmix": mix_norm,
        "wqkvT": wmix[:, :n_qkv].T.astype(BF16),
        "wfT": jnp.pad(wmix[:, n_qkv:n_qkv + HEADS].T, ((0, F_ROWS - HEADS), (0, 0))).astype(BF16),
        "wbcu": wmix[:, n_qkv + HEADS:].astype(BF16),
        "bf": jnp.pad(b_forget[0], (0, F_ROWS - HEADS))[:, None],
        "gq": q_norm[0], "gk": k_norm[0],
        "cw": jnp.pad(conv_w[0], ((0, SUBLANES - CONV_WIDTH), (0, 0))),
        "gconv": conv_out_norm,
        "gattn": attn_out_norm[0],
        "woa": w_mix_out[0, :D_ATTN].astype(BF16), "woc": w_mix_out[0, D_ATTN:].astype(BF16),
        "g2": ffn2_norm, "wg2": wg2, "wu2": wu2, "wo2": wo2, "gfin": final_norm,
    }

    n_pad = META_TILE - N_META
    meta = jnp.pad(meta_tokens.astype(x.dtype), ((n_pad, 0), (0, 0)))[None]
    zeros_zc = jnp.zeros((SUBLANES, D_CONV), F32)
    zeros_fc = jnp.zeros((F_ROWS, LANES), F32)
    _, _, k_m, vT_m, _, _, zc_m, fc_m = _ffn1_mix_in(
        meta, jnp.full((1,), n_pad, jnp.int32), zeros_zc, zeros_fc, p, META_TILE)
    k_meta = k_m[0, :, n_pad:, :]
    vT_meta = vT_m[0, :, :, n_pad:]

    h1, qT, k, vT, f_cum, oconv, _, _ = _ffn1_mix_in(
        x, jnp.zeros((1,), jnp.int32), zc_m, fc_m, p, TOKEN_TILE)

    fq0 = f_cum[:, :, 0::Q_BLOCK].reshape(nb * HEADS, seq // Q_BLOCK)
    flast = f_cum[:, :, K_BLOCK - 1::K_BLOCK].reshape(nb * HEADS, seq // K_BLOCK)
    c_bound = (16.0 * 1.02 * jnp.max(jnp.abs(q_norm)) * jnp.max(jnp.abs(k_norm)) + 1.0).reshape(1).astype(F32)

    o_t = _fox_attention(qT, k, vT, k_meta, vT_meta, fq0, flast, c_bound)
    o_t = o_t.reshape(nb, D_ATTN, seq)
    return _mix_out_ffn2(h1, o_t, oconv, p, TOKEN_TILE)
```

```python
import jax
import jax.numpy as jnp
from jax import lax
from jax.experimental import pallas as pl
from jax.experimental.pallas import tpu as pltpu

D_MODEL = 1024
N_META = 16
D_ATTN = 512
D_CONV = 512
HEADS = 8
HEAD_DIM = 64
CONV_WIDTH = 3
D_FF = 2816
EPS = 1e-6

F32 = jnp.float32
BF16 = jnp.bfloat16

LANES = 128
SUBLANES = 8
MXU_DIM = 256
VMEM_LIMIT_BYTES = 60000 * 1024

TOKEN_TILE = 512
META_TILE = LANES
FF_CHUNK = MXU_DIM
N_FF_CHUNKS = D_FF // FF_CHUNK
Q_BLOCK = MXU_DIM
K_BLOCK = MXU_DIM
AUG_DIM = LANES
HEAD_GROUP = 4
MASKED = -1e30
LOG2E = 1.4426950408889634
F_ROWS = 2 * SUBLANES

EXP_UNDERFLOW = -104.0
BOUNDED_LOGIT_MAX = 120.0

assert D_FF % FF_CHUNK == 0 and HEADS % HEAD_GROUP == 0


def _dot(a, b):
    return jnp.dot(a, b, preferred_element_type=F32)


def _dot_nt(a, b):
    return lax.dot_general(a, b, (((1,), (1,)), ((), ())), preferred_element_type=F32)


def _rmsnorm_rows(x, gain):
    ms = jnp.mean(x * x, axis=-1, keepdims=True)
    return x * lax.rsqrt(ms + EPS) * gain


def _split3(x):
    hi = x.astype(BF16)
    r1 = x - hi.astype(F32)
    mid = r1.astype(BF16)
    lo = (r1 - mid.astype(F32)).astype(BF16)
    return hi, mid, lo


def _swiglu(xn_ref, wg_ref, wu_ref, wo_ref, acc_ref):
    acc_ref[...] = jnp.zeros_like(acc_ref)

    def body(c, carry):
        xn = xn_ref[...]
        g = _dot(xn, wg_ref[c])
        u = _dot(xn, wu_ref[c])
        a = (g * jax.nn.sigmoid(g) * u).astype(BF16)
        acc_ref[...] += _dot(a, wo_ref[c])
        return carry

    lax.fori_loop(0, N_FF_CHUNKS, body, 0)


def _ffn1_mix_in_kernel(
        npad_ref,
        x_ref, zc_in_ref, fc_in_ref,
        g1_ref, wg_ref, wu_ref, wo_ref,
        gmix_ref, wqkvT_ref, wfT_ref, wbcu_ref, bf_ref, tri_ref,
        gq_ref, gk_ref, cw_ref, gconv_ref,
        h1_ref, qT_ref, k_ref, vT_ref, f_ref, oconv_ref, zc_out_ref, fc_out_ref,
        xn_ref, acc_ref, zc_ref, fc_ref):
    t = pl.program_id(1)
    tm = x_ref.shape[1]

    @pl.when(t == 0)
    def _():
        zc_ref[...] = zc_in_ref[...]
        fc_ref[...] = fc_in_ref[...]

    x = x_ref[0]
    xn_ref[...] = _rmsnorm_rows(x, g1_ref[...]).astype(BF16)
    _swiglu(xn_ref, wg_ref, wu_ref, wo_ref, acc_ref)
    h1 = x + 0.5 * acc_ref[...]
    h1_ref[0] = h1

    xn2 = _rmsnorm_rows(h1, gmix_ref[...]).astype(BF16)
    qkvT = _dot_nt(wqkvT_ref[...], xn2)
    fT = _dot_nt(wfT_ref[...], xn2)
    bcu = _dot(xn2, wbcu_ref[...])

    fl = fT + bf_ref[...]
    logf = jnp.minimum(fl, 0.0) - jnp.log(1.0 + jnp.exp(-jnp.abs(fl)))
    pos = t * tm + lax.broadcasted_iota(jnp.int32, logf.shape, 1)
    logf = jnp.where(pos >= npad_ref[0], logf, 0.0)
    tri = tri_ref[...]
    hi, mid, lo = _split3(logf)
    csum = _dot(hi, tri) + _dot(mid, tri) + _dot(lo, tri)
    f_all = csum + jnp.concatenate([fc_ref[...]] * (tm // LANES), axis=1)
    fc_new = jnp.broadcast_to(f_all[:, tm - 1:tm], fc_ref.shape)
    fc_ref[...] = fc_new
    fc_out_ref[...] = fc_new
    f_ref[0] = f_all[:HEADS]

    q3 = qkvT[0:D_ATTN].reshape(HEADS, HEAD_DIM, tm)
    k3 = qkvT[D_ATTN:2 * D_ATTN].reshape(HEADS, HEAD_DIM, tm)
    v3 = qkvT[2 * D_ATTN:3 * D_ATTN].reshape(HEADS, HEAD_DIM, tm)
    scale = HEAD_DIM ** -0.5 * LOG2E
    qn = q3 * lax.rsqrt(jnp.mean(q3 * q3, axis=1, keepdims=True) + EPS) * (gq_ref[...] * scale)[None]
    kn = k3 * lax.rsqrt(jnp.mean(k3 * k3, axis=1, keepdims=True) + EPS) * gk_ref[...][None]

    f_hi, f_mid, f_lo = (p.astype(F32) for p in _split3(f_all * LOG2E))
    row = lax.broadcasted_iota(jnp.int32, (SUBLANES, tm), 0)
    pad_rows = jnp.zeros((AUG_DIM - HEAD_DIM - SUBLANES, tm), F32)
    for h in range(HEADS):
        fh = jnp.broadcast_to(f_hi[h:h + 1], (SUBLANES, tm))
        fm = jnp.broadcast_to(f_mid[h:h + 1], (SUBLANES, tm))
        fo = jnp.broadcast_to(f_lo[h:h + 1], (SUBLANES, tm))
        ones_mid = jnp.where(row < 6, 1.0, 0.0)
        aug_q = jnp.where(row == 0, fh, jnp.where(row == 1, fm, jnp.where(row == 2, fo, ones_mid)))
        aug_k = jnp.where(row == 3, -fh, jnp.where(row == 4, -fm, jnp.where(row == 5, -fo, ones_mid)))
        qT_ref[0, h] = jnp.concatenate([qn[h], aug_q, pad_rows], axis=0).astype(BF16)
        k_aug_t = jnp.concatenate([kn[h], aug_k, pad_rows], axis=0)
        k_ref[0, h] = k_aug_t.T.astype(BF16)
        vT_ref[0, h] = v3[h].astype(BF16)

    gate_b = bcu[:, 0:D_CONV]
    z = bcu[:, D_CONV:2 * D_CONV] * bcu[:, 2 * D_CONV:3 * D_CONV]
    zc = zc_ref[...]
    rowz = lax.broadcasted_iota(jnp.int32, z.shape, 0)
    prev1 = jnp.broadcast_to(zc[7:8], z.shape)
    prev2 = jnp.broadcast_to(zc[6:7], z.shape)
    z1 = jnp.where(rowz == 0, prev1, pltpu.roll(z, 1, axis=0))
    z2 = jnp.where(rowz == 0, prev2, jnp.where(rowz == 1, prev1, pltpu.roll(z, 2, axis=0)))
    cw = cw_ref[...]
    y = cw[0:1] * z2 + cw[1:2] * z1 + cw[2:3] * z
    oconv_ref[0] = _rmsnorm_rows(gate_b * y, gconv_ref[...]).astype(BF16)
    zc_new = z[tm - SUBLANES:tm]
    zc_ref[...] = zc_new
    zc_out_ref[...] = zc_new


def _const_spec(shape):
    nd = len(shape)
    return pl.BlockSpec(shape, lambda *_: (0,) * nd, pipeline_mode=pl.Buffered(1))


def _ffn1_mix_in(x, n_pad, zc_in, fc_in, p, tm):
    nb, seq, _ = x.shape
    nt = seq // tm
    tri = jnp.triu(jnp.ones((tm, tm), BF16))
    gq_t = jnp.broadcast_to(p["gq"][:, None], (HEAD_DIM, tm))
    gk_t = jnp.broadcast_to(p["gk"][:, None], (HEAD_DIM, tm))

    def tile3(last):
        return pl.BlockSpec((1, tm, last), lambda b, t, *_: (b, t, 0))

    in_specs = [
        tile3(D_MODEL),
        _const_spec((SUBLANES, D_CONV)), _const_spec((F_ROWS, LANES)),
        _const_spec((1, D_MODEL)),
        _const_spec((N_FF_CHUNKS, D_MODEL, FF_CHUNK)), _const_spec((N_FF_CHUNKS, D_MODEL, FF_CHUNK)),
        _const_spec((N_FF_CHUNKS, FF_CHUNK, D_MODEL)),
        _const_spec((1, D_MODEL)), _const_spec((3 * D_ATTN, D_MODEL)), _const_spec((F_ROWS, D_MODEL)),
        _const_spec((D_MODEL, 3 * D_CONV)), _const_spec((F_ROWS, 1)), _const_spec((tm, tm)),
        _const_spec((HEAD_DIM, tm)), _const_spec((HEAD_DIM, tm)),
        _const_spec((SUBLANES, D_CONV)), _const_spec((1, D_CONV)),
    ]
    out_shape = [
        jax.ShapeDtypeStruct((nb, seq, D_MODEL), F32),
        jax.ShapeDtypeStruct((nb, HEADS, AUG_DIM, seq), BF16),
        jax.ShapeDtypeStruct((nb, HEADS, seq, AUG_DIM), BF16),
        jax.ShapeDtypeStruct((nb, HEADS, HEAD_DIM, seq), BF16),
        jax.ShapeDtypeStruct((nb, HEADS, seq), F32),
        jax.ShapeDtypeStruct((nb, seq, D_CONV), BF16),
        jax.ShapeDtypeStruct((SUBLANES, D_CONV), F32),
        jax.ShapeDtypeStruct((F_ROWS, LANES), F32),
    ]
    out_specs = [
        tile3(D_MODEL),
        pl.BlockSpec((1, HEADS, AUG_DIM, tm), lambda b, t, *_: (b, 0, 0, t)),
        pl.BlockSpec((1, HEADS, tm, AUG_DIM), lambda b, t, *_: (b, 0, t, 0)),
        pl.BlockSpec((1, HEADS, HEAD_DIM, tm), lambda b, t, *_: (b, 0, 0, t)),
        pl.BlockSpec((1, HEADS, tm), lambda b, t, *_: (b, 0, t)),
        tile3(D_CONV),
        pl.BlockSpec((SUBLANES, D_CONV), lambda b, t, *_: (0, 0)),
        pl.BlockSpec((F_ROWS, LANES), lambda b, t, *_: (0, 0)),
    ]
    grid_spec = pltpu.PrefetchScalarGridSpec(
        num_scalar_prefetch=1, grid=(nb, nt), in_specs=in_specs, out_specs=out_specs,
        scratch_shapes=[
            pltpu.VMEM((tm, D_MODEL), BF16),
            pltpu.VMEM((tm, D_MODEL), F32),
            pltpu.VMEM((SUBLANES, D_CONV), F32),
            pltpu.VMEM((F_ROWS, LANES), F32),
        ])
    return pl.pallas_call(
        _ffn1_mix_in_kernel, out_shape=out_shape, grid_spec=grid_spec, name="ffn1_mix_in",
        compiler_params=pltpu.CompilerParams(
            dimension_semantics=("arbitrary", "arbitrary"), vmem_limit_bytes=VMEM_LIMIT_BYTES),
    )(n_pad, x, zc_in, fc_in,
      p["g1"], p["wg1"], p["wu1"], p["wo1"],
      p["gmix"], p["wqkvT"], p["wfT"], p["wbcu"], p["bf"], tri,
      gq_t, gk_t, p["cw"], p["gconv"])


def _fox_attention_kernel(fq0_ref, flast_ref, cb_ref,
                          qT_ref, k_ref, vT_ref, km_ref, vmT_ref,
                          o_ref,
                          acc_ref, l_ref):
    bh0 = pl.program_id(0) * HEADS + pl.program_id(1) * HEAD_GROUP
    heads = range(HEAD_GROUP)
    nq = qT_ref.shape[3] // Q_BLOCK
    c_bound = cb_ref[0]
    key_idx = lax.broadcasted_iota(jnp.int32, (K_BLOCK, Q_BLOCK), 0)
    qry_idx = lax.broadcasted_iota(jnp.int32, (K_BLOCK, Q_BLOCK), 1)
    causal = key_idx <= qry_idx

    def block_live(g, i, j):
        return fq0_ref[bh0 + g, i] - flast_ref[bh0 + g, jnp.maximum(j, 0)] + c_bound >= EXP_UNDERFLOW

    def sublane_partial_sum(p):
        return jnp.sum(p.reshape(p.shape[0] // SUBLANES, SUBLANES, p.shape[1]), axis=0)

    def bounded_tile(k_blk, v_t, q_t, masked):
        s = _dot(k_blk, q_t)
        if masked:
            s = jnp.where(causal, s, MASKED)
        p = jnp.exp2(s)
        return sublane_partial_sum(p), _dot(v_t, p.astype(BF16))

    def bounded_q_block(i, with_prev):
        q0 = pl.multiple_of(i * Q_BLOCK, Q_BLOCK)
        q_t = [qT_ref[0, g, :, pl.ds(q0, Q_BLOCK)] for g in heads]
        p0 = pl.multiple_of(jnp.maximum(i - 1, 0) * K_BLOCK, K_BLOCK)
        s_m = [_dot(km_ref[g], q_t[g]) for g in heads]
        s_d = [_dot(k_ref[0, g, pl.ds(q0, K_BLOCK), :], q_t[g]) for g in heads]
        s_p = [_dot(k_ref[0, g, pl.ds(p0, K_BLOCK), :], q_t[g]) for g in heads] if with_prev else None
        p_m = [jnp.exp2(s) for s in s_m]
        p_d = [jnp.exp2(jnp.where(causal, s, MASKED)) for s in s_d]
        p_p = [jnp.exp2(s) for s in s_p] if with_prev else None
        for g in heads:
            l_g = sublane_partial_sum(p_m[g]) + sublane_partial_sum(p_d[g])
            acc_g = (_dot(vmT_ref[g], p_m[g].astype(BF16))
                     + _dot(vT_ref[0, g, :, pl.ds(q0, K_BLOCK)], p_d[g].astype(BF16)))
            if with_prev:
                l_g = l_g + sublane_partial_sum(p_p[g])
                acc_g = acc_g + _dot(vT_ref[0, g, :, pl.ds(p0, K_BLOCK)], p_p[g].astype(BF16))
            l_ref[g] = l_g
            acc_ref[g] = acc_g

        def cond(j):
            live = block_live(0, i, j)
            for g in heads[1:]:
                live = jnp.logical_or(live, block_live(g, i, j))
            return jnp.logical_and(j >= 0, live)

        def body(j):
            k0 = pl.multiple_of(j * K_BLOCK, K_BLOCK)
            for g in heads:
                l_j, acc_j = bounded_tile(k_ref[0, g, pl.ds(k0, K_BLOCK), :],
                                          vT_ref[0, g, :, pl.ds(k0, K_BLOCK)], q_t[g], False)
                l_ref[g] += l_j
                acc_ref[g] += acc_j
            return j - 1

        lax.while_loop(cond, body, i - 2)
        for g in heads:
            l_tot = jnp.sum(l_ref[g], axis=0, keepdims=True)
            o_ref[0, g, :, pl.ds(q0, Q_BLOCK)] = (acc_ref[g] / l_tot).astype(o_ref.dtype)

    @pl.when(c_bound <= BOUNDED_LOGIT_MAX)
    def _():
        bounded_q_block(0, False)

        def step(i, carry):
            bounded_q_block(i, True)
            return carry

        lax.fori_loop(1, nq, step, 0)

    def online_q_block(g, i):
        q0 = pl.multiple_of(i * Q_BLOCK, Q_BLOCK)
        q_t = qT_ref[0, g, :, pl.ds(q0, Q_BLOCK)]
        s_m = _dot(km_ref[g], q_t)
        m = jnp.max(s_m, axis=0, keepdims=True)
        p_m = jnp.exp2(s_m - m)
        l = jnp.sum(p_m, axis=0, keepdims=True)
        acc = _dot(vmT_ref[g], p_m.astype(BF16))

        def online_step(state, k0, masked):
            m, l, acc = state
            s = _dot(k_ref[0, g, pl.ds(k0, K_BLOCK), :], q_t)
            if masked:
                s = jnp.where(causal, s, MASKED)
            m_new = jnp.maximum(m, jnp.max(s, axis=0, keepdims=True))
            alpha = jnp.exp2(m - m_new)
            p = jnp.exp2(s - m_new)
            l = alpha * l + jnp.sum(p, axis=0, keepdims=True)
            acc = alpha * acc + _dot(vT_ref[0, g, :, pl.ds(k0, K_BLOCK)], p.astype(BF16))
            return m_new, l, acc

        state = online_step((m, l, acc), q0, True)

        def cond(carry):
            return jnp.logical_and(carry[0] >= 0, block_live(g, i, carry[0]))

        def body(carry):
            j = carry[0]
            return (j - 1,) + online_step(carry[1:], pl.multiple_of(j * K_BLOCK, K_BLOCK), False)

        _, m, l, acc = lax.while_loop(cond, body, (i - 1,) + state)
        o_ref[0, g, :, pl.ds(q0, Q_BLOCK)] = (acc / l).astype(o_ref.dtype)

    @pl.when(c_bound > BOUNDED_LOGIT_MAX)
    def _():
        def head_loop(g, carry):
            def step(i, inner):
                online_q_block(g, i)
                return inner

            lax.fori_loop(0, nq, step, 0)
            return carry

        lax.fori_loop(0, HEAD_GROUP, head_loop, 0)


def _fox_attention(qT, k, vT, k_meta, vT_meta, fq0, flast, c_bound):
    nb, _, _, seq = qT.shape
    smem = pl.BlockSpec(memory_space=pltpu.SMEM)
    hg = HEAD_GROUP
    return pl.pallas_call(
        _fox_attention_kernel,
        out_shape=jax.ShapeDtypeStruct((nb, HEADS, HEAD_DIM, seq), BF16),
        grid=(nb, HEADS // hg),
        in_specs=[
            smem, smem, smem,
            pl.BlockSpec((1, hg, AUG_DIM, seq), lambda b, h: (b, h, 0, 0)),
            pl.BlockSpec((1, hg, seq, AUG_DIM), lambda b, h: (b, h, 0, 0)),
            pl.BlockSpec((1, hg, HEAD_DIM, seq), lambda b, h: (b, h, 0, 0)),
            pl.BlockSpec((hg, N_META, AUG_DIM), lambda b, h: (h, 0, 0)),
            pl.BlockSpec((hg, HEAD_DIM, N_META), lambda b, h: (h, 0, 0)),
        ],
        out_specs=pl.BlockSpec((1, hg, HEAD_DIM, seq), lambda b, h: (b, h, 0, 0)),
        scratch_shapes=[pltpu.VMEM((hg, HEAD_DIM, Q_BLOCK), F32),
                        pltpu.VMEM((hg, SUBLANES, Q_BLOCK), F32)],
        name="fox_attention",
        compiler_params=pltpu.CompilerParams(
            dimension_semantics=("arbitrary", "arbitrary"), vmem_limit_bytes=VMEM_LIMIT_BYTES),
    )(fq0, flast, c_bound, qT, k, vT, k_meta, vT_meta)


def _mix_out_ffn2_kernel(h1_ref, oT_ref, oconv_ref,
                         gattn_ref, woa_ref, woc_ref,
                         g2_ref, wg_ref, wu_ref, wo_ref, gfin_ref,
                         out_ref,
                         xn_ref, acc_ref):
    o_t = oT_ref[0].astype(F32)
    ms = jnp.mean(o_t * o_t, axis=0, keepdims=True)
    o_n = (o_t * lax.rsqrt(ms + EPS) * gattn_ref[...]).T.astype(BF16)
    mix = _dot(o_n, woa_ref[...]) + _dot(oconv_ref[0], woc_ref[...])
    h2 = h1_ref[0] + mix
    xn_ref[...] = _rmsnorm_rows(h2, g2_ref[...]).astype(BF16)
    _swiglu(xn_ref, wg_ref, wu_ref, wo_ref, acc_ref)
    h3 = h2 + 0.5 * acc_ref[...]
    out_ref[0] = _rmsnorm_rows(h3, gfin_ref[...])


def _mix_out_ffn2(h1, o_t, oconv, p, tm):
    nb, seq, _ = h1.shape
    gattn_t = jnp.broadcast_to(p["gattn"][:, None], (D_ATTN, tm))
    in_specs = [
        pl.BlockSpec((1, tm, D_MODEL), lambda b, t: (b, t, 0)),
        pl.BlockSpec((1, D_ATTN, tm), lambda b, t: (b, 0, t)),
        pl.BlockSpec((1, tm, D_CONV), lambda b, t: (b, t, 0)),
        _const_spec((D_ATTN, tm)), _const_spec((D_ATTN, D_MODEL)), _const_spec((D_CONV, D_MODEL)),
        _const_spec((1, D_MODEL)),
        _const_spec((N_FF_CHUNKS, D_MODEL, FF_CHUNK)), _const_spec((N_FF_CHUNKS, D_MODEL, FF_CHUNK)),
        _const_spec((N_FF_CHUNKS, FF_CHUNK, D_MODEL)),
        _const_spec((1, D_MODEL)),
    ]
    return pl.pallas_call(
        _mix_out_ffn2_kernel,
        out_shape=jax.ShapeDtypeStruct((nb, seq, D_MODEL), F32),
        grid=(nb, seq // tm),
        in_specs=in_specs,
        out_specs=pl.BlockSpec((1, tm, D_MODEL), lambda b, t: (b, t, 0)),
        scratch_shapes=[pltpu.VMEM((tm, D_MODEL), BF16), pltpu.VMEM((tm, D_MODEL), F32)],
        name="mix_out_ffn2",
        compiler_params=pltpu.CompilerParams(
            dimension_semantics=("arbitrary", "arbitrary"), vmem_limit_bytes=VMEM_LIMIT_BYTES),
    )(h1, o_t, oconv, gattn_t, p["woa"], p["woc"], p["g2"], p["wg2"], p["wu2"], p["wo2"], p["gfin"])


def _ffn_weights(w_in, w_out):
    def chunks(w):
        return w.reshape(D_MODEL, N_FF_CHUNKS, FF_CHUNK).transpose(1, 0, 2).astype(BF16)
    return chunks(w_in[:, :D_FF]), chunks(w_in[:, D_FF:]), w_out.reshape(N_FF_CHUNKS, FF_CHUNK, D_MODEL).astype(BF16)


def kernel(x, meta_tokens, ffn1_norm, ffn1_w_in, ffn1_w_out, mix_norm, w_mix_in, b_forget, q_norm, k_norm, conv_w, attn_out_norm, conv_out_norm, w_mix_out, ffn2_norm, ffn2_w_in, ffn2_w_out, final_norm):
    nb, seq, _ = x.shape
    wg1, wu1, wo1 = _ffn_weights(ffn1_w_in[0], ffn1_w_out[0])
    wg2, wu2, wo2 = _ffn_weights(ffn2_w_in[0], ffn2_w_out[0])
    wmix = w_mix_in[0]
    n_qkv = 3 * D_ATTN
    p = {
        "g1": ffn1_norm, "wg1": wg1, "wu1": wu1, "wo1": wo1,
        "gmix": mix_norm,
        "wqkvT": wmix[:, :n_qkv].T.astype(BF16),
        "wfT": jnp.pad(wmix[:, n_qkv:n_qkv + HEADS].T, ((0, F_ROWS - HEADS), (0, 0))).astype(BF16),
        "wbcu": wmix[:, n_qkv + HEADS:].astype(BF16),
        "bf": jnp.pad(b_forget[0], (0, F_ROWS - HEADS))[:, None],
        "gq": q_norm[0], "gk": k_norm[0],
        "cw": jnp.pad(conv_w[0], ((0, SUBLANES - CONV_WIDTH), (0, 0))),
        "gconv": conv_out_norm,
        "gattn": attn_out_norm[0],
        "woa": w_mix_out[0, :D_ATTN].astype(BF16), "woc": w_mix_out[0, D_ATTN:].astype(BF16),
        "g2": ffn2_norm, "wg2": wg2, "wu2": wu2, "wo2": wo2, "gfin": final_norm,
    }

    n_pad = META_TILE - N_META
    meta = jnp.pad(meta_tokens.astype(x.dtype), ((n_pad, 0), (0, 0)))[None]
    zeros_zc = jnp.zeros((SUBLANES, D_CONV), F32)
    zeros_fc = jnp.zeros((F_ROWS, LANES), F32)
    _, _, k_m, vT_m, _, _, zc_m, fc_m = _ffn1_mix_in(
        meta, jnp.full((1,), n_pad, jnp.int32), zeros_zc, zeros_fc, p, META_TILE)
    k_meta = k_m[0, :, n_pad:, :]
    vT_meta = vT_m[0, :, :, n_pad:]

    h1, qT, k, vT, f_cum, oconv, _, _ = _ffn1_mix_in(
        x, jnp.zeros((1,), jnp.int32), zc_m, fc_m, p, TOKEN_TILE)

    fq0 = f_cum[:, :, 0::Q_BLOCK].reshape(nb * HEADS, seq // Q_BLOCK)
    flast = f_cum[:, :, K_BLOCK - 1::K_BLOCK].reshape(nb * HEADS, seq // K_BLOCK)
    c_bound = (16.0 * 1.02 * jnp.max(jnp.abs(q_norm)) * jnp.max(jnp.abs(k_norm)) + 1.0).reshape(1).astype(F32)

    o_t = _fox_attention(qT, k, vT, k_meta, vT_meta, fq0, flast, c_bound)
    o_t = o_t.reshape(nb, D_ATTN, seq)
    return _mix_out_ffn2(h1, o_t, oconv, p, TOKEN_TILE)
```

```python
import jax
import jax.numpy as jnp
from jax import lax
from jax.experimental import pallas as pl
from jax.experimental.pallas import tpu as pltpu

D_MODEL = 1024
N_META = 16
D_ATTN = 512
D_CONV = 512
HEADS = 8
HEAD_DIM = 64
CONV_WIDTH = 3
D_FF = 2816
EPS = 1e-6

F32 = jnp.float32
BF16 = jnp.bfloat16

LANES = 128
SUBLANES = 8
MXU_DIM = 256
VMEM_LIMIT_BYTES = 60000 * 1024

TOKEN_TILE = 512
META_TILE = LANES
FF_CHUNK = MXU_DIM
N_FF_CHUNKS = D_FF // FF_CHUNK
Q_BLOCK = MXU_DIM
K_BLOCK = MXU_DIM
AUG_DIM = LANES
HEAD_GROUP = 4
MASKED = -1e30
LOG2E = 1.4426950408889634
F_ROWS = 2 * SUBLANES

EXP_UNDERFLOW = -104.0
BOUNDED_LOGIT_MAX = 120.0

assert D_FF % FF_CHUNK == 0 and HEADS % HEAD_GROUP == 0


def _dot(a, b):
    return jnp.dot(a, b, preferred_element_type=F32)


def _dot_nt(a, b):
    return lax.dot_general(a, b, (((1,), (1,)), ((), ())), preferred_element_type=F32)


def _rmsnorm_rows(x, gain):
    ms = jnp.mean(x * x, axis=-1, keepdims=True)
    return x * lax.rsqrt(ms + EPS) * gain


def _split3(x):
    hi = x.astype(BF16)
    r1 = x - hi.astype(F32)
    mid = r1.astype(BF16)
    lo = (r1 - mid.astype(F32)).astype(BF16)
    return hi, mid, lo


def _swiglu(xn_ref, wg_ref, wu_ref, wo_ref, acc_ref):
    acc_ref[...] = jnp.zeros_like(acc_ref)

    def body(c, carry):
        xn = xn_ref[...]
        g = _dot(xn, wg_ref[c])
        u = _dot(xn, wu_ref[c])
        a = (g * jax.nn.sigmoid(g) * u).astype(BF16)
        acc_ref[...] += _dot(a, wo_ref[c])
        return carry

    lax.fori_loop(0, N_FF_CHUNKS, body, 0, unroll=True)


def _ffn1_mix_in_kernel(
        npad_ref,
        x_ref, zc_in_ref, fc_in_ref,
        g1_ref, wg_ref, wu_ref, wo_ref,
        gmix_ref, wqkvT_ref, wfT_ref, wbcu_ref, bf_ref, tri_ref,
        gq_ref, gk_ref, cw_ref, gconv_ref,
        h1_ref, qT_ref, k_ref, vT_ref, f_ref, oconv_ref, zc_out_ref, fc_out_ref,
        xn_ref, acc_ref, zc_ref, fc_ref):
    t = pl.program_id(1)
    tm = x_ref.shape[1]

    @pl.when(t == 0)
    def _():
        zc_ref[...] = zc_in_ref[...]
        fc_ref[...] = fc_in_ref[...]

    x = x_ref[0]
    xn_ref[...] = _rmsnorm_rows(x, g1_ref[...]).astype(BF16)
    _swiglu(xn_ref, wg_ref, wu_ref, wo_ref, acc_ref)
    h1 = x + 0.5 * acc_ref[...]
    h1_ref[0] = h1

    xn2 = _rmsnorm_rows(h1, gmix_ref[...]).astype(BF16)
    qkvT = _dot_nt(wqkvT_ref[...], xn2)
    fT = _dot_nt(wfT_ref[...], xn2)
    bcu = _dot(xn2, wbcu_ref[...])

    fl = fT + bf_ref[...]
    logf = jnp.minimum(fl, 0.0) - jnp.log(1.0 + jnp.exp(-jnp.abs(fl)))
    pos = t * tm + lax.broadcasted_iota(jnp.int32, logf.shape, 1)
    logf = jnp.where(pos >= npad_ref[0], logf, 0.0)
    tri = tri_ref[...]
    hi, mid, lo = _split3(logf)
    csum = _dot(hi, tri) + _dot(mid, tri) + _dot(lo, tri)
    f_all = csum + jnp.concatenate([fc_ref[...]] * (tm // LANES), axis=1)
    fc_new = jnp.broadcast_to(f_all[:, tm - 1:tm], fc_ref.shape)
    fc_ref[...] = fc_new
    fc_out_ref[...] = fc_new
    f_ref[0] = f_all[:HEADS]

    q3 = qkvT[0:D_ATTN].reshape(HEADS, HEAD_DIM, tm)
    k3 = qkvT[D_ATTN:2 * D_ATTN].reshape(HEADS, HEAD_DIM, tm)
    v3 = qkvT[2 * D_ATTN:3 * D_ATTN].reshape(HEADS, HEAD_DIM, tm)
    scale = HEAD_DIM ** -0.5 * LOG2E
    qn = q3 * lax.rsqrt(jnp.mean(q3 * q3, axis=1, keepdims=True) + EPS) * (gq_ref[...] * scale)[None]
    kn = k3 * lax.rsqrt(jnp.mean(k3 * k3, axis=1, keepdims=True) + EPS) * gk_ref[...][None]

    f_hi, f_mid, f_lo = (p.astype(F32) for p in _split3(f_all * LOG2E))
    row = lax.broadcasted_iota(jnp.int32, (SUBLANES, tm), 0)
    pad_rows = jnp.zeros((AUG_DIM - HEAD_DIM - SUBLANES, tm), F32)
    for h in range(HEADS):
        fh = jnp.broadcast_to(f_hi[h:h + 1], (SUBLANES, tm))
        fm = jnp.broadcast_to(f_mid[h:h + 1], (SUBLANES, tm))
        fo = jnp.broadcast_to(f_lo[h:h + 1], (SUBLANES, tm))
        ones_mid = jnp.where(row < 6, 1.0, 0.0)
        aug_q = jnp.where(row == 0, fh, jnp.where(row == 1, fm, jnp.where(row == 2, fo, ones_mid)))
        aug_k = jnp.where(row == 3, -fh, jnp.where(row == 4, -fm, jnp.where(row == 5, -fo, ones_mid)))
        qT_ref[0, h] = jnp.concatenate([qn[h], aug_q, pad_rows], axis=0).astype(BF16)
        k_aug_t = jnp.concatenate([kn[h], aug_k, pad_rows], axis=0)
        k_ref[0, h] = k_aug_t.T.astype(BF16)
        vT_ref[0, h] = v3[h].astype(BF16)

    gate_b = bcu[:, 0:D_CONV]
    z = bcu[:, D_CONV:2 * D_CONV] * bcu[:, 2 * D_CONV:3 * D_CONV]
    zc = zc_ref[...]
    rowz = lax.broadcasted_iota(jnp.int32, z.shape, 0)
    prev1 = jnp.broadcast_to(zc[7:8], z.shape)
    prev2 = jnp.broadcast_to(zc[6:7], z.shape)
    z1 = jnp.where(rowz == 0, prev1, pltpu.roll(z, 1, axis=0))
    z2 = jnp.where(rowz == 0, prev2, jnp.where(rowz == 1, prev1, pltpu.roll(z, 2, axis=0)))
    cw = cw_ref[...]
    y = cw[0:1] * z2 + cw[1:2] * z1 + cw[2:3] * z
    oconv_ref[0] = _rmsnorm_rows(gate_b * y, gconv_ref[...]).astype(BF16)
    zc_new = z[tm - SUBLANES:tm]
    zc_ref[...] = zc_new
    zc_out_ref[...] = zc_new


def _const_spec(shape):
    nd = len(shape)
    return pl.BlockSpec(shape, lambda *_: (0,) * nd, pipeline_mode=pl.Buffered(1))


def _ffn1_mix_in(x, n_pad, zc_in, fc_in, p, tm):
    nb, seq, _ = x.shape
    nt = seq // tm
    tri = jnp.triu(jnp.ones((tm, tm), BF16))
    gq_t = jnp.broadcast_to(p["gq"][:, None], (HEAD_DIM, tm))
    gk_t = jnp.broadcast_to(p["gk"][:, None], (HEAD_DIM, tm))

    def tile3(last):
        return pl.BlockSpec((1, tm, last), lambda b, t, *_: (b, t, 0))

    in_specs = [
        tile3(D_MODEL),
        _const_spec((SUBLANES, D_CONV)), _const_spec((F_ROWS, LANES)),
        _const_spec((1, D_MODEL)),
        _const_spec((N_FF_CHUNKS, D_MODEL, FF_CHUNK)), _const_spec((N_FF_CHUNKS, D_MODEL, FF_CHUNK)),
        _const_spec((N_FF_CHUNKS, FF_CHUNK, D_MODEL)),
        _const_spec((1, D_MODEL)), _const_spec((3 * D_ATTN, D_MODEL)), _const_spec((F_ROWS, D_MODEL)),
        _const_spec((D_MODEL, 3 * D_CONV)), _const_spec((F_ROWS, 1)), _const_spec((tm, tm)),
        _const_spec((HEAD_DIM, tm)), _const_spec((HEAD_DIM, tm)),
        _const_spec((SUBLANES, D_CONV)), _const_spec((1, D_CONV)),
    ]
    out_shape = [
        jax.ShapeDtypeStruct((nb, seq, D_MODEL), F32),
        jax.ShapeDtypeStruct((nb, HEADS, AUG_DIM, seq), BF16),
        jax.ShapeDtypeStruct((nb, HEADS, seq, AUG_DIM), BF16),
        jax.ShapeDtypeStruct((nb, HEADS, HEAD_DIM, seq), BF16),
        jax.ShapeDtypeStruct((nb, HEADS, seq), F32),
        jax.ShapeDtypeStruct((nb, seq, D_CONV), BF16),
        jax.ShapeDtypeStruct((SUBLANES, D_CONV), F32),
        jax.ShapeDtypeStruct((F_ROWS, LANES), F32),
    ]
    out_specs = [
        tile3(D_MODEL),
        pl.BlockSpec((1, HEADS, AUG_DIM, tm), lambda b, t, *_: (b, 0, 0, t)),
        pl.BlockSpec((1, HEADS, tm, AUG_DIM), lambda b, t, *_: (b, 0, t, 0)),
        pl.BlockSpec((1, HEADS, HEAD_DIM, tm), lambda b, t, *_: (b, 0, 0, t)),
        pl.BlockSpec((1, HEADS, tm), lambda b, t, *_: (b, 0, t)),
        tile3(D_CONV),
        pl.BlockSpec((SUBLANES, D_CONV), lambda b, t, *_: (0, 0)),
        pl.BlockSpec((F_ROWS, LANES), lambda b, t, *_: (0, 0)),
    ]
    grid_spec = pltpu.PrefetchScalarGridSpec(
        num_scalar_prefetch=1, grid=(nb, nt), in_specs=in_specs, out_specs=out_specs,
        scratch_shapes=[
            pltpu.VMEM((tm, D_MODEL), BF16),
            pltpu.VMEM((tm, D_MODEL), F32),
            pltpu.VMEM((SUBLANES, D_CONV), F32),
            pltpu.VMEM((F_ROWS, LANES), F32),
        ])
    return pl.pallas_call(
        _ffn1_mix_in_kernel, out_shape=out_shape, grid_spec=grid_spec, name="ffn1_mix_in",
        compiler_params=pltpu.CompilerParams(
            dimension_semantics=("arbitrary", "arbitrary"), vmem_limit_bytes=VMEM_LIMIT_BYTES),
    )(n_pad, x, zc_in, fc_in,
      p["g1"], p["wg1"], p["wu1"], p["wo1"],
      p["gmix"], p["wqkvT"], p["wfT"], p["wbcu"], p["bf"], tri,
      gq_t, gk_t, p["cw"], p["gconv"])


def _fox_attention_kernel(fq0_ref, flast_ref, cb_ref,
                          qT_ref, k_ref, vT_ref, km_ref, vmT_ref,
                          o_ref,
                          acc_ref, l_ref):
    bh0 = pl.program_id(0) * HEADS + pl.program_id(1) * HEAD_GROUP
    heads = range(HEAD_GROUP)
    nq = qT_ref.shape[3] // Q_BLOCK
    c_bound = cb_ref[0]
    key_idx = lax.broadcasted_iota(jnp.int32, (K_BLOCK, Q_BLOCK), 0)
    qry_idx = lax.broadcasted_iota(jnp.int32, (K_BLOCK, Q_BLOCK), 1)
    causal = key_idx <= qry_idx

    def block_live(g, i, j):
        return fq0_ref[bh0 + g, i] - flast_ref[bh0 + g, jnp.maximum(j, 0)] + c_bound >= EXP_UNDERFLOW

    def sublane_partial_sum(p):
        return jnp.sum(p.reshape(p.shape[0] // SUBLANES, SUBLANES, p.shape[1]), axis=0)

    def bounded_tile(k_blk, v_t, q_t, masked):
        s = _dot(k_blk, q_t)
        if masked:
            s = jnp.where(causal, s, MASKED)
        p = jnp.exp2(s)
        return sublane_partial_sum(p), _dot(v_t, p.astype(BF16))

    def bounded_q_block(i, with_prev):
        q0 = pl.multiple_of(i * Q_BLOCK, Q_BLOCK)
        q_t = [qT_ref[0, g, :, pl.ds(q0, Q_BLOCK)] for g in heads]
        p0 = pl.multiple_of(jnp.maximum(i - 1, 0) * K_BLOCK, K_BLOCK)
        s_m = [_dot(km_ref[g], q_t[g]) for g in heads]
        s_d = [_dot(k_ref[0, g, pl.ds(q0, K_BLOCK), :], q_t[g]) for g in heads]
        s_p = [_dot(k_ref[0, g, pl.ds(p0, K_BLOCK), :], q_t[g]) for g in heads] if with_prev else None
        p_m = [jnp.exp2(s) for s in s_m]
        p_d = [jnp.exp2(jnp.where(causal, s, MASKED)) for s in s_d]
        p_p = [jnp.exp2(s) for s in s_p] if with_prev else None
        for g in heads:
            l_g = sublane_partial_sum(p_m[g]) + sublane_partial_sum(p_d[g])
            acc_g = (_dot(vmT_ref[g], p_m[g].astype(BF16))
                     + _dot(vT_ref[0, g, :, pl.ds(q0, K_BLOCK)], p_d[g].astype(BF16)))
            if with_prev:
                l_g = l_g + sublane_partial_sum(p_p[g])
                acc_g = acc_g + _dot(vT_ref[0, g, :, pl.ds(p0, K_BLOCK)], p_p[g].astype(BF16))
            l_ref[g] = l_g
            acc_ref[g] = acc_g

        def cond(j):
            live = block_live(0, i, j)
            for g in heads[1:]:
                live = jnp.logical_or(live, block_live(g, i, j))
            return jnp.logical_and(j >= 0, live)

        def body(j):
            k0 = pl.multiple_of(j * K_BLOCK, K_BLOCK)
            for g in heads:
                l_j, acc_j = bounded_tile(k_ref[0, g, pl.ds(k0, K_BLOCK), :],
                                          vT_ref[0, g, :, pl.ds(k0, K_BLOCK)], q_t[g], False)
                l_ref[g] += l_j
                acc_ref[g] += acc_j
            return j - 1

        lax.while_loop(cond, body, i - 2)
        for g in heads:
            l_tot = jnp.sum(l_ref[g], axis=0, keepdims=True)
            o_ref[0, g, :, pl.ds(q0, Q_BLOCK)] = (acc_ref[g] / l_tot).astype(o_ref.dtype)

    @pl.when(c_bound <= BOUNDED_LOGIT_MAX)
    def _():
        bounded_q_block(0, False)

        def step(i, carry):
            bounded_q_block(i, True)
            return carry

        lax.fori_loop(1, nq, step, 0)

    def online_q_block(g, i):
        q0 = pl.multiple_of(i * Q_BLOCK, Q_BLOCK)
        q_t = qT_ref[0, g, :, pl.ds(q0, Q_BLOCK)]
        s_m = _dot(km_ref[g], q_t)
        m = jnp.max(s_m, axis=0, keepdims=True)
        p_m = jnp.exp2(s_m - m)
        l = jnp.sum(p_m, axis=0, keepdims=True)
        acc = _dot(vmT_ref[g], p_m.astype(BF16))

        def online_step(state, k0, masked):
            m, l, acc = state
            s = _dot(k_ref[0, g, pl.ds(k0, K_BLOCK), :], q_t)
            if masked:
                s = jnp.where(causal, s, MASKED)
            m_new = jnp.maximum(m, jnp.max(s, axis=0, keepdims=True))
            alpha = jnp.exp2(m - m_new)
            p = jnp.exp2(s - m_new)
            l = alpha * l + jnp.sum(p, axis=0, keepdims=True)
            acc = alpha * acc + _dot(vT_ref[0, g, :, pl.ds(k0, K_BLOCK)], p.astype(BF16))
            return m_new, l, acc

        state = online_step((m, l, acc), q0, True)

        def cond(carry):
            return jnp.logical_and(carry[0] >= 0, block_live(g, i, carry[0]))

        def body(carry):
            j = carry[0]
            return (j - 1,) + online_step(carry[1:], pl.multiple_of(j * K_BLOCK, K_BLOCK), False)

        _, m, l, acc = lax.while_loop(cond, body, (i - 1,) + state)
        o_ref[0, g, :, pl.ds(q0, Q_BLOCK)] = (acc / l).astype(o_ref.dtype)

    @pl.when(c_bound > BOUNDED_LOGIT_MAX)
    def _():
        def head_loop(g, carry):
            def step(i, inner):
                online_q_block(g, i)
                return inner

            lax.fori_loop(0, nq, step, 0)
            return carry

        lax.fori_loop(0, HEAD_GROUP, head_loop, 0)


def _fox_attention(qT, k, vT, k_meta, vT_meta, fq0, flast, c_bound):
    nb, _, _, seq = qT.shape
    smem = pl.BlockSpec(memory_space=pltpu.SMEM)
    hg = HEAD_GROUP
    return pl.pallas_call(
        _fox_attention_kernel,
        out_shape=jax.ShapeDtypeStruct((nb, HEADS, HEAD_DIM, seq), BF16),
        grid=(nb, HEADS // hg),
        in_specs=[
            smem, smem, smem,
            pl.BlockSpec((1, hg, AUG_DIM, seq), lambda b, h: (b, h, 0, 0)),
            pl.BlockSpec((1, hg, seq, AUG_DIM), lambda b, h: (b, h, 0, 0)),
            pl.BlockSpec((1, hg, HEAD_DIM, seq), lambda b, h: (b, h, 0, 0)),
            pl.BlockSpec((hg, N_META, AUG_DIM), lambda b, h: (h, 0, 0)),
            pl.BlockSpec((hg, HEAD_DIM, N_META), lambda b, h: (h, 0, 0)),
        ],
        out_specs=pl.BlockSpec((1, hg, HEAD_DIM, seq), lambda b, h: (b, h, 0, 0)),
        scratch_shapes=[pltpu.VMEM((hg, HEAD_DIM, Q_BLOCK), F32),
                        pltpu.VMEM((hg, SUBLANES, Q_BLOCK), F32)],
        name="fox_attention",
        compiler_params=pltpu.CompilerParams(
            dimension_semantics=("arbitrary", "arbitrary"), vmem_limit_bytes=VMEM_LIMIT_BYTES),
    )(fq0, flast, c_bound, qT, k, vT, k_meta, vT_meta)


def _mix_out_ffn2_kernel(h1_ref, oT_ref, oconv_ref,
                         gattn_ref, woa_ref, woc_ref,
                         g2_ref, wg_ref, wu_ref, wo_ref, gfin_ref,
                         out_ref,
                         xn_ref, acc_ref):
    o_t = oT_ref[0].astype(F32)
    ms = jnp.mean(o_t * o_t, axis=0, keepdims=True)
    o_n = (o_t * lax.rsqrt(ms + EPS) * gattn_ref[...]).T.astype(BF16)
    mix = _dot(o_n, woa_ref[...]) + _dot(oconv_ref[0], woc_ref[...])
    h2 = h1_ref[0] + mix
    xn_ref[...] = _rmsnorm_rows(h2, g2_ref[...]).astype(BF16)
    _swiglu(xn_ref, wg_ref, wu_ref, wo_ref, acc_ref)
    h3 = h2 + 0.5 * acc_ref[...]
    out_ref[0] = _rmsnorm_rows(h3, gfin_ref[...])


def _mix_out_ffn2(h1, o_t, oconv, p, tm):
    nb, seq, _ = h1.shape
    gattn_t = jnp.broadcast_to(p["gattn"][:, None], (D_ATTN, tm))
    in_specs = [
        pl.BlockSpec((1, tm, D_MODEL), lambda b, t: (b, t, 0)),
        pl.BlockSpec((1, D_ATTN, tm), lambda b, t: (b, 0, t)),
        pl.BlockSpec((1, tm, D_CONV), lambda b, t: (b, t, 0)),
        _const_spec((D_ATTN, tm)), _const_spec((D_ATTN, D_MODEL)), _const_spec((D_CONV, D_MODEL)),
        _const_spec((1, D_MODEL)),
        _const_spec((N_FF_CHUNKS, D_MODEL, FF_CHUNK)), _const_spec((N_FF_CHUNKS, D_MODEL, FF_CHUNK)),
        _const_spec((N_FF_CHUNKS, FF_CHUNK, D_MODEL)),
        _const_spec((1, D_MODEL)),
    ]
    return pl.pallas_call(
        _mix_out_ffn2_kernel,
        out_shape=jax.ShapeDtypeStruct((nb, seq, D_MODEL), F32),
        grid=(nb, seq // tm),
        in_specs=in_specs,
        out_specs=pl.BlockSpec((1, tm, D_MODEL), lambda b, t: (b, t, 0)),
        scratch_shapes=[pltpu.VMEM((tm, D_MODEL), BF16), pltpu.VMEM((tm, D_MODEL), F32)],
        name="mix_out_ffn2",
        compiler_params=pltpu.CompilerParams(
            dimension_semantics=("arbitrary", "arbitrary"), vmem_limit_bytes=VMEM_LIMIT_BYTES),
    )(h1, o_t, oconv, gattn_t, p["woa"], p["woc"], p["g2"], p["wg2"], p["wu2"], p["wo2"], p["gfin"])


def _ffn_weights(w_in, w_out):
    def chunks(w):
        return w.reshape(D_MODEL, N_FF_CHUNKS, FF_CHUNK).transpose(1, 0, 2).astype(BF16)
    return chunks(w_in[:, :D_FF]), chunks(w_in[:, D_FF:]), w_out.reshape(N_FF_CHUNKS, FF_CHUNK, D_MODEL).astype(BF16)


def kernel(x, meta_tokens, ffn1_norm, ffn1_w_in, ffn1_w_out, mix_norm, w_mix_in, b_forget, q_norm, k_norm, conv_w, attn_out_norm, conv_out_norm, w_mix_out, ffn2_norm, ffn2_w_in, ffn2_w_out, final_norm):
    nb, seq, _ = x.shape
    wg1, wu1, wo1 = _ffn_weights(ffn1_w_in[0], ffn1_w_out[0])
    wg2, wu2, wo2 = _ffn_weights(ffn2_w_in[0], ffn2_w_out[0])
    wmix = w_mix_in[0]
    n_qkv = 3 * D_ATTN
    p = {
        "g1": ffn1_norm, "wg1": wg1, "wu1": wu1, "wo1": wo1,
        "gmix": mix_norm,
        "wqkvT": wmix[:, :n_qkv].T.astype(BF16),
        "wfT": jnp.pad(wmix[:, n_qkv:n_qkv + HEADS].T, ((0, F_ROWS - HEADS), (0, 0))).astype(BF16),
        "wbcu": wmix[:, n_qkv + HEADS:].astype(BF16),
        "bf": jnp.pad(b_forget[0], (0, F_ROWS - HEADS))[:, None],
        "gq": q_norm[0], "gk": k_norm[0],
        "cw": jnp.pad(conv_w[0], ((0, SUBLANES - CONV_WIDTH), (0, 0))),
        "gconv": conv_out_norm,
        "gattn": attn_out_norm[0],
        "woa": w_mix_out[0, :D_ATTN].astype(BF16), "woc": w_mix_out[0, D_ATTN:].astype(BF16),
        "g2": ffn2_norm, "wg2": wg2, "wu2": wu2, "wo2": wo2, "gfin": final_norm,
    }

    n_pad = META_TILE - N_META
    meta = jnp.pad(meta_tokens.astype(x.dtype), ((n_pad, 0), (0, 0)))[None]
    zeros_zc = jnp.zeros((SUBLANES, D_CONV), F32)
    zeros_fc = jnp.zeros((F_ROWS, LANES), F32)
    _, _, k_m, vT_m, _, _, zc_m, fc_m = _ffn1_mix_in(
        meta, jnp.full((1,), n_pad, jnp.int32), zeros_zc, zeros_fc, p, META_TILE)
    k_meta = k_m[0, :, n_pad:, :]
    vT_meta = vT_m[0, :, :, n_pad:]

    h1, qT, k, vT, f_cum, oconv, _, _ = _ffn1_mix_in(
        x, jnp.zeros((1,), jnp.int32), zc_m, fc_m, p, TOKEN_TILE)

    fq0 = f_cum[:, :, 0::Q_BLOCK].reshape(nb * HEADS, seq // Q_BLOCK)
    flast = f_cum[:, :, K_BLOCK - 1::K_BLOCK].reshape(nb * HEADS, seq // K_BLOCK)
    c_bound = (16.0 * 1.02 * jnp.max(jnp.abs(q_norm)) * jnp.max(jnp.abs(k_norm)) + 1.0).reshape(1).astype(F32)

    o_t = _fox_attention(qT, k, vT, k_meta, vT_meta, fq0, flast, c_bound)
    o_t = o_t.reshape(nb, D_ATTN, seq)
    return _mix_out_ffn2(h1, o_t, oconv, p, TOKEN_TILE)
```

```python
import jax
import jax.numpy as jnp
from jax import lax
from jax.experimental import pallas as pl
from jax.experimental.pallas import tpu as pltpu

D_MODEL = 1024
N_META = 16
D_ATTN = 512
D_CONV = 512
HEADS = 8
HEAD_DIM = 64
CONV_WIDTH = 3
D_FF = 2816
EPS = 1e-6

F32 = jnp.float32
BF16 = jnp.bfloat16

LANES = 128
SUBLANES = 8
MXU_DIM = 256
VMEM_LIMIT_BYTES = 60000 * 1024

TOKEN_TILE = 512
META_TILE = LANES
FF_CHUNK = MXU_DIM
N_FF_CHUNKS = D_FF // FF_CHUNK
Q_BLOCK = MXU_DIM
K_BLOCK = MXU_DIM
AUG_DIM = LANES
HEAD_GROUP = 4
MASKED = -1e30
LOG2E = 1.4426950408889634
F_ROWS = 2 * SUBLANES

EXP_UNDERFLOW = -104.0
BOUNDED_LOGIT_MAX = 120.0

assert D_FF % FF_CHUNK == 0 and HEADS % HEAD_GROUP == 0


def _dot(a, b):
    return jnp.dot(a, b, preferred_element_type=F32)


def _dot_nt(a, b):
    return lax.dot_general(a, b, (((1,), (1,)), ((), ())), preferred_element_type=F32)


def _rmsnorm_rows(x, gain):
    ms = jnp.mean(x * x, axis=-1, keepdims=True)
    return x * lax.rsqrt(ms + EPS) * gain


def _split3(x):
    hi = x.astype(BF16)
    r1 = x - hi.astype(F32)
    mid = r1.astype(BF16)
    lo = (r1 - mid.astype(F32)).astype(BF16)
    return hi, mid, lo


def _swiglu(xn_ref, wg_ref, wu_ref, wo_ref, acc_ref):
    acc_ref[...] = jnp.zeros_like(acc_ref)

    def body(c, carry):
        xn = xn_ref[...]
        g = _dot(xn, wg_ref[c])
        u = _dot(xn, wu_ref[c])
        a = (g * jax.nn.sigmoid(g) * u).astype(BF16)
        acc_ref[...] += _dot(a, wo_ref[c])
        return carry

    lax.fori_loop(0, N_FF_CHUNKS, body, 0, unroll=True)


def _ffn1_mix_in_kernel(
        npad_ref,
        x_ref, zc_in_ref, fc_in_ref,
        g1_ref, wg_ref, wu_ref, wo_ref,
        gmix_ref, wfkT_ref, wqT_ref, wvT_ref, wcu_ref, wb_ref, bf_ref, tri_ref,
        gq_ref, gk_ref, cw_ref, gconv_ref,
        h1_ref, qT_ref, k_ref, vT_ref, f_ref, oconv_ref, zc_out_ref, fc_out_ref,
        xn_ref, acc_ref, zc_ref, fc_ref):
    t = pl.program_id(1)
    tm = x_ref.shape[1]

    @pl.when(t == 0)
    def _():
        zc_ref[...] = zc_in_ref[...]
        fc_ref[...] = fc_in_ref[...]

    x = x_ref[0]
    xn_ref[...] = _rmsnorm_rows(x, g1_ref[...]).astype(BF16)
    _swiglu(xn_ref, wg_ref, wu_ref, wo_ref, acc_ref)
    h1 = x + 0.5 * acc_ref[...]
    h1_ref[0] = h1

    xn2 = _rmsnorm_rows(h1, gmix_ref[...]).astype(BF16)
    fkT = _dot_nt(wfkT_ref[...], xn2)
    cu = _dot(xn2, wcu_ref[...])

    fl = fkT[0:F_ROWS] + bf_ref[...]
    logf = jnp.minimum(fl, 0.0) - jnp.log(1.0 + jnp.exp(-jnp.abs(fl)))
    pos = t * tm + lax.broadcasted_iota(jnp.int32, logf.shape, 1)
    logf = jnp.where(pos >= npad_ref[0], logf, 0.0)
    pieces = _dot(jnp.concatenate(_split3(logf), axis=0), tri_ref[...])
    csum = pieces[0:F_ROWS] + pieces[F_ROWS:2 * F_ROWS] + pieces[2 * F_ROWS:3 * F_ROWS]
    f_all = csum + jnp.concatenate([fc_ref[...]] * (tm // LANES), axis=1)
    fc_new = jnp.broadcast_to(f_all[:, tm - 1:tm], fc_ref.shape)
    fc_ref[...] = fc_new
    fc_out_ref[...] = fc_new
    f_ref[0] = f_all[:HEADS]

    f_hi, f_mid, f_lo = (p.astype(F32) for p in _split3(f_all * LOG2E))
    row = lax.broadcasted_iota(jnp.int32, (SUBLANES, tm), 0)
    ones_mid = jnp.where(row < 6, 1.0, 0.0)
    pad_rows = jnp.zeros((AUG_DIM - HEAD_DIM - SUBLANES, tm), F32)

    def head_pieces(h):
        return (jnp.broadcast_to(f_hi[h:h + 1], (SUBLANES, tm)),
                jnp.broadcast_to(f_mid[h:h + 1], (SUBLANES, tm)),
                jnp.broadcast_to(f_lo[h:h + 1], (SUBLANES, tm)))

    def head_rmsnorm(x_t, gain_t):
        x3 = x_t.reshape(HEADS, HEAD_DIM, tm)
        return x3 * lax.rsqrt(jnp.mean(x3 * x3, axis=1, keepdims=True) + EPS) * gain_t[None]

    kn = head_rmsnorm(fkT[F_ROWS:F_ROWS + D_ATTN], gk_ref[...])
    qT = _dot_nt(wqT_ref[...], xn2)
    gate_b = _dot(xn2, wb_ref[...])
    for h in range(HEADS):
        fh, fm, fo = head_pieces(h)
        aug_k = jnp.where(row == 3, -fh, jnp.where(row == 4, -fm, jnp.where(row == 5, -fo, ones_mid)))
        k_aug_t = jnp.concatenate([kn[h], aug_k, pad_rows], axis=0)
        k_ref[0, h] = k_aug_t.T.astype(BF16)

    vT = _dot_nt(wvT_ref[...], xn2)
    z = cu[:, 0:D_CONV] * cu[:, D_CONV:2 * D_CONV]
    zc = zc_ref[...]
    rowz = lax.broadcasted_iota(jnp.int32, z.shape, 0)
    prev1 = jnp.broadcast_to(zc[7:8], z.shape)
    prev2 = jnp.broadcast_to(zc[6:7], z.shape)
    z1 = jnp.where(rowz == 0, prev1, pltpu.roll(z, 1, axis=0))
    z2 = jnp.where(rowz == 0, prev2, jnp.where(rowz == 1, prev1, pltpu.roll(z, 2, axis=0)))
    cw = cw_ref[...]
    y = cw[0:1] * z2 + cw[1:2] * z1 + cw[2:3] * z
    oconv_ref[0] = _rmsnorm_rows(gate_b * y, gconv_ref[...]).astype(BF16)
    zc_new = z[tm - SUBLANES:tm]
    zc_ref[...] = zc_new
    zc_out_ref[...] = zc_new

    qn = head_rmsnorm(qT, gq_ref[...] * (HEAD_DIM ** -0.5 * LOG2E))
    v3 = vT.reshape(HEADS, HEAD_DIM, tm)
    for h in range(HEADS):
        fh, fm, fo = head_pieces(h)
        aug_q = jnp.where(row == 0, fh, jnp.where(row == 1, fm, jnp.where(row == 2, fo, ones_mid)))
        qT_ref[0, h] = jnp.concatenate([qn[h], aug_q, pad_rows], axis=0).astype(BF16)
        vT_ref[0, h] = v3[h].astype(BF16)


def _const_spec(shape):
    nd = len(shape)
    return pl.BlockSpec(shape, lambda *_: (0,) * nd, pipeline_mode=pl.Buffered(1))


def _ffn1_mix_in(x, n_pad, zc_in, fc_in, p, tm):
    nb, seq, _ = x.shape
    nt = seq // tm
    tri = jnp.triu(jnp.ones((tm, tm), BF16))
    gq_t = jnp.broadcast_to(p["gq"][:, None], (HEAD_DIM, tm))
    gk_t = jnp.broadcast_to(p["gk"][:, None], (HEAD_DIM, tm))

    def tile3(last):
        return pl.BlockSpec((1, tm, last), lambda b, t, *_: (b, t, 0))

    in_specs = [
        tile3(D_MODEL),
        _const_spec((SUBLANES, D_CONV)), _const_spec((F_ROWS, LANES)),
        _const_spec((1, D_MODEL)),
        _const_spec((N_FF_CHUNKS, D_MODEL, FF_CHUNK)), _const_spec((N_FF_CHUNKS, D_MODEL, FF_CHUNK)),
        _const_spec((N_FF_CHUNKS, FF_CHUNK, D_MODEL)),
        _const_spec((1, D_MODEL)), _const_spec((F_ROWS + D_ATTN, D_MODEL)),
        _const_spec((D_ATTN, D_MODEL)), _const_spec((D_ATTN, D_MODEL)),
        _const_spec((D_MODEL, 2 * D_CONV)), _const_spec((D_MODEL, D_CONV)),
        _const_spec((F_ROWS, 1)), _const_spec((tm, tm)),
        _const_spec((HEAD_DIM, tm)), _const_spec((HEAD_DIM, tm)),
        _const_spec((SUBLANES, D_CONV)), _const_spec((1, D_CONV)),
    ]
    out_shape = [
        jax.ShapeDtypeStruct((nb, seq, D_MODEL), F32),
        jax.ShapeDtypeStruct((nb, HEADS, AUG_DIM, seq), BF16),
        jax.ShapeDtypeStruct((nb, HEADS, seq, AUG_DIM), BF16),
        jax.ShapeDtypeStruct((nb, HEADS, HEAD_DIM, seq), BF16),
        jax.ShapeDtypeStruct((nb, HEADS, seq), F32),
        jax.ShapeDtypeStruct((nb, seq, D_CONV), BF16),
        jax.ShapeDtypeStruct((SUBLANES, D_CONV), F32),
        jax.ShapeDtypeStruct((F_ROWS, LANES), F32),
    ]
    out_specs = [
        tile3(D_MODEL),
        pl.BlockSpec((1, HEADS, AUG_DIM, tm), lambda b, t, *_: (b, 0, 0, t)),
        pl.BlockSpec((1, HEADS, tm, AUG_DIM), lambda b, t, *_: (b, 0, t, 0)),
        pl.BlockSpec((1, HEADS, HEAD_DIM, tm), lambda b, t, *_: (b, 0, 0, t)),
        pl.BlockSpec((1, HEADS, tm), lambda b, t, *_: (b, 0, t)),
        tile3(D_CONV),
        pl.BlockSpec((SUBLANES, D_CONV), lambda b, t, *_: (0, 0)),
        pl.BlockSpec((F_ROWS, LANES), lambda b, t, *_: (0, 0)),
    ]
    grid_spec = pltpu.PrefetchScalarGridSpec(
        num_scalar_prefetch=1, grid=(nb, nt), in_specs=in_specs, out_specs=out_specs,
        scratch_shapes=[
            pltpu.VMEM((tm, D_MODEL), BF16),
            pltpu.VMEM((tm, D_MODEL), F32),
            pltpu.VMEM((SUBLANES, D_CONV), F32),
            pltpu.VMEM((F_ROWS, LANES), F32),
        ])
    return pl.pallas_call(
        _ffn1_mix_in_kernel, out_shape=out_shape, grid_spec=grid_spec, name="ffn1_mix_in",
        compiler_params=pltpu.CompilerParams(
            dimension_semantics=("arbitrary", "arbitrary"), vmem_limit_bytes=VMEM_LIMIT_BYTES),
    )(n_pad, x, zc_in, fc_in,
      p["g1"], p["wg1"], p["wu1"], p["wo1"],
      p["gmix"], p["wfkT"], p["wqT"], p["wvT"], p["wcu"], p["wb"], p["bf"], tri,
      gq_t, gk_t, p["cw"], p["gconv"])


def _fox_attention_kernel(fq0_ref, flast_ref, cb_ref,
                          qT_ref, k_ref, vT_ref, km_ref, vmT_ref,
                          o_ref,
                          acc_ref, l_ref):
    bh0 = pl.program_id(0) * HEADS + pl.program_id(1) * HEAD_GROUP
    heads = range(HEAD_GROUP)
    nq = qT_ref.shape[3] // Q_BLOCK
    c_bound = cb_ref[0]
    key_idx = lax.broadcasted_iota(jnp.int32, (K_BLOCK, Q_BLOCK), 0)
    qry_idx = lax.broadcasted_iota(jnp.int32, (K_BLOCK, Q_BLOCK), 1)
    causal = key_idx <= qry_idx

    def block_live(g, i, j):
        return fq0_ref[bh0 + g, i] - flast_ref[bh0 + g, jnp.maximum(j, 0)] + c_bound >= EXP_UNDERFLOW

    def sublane_partial_sum(p):
        return jnp.sum(p.reshape(p.shape[0] // SUBLANES, SUBLANES, p.shape[1]), axis=0)

    def bounded_tile(k_blk, v_t, q_t, masked):
        s = _dot(k_blk, q_t)
        if masked:
            s = jnp.where(causal, s, MASKED)
        p = jnp.exp2(s)
        return sublane_partial_sum(p), _dot(v_t, p.astype(BF16))

    def bounded_q_block(i, with_prev):
        q0 = pl.multiple_of(i * Q_BLOCK, Q_BLOCK)
        q_t = [qT_ref[0, g, :, pl.ds(q0, Q_BLOCK)] for g in heads]
        p0 = pl.multiple_of(jnp.maximum(i - 1, 0) * K_BLOCK, K_BLOCK)
        s_m = [_dot(km_ref[g], q_t[g]) for g in heads]
        s_d = [_dot(k_ref[0, g, pl.ds(q0, K_BLOCK), :], q_t[g]) for g in heads]
        s_p = [_dot(k_ref[0, g, pl.ds(p0, K_BLOCK), :], q_t[g]) for g in heads] if with_prev else None
        p_m = [jnp.exp2(s) for s in s_m]
        p_d = [jnp.exp2(jnp.where(causal, s, MASKED)) for s in s_d]
        p_p = [jnp.exp2(s) for s in s_p] if with_prev else None
        for g in heads:
            l_g = sublane_partial_sum(p_m[g]) + sublane_partial_sum(p_d[g])
            acc_g = (_dot(vmT_ref[g], p_m[g].astype(BF16))
                     + _dot(vT_ref[0, g, :, pl.ds(q0, K_BLOCK)], p_d[g].astype(BF16)))
            if with_prev:
                l_g = l_g + sublane_partial_sum(p_p[g])
                acc_g = acc_g + _dot(vT_ref[0, g, :, pl.ds(p0, K_BLOCK)], p_p[g].astype(BF16))
            l_ref[g] = l_g
            acc_ref[g] = acc_g

        def cond(j):
            live = block_live(0, i, j)
            for g in heads[1:]:
                live = jnp.logical_or(live, block_live(g, i, j))
            return jnp.logical_and(j >= 0, live)

        def body(j):
            k0 = pl.multiple_of(j * K_BLOCK, K_BLOCK)
            for g in heads:
                l_j, acc_j = bounded_tile(k_ref[0, g, pl.ds(k0, K_BLOCK), :],
                                          vT_ref[0, g, :, pl.ds(k0, K_BLOCK)], q_t[g], False)
                l_ref[g] += l_j
                acc_ref[g] += acc_j
            return j - 1

        lax.while_loop(cond, body, i - 2)
        for g in heads:
            l_tot = jnp.sum(l_ref[g], axis=0, keepdims=True)
            o_ref[0, g, :, pl.ds(q0, Q_BLOCK)] = (acc_ref[g] / l_tot).astype(o_ref.dtype)

    @pl.when(c_bound <= BOUNDED_LOGIT_MAX)
    def _():
        bounded_q_block(0, False)

        def step(i, carry):
            bounded_q_block(i, True)
            return carry

        lax.fori_loop(1, nq, step, 0)

    def online_q_block(g, i):
        q0 = pl.multiple_of(i * Q_BLOCK, Q_BLOCK)
        q_t = qT_ref[0, g, :, pl.ds(q0, Q_BLOCK)]
        s_m = _dot(km_ref[g], q_t)
        m = jnp.max(s_m, axis=0, keepdims=True)
        p_m = jnp.exp2(s_m - m)
        l = jnp.sum(p_m, axis=0, keepdims=True)
        acc = _dot(vmT_ref[g], p_m.astype(BF16))

        def online_step(state, k0, masked):
            m, l, acc = state
            s = _dot(k_ref[0, g, pl.ds(k0, K_BLOCK), :], q_t)
            if masked:
                s = jnp.where(causal, s, MASKED)
            m_new = jnp.maximum(m, jnp.max(s, axis=0, keepdims=True))
            alpha = jnp.exp2(m - m_new)
            p = jnp.exp2(s - m_new)
            l = alpha * l + jnp.sum(p, axis=0, keepdims=True)
            acc = alpha * acc + _dot(vT_ref[0, g, :, pl.ds(k0, K_BLOCK)], p.astype(BF16))
            return m_new, l, acc

        state = online_step((m, l, acc), q0, True)

        def cond(carry):
            return jnp.logical_and(carry[0] >= 0, block_live(g, i, carry[0]))

        def body(carry):
            j = carry[0]
            return (j - 1,) + online_step(carry[1:], pl.multiple_of(j * K_BLOCK, K_BLOCK), False)

        _, m, l, acc = lax.while_loop(cond, body, (i - 1,) + state)
        o_ref[0, g, :, pl.ds(q0, Q_BLOCK)] = (acc / l).astype(o_ref.dtype)

    @pl.when(c_bound > BOUNDED_LOGIT_MAX)
    def _():
        def head_loop(g, carry):
            def step(i, inner):
                online_q_block(g, i)
                return inner

            lax.fori_loop(0, nq, step, 0)
            return carry

        lax.fori_loop(0, HEAD_GROUP, head_loop, 0)


def _fox_attention(qT, k, vT, k_meta, vT_meta, fq0, flast, c_bound):
    nb, _, _, seq = qT.shape
    smem = pl.BlockSpec(memory_space=pltpu.SMEM)
    hg = HEAD_GROUP
    return pl.pallas_call(
        _fox_attention_kernel,
        out_shape=jax.ShapeDtypeStruct((nb, HEADS, HEAD_DIM, seq), BF16),
        grid=(nb, HEADS // hg),
        in_specs=[
            smem, smem, smem,
            pl.BlockSpec((1, hg, AUG_DIM, seq), lambda b, h: (b, h, 0, 0)),
            pl.BlockSpec((1, hg, seq, AUG_DIM), lambda b, h: (b, h, 0, 0)),
            pl.BlockSpec((1, hg, HEAD_DIM, seq), lambda b, h: (b, h, 0, 0)),
            pl.BlockSpec((hg, N_META, AUG_DIM), lambda b, h: (h, 0, 0)),
            pl.BlockSpec((hg, HEAD_DIM, N_META), lambda b, h: (h, 0, 0)),
        ],
        out_specs=pl.BlockSpec((1, hg, HEAD_DIM, seq), lambda b, h: (b, h, 0, 0)),
        scratch_shapes=[pltpu.VMEM((hg, HEAD_DIM, Q_BLOCK), F32),
                        pltpu.VMEM((hg, SUBLANES, Q_BLOCK), F32)],
        name="fox_attention",
        compiler_params=pltpu.CompilerParams(
            dimension_semantics=("arbitrary", "arbitrary"), vmem_limit_bytes=VMEM_LIMIT_BYTES),
    )(fq0, flast, c_bound, qT, k, vT, k_meta, vT_meta)


def _mix_out_ffn2_kernel(h1_ref, oT_ref, oconv_ref,
                         gattn_ref, woa_ref, woc_ref,
                         g2_ref, wg_ref, wu_ref, wo_ref, gfin_ref,
                         out_ref,
                         xn_ref, acc_ref):
    o_t = oT_ref[0].astype(F32)
    ms = jnp.mean(o_t * o_t, axis=0, keepdims=True)
    o_n = (o_t * lax.rsqrt(ms + EPS) * gattn_ref[...]).T.astype(BF16)
    mix = _dot(o_n, woa_ref[...]) + _dot(oconv_ref[0], woc_ref[...])
    h2 = h1_ref[0] + mix
    xn_ref[...] = _rmsnorm_rows(h2, g2_ref[...]).astype(BF16)
    _swiglu(xn_ref, wg_ref, wu_ref, wo_ref, acc_ref)
    h3 = h2 + 0.5 * acc_ref[...]
    out_ref[0] = _rmsnorm_rows(h3, gfin_ref[...])


def _mix_out_ffn2(h1, o_t, oconv, p, tm):
    nb, seq, _ = h1.shape
    gattn_t = jnp.broadcast_to(p["gattn"][:, None], (D_ATTN, tm))
    in_specs = [
        pl.BlockSpec((1, tm, D_MODEL), lambda b, t: (b, t, 0)),
        pl.BlockSpec((1, D_ATTN, tm), lambda b, t: (b, 0, t)),
        pl.BlockSpec((1, tm, D_CONV), lambda b, t: (b, t, 0)),
        _const_spec((D_ATTN, tm)), _const_spec((D_ATTN, D_MODEL)), _const_spec((D_CONV, D_MODEL)),
        _const_spec((1, D_MODEL)),
        _const_spec((N_FF_CHUNKS, D_MODEL, FF_CHUNK)), _const_spec((N_FF_CHUNKS, D_MODEL, FF_CHUNK)),
        _const_spec((N_FF_CHUNKS, FF_CHUNK, D_MODEL)),
        _const_spec((1, D_MODEL)),
    ]
    return pl.pallas_call(
        _mix_out_ffn2_kernel,
        out_shape=jax.ShapeDtypeStruct((nb, seq, D_MODEL), F32),
        grid=(nb, seq // tm),
        in_specs=in_specs,
        out_specs=pl.BlockSpec((1, tm, D_MODEL), lambda b, t: (b, t, 0)),
        scratch_shapes=[pltpu.VMEM((tm, D_MODEL), BF16), pltpu.VMEM((tm, D_MODEL), F32)],
        name="mix_out_ffn2",
        compiler_params=pltpu.CompilerParams(
            dimension_semantics=("arbitrary", "arbitrary"), vmem_limit_bytes=VMEM_LIMIT_BYTES),
    )(h1, o_t, oconv, gattn_t, p["woa"], p["woc"], p["g2"], p["wg2"], p["wu2"], p["wo2"], p["gfin"])


def _ffn_weights(w_in, w_out):
    def chunks(w):
        return w.reshape(D_MODEL, N_FF_CHUNKS, FF_CHUNK).transpose(1, 0, 2).astype(BF16)
    return chunks(w_in[:, :D_FF]), chunks(w_in[:, D_FF:]), w_out.reshape(N_FF_CHUNKS, FF_CHUNK, D_MODEL).astype(BF16)


def kernel(x, meta_tokens, ffn1_norm, ffn1_w_in, ffn1_w_out, mix_norm, w_mix_in, b_forget, q_norm, k_norm, conv_w, attn_out_norm, conv_out_norm, w_mix_out, ffn2_norm, ffn2_w_in, ffn2_w_out, final_norm):
    nb, seq, _ = x.shape
    wg1, wu1, wo1 = _ffn_weights(ffn1_w_in[0], ffn1_w_out[0])
    wg2, wu2, wo2 = _ffn_weights(ffn2_w_in[0], ffn2_w_out[0])
    wmix = w_mix_in[0]
    n_qkv = 3 * D_ATTN
    p = {
        "g1": ffn1_norm, "wg1": wg1, "wu1": wu1, "wo1": wo1,
        "gmix": mix_norm,
        "wfkT": jnp.concatenate([jnp.pad(wmix[:, n_qkv:n_qkv + HEADS].T, ((0, F_ROWS - HEADS), (0, 0))),
                                 wmix[:, D_ATTN:2 * D_ATTN].T], axis=0).astype(BF16),
        "wqT": wmix[:, :D_ATTN].T.astype(BF16),
        "wvT": wmix[:, 2 * D_ATTN:n_qkv].T.astype(BF16),
        "wcu": wmix[:, n_qkv + HEADS + D_CONV:].astype(BF16),
        "wb": wmix[:, n_qkv + HEADS:n_qkv + HEADS + D_CONV].astype(BF16),
        "bf": jnp.pad(b_forget[0], (0, F_ROWS - HEADS))[:, None],
        "gq": q_norm[0], "gk": k_norm[0],
        "cw": jnp.pad(conv_w[0], ((0, SUBLANES - CONV_WIDTH), (0, 0))),
        "gconv": conv_out_norm,
        "gattn": attn_out_norm[0],
        "woa": w_mix_out[0, :D_ATTN].astype(BF16), "woc": w_mix_out[0, D_ATTN:].astype(BF16),
        "g2": ffn2_norm, "wg2": wg2, "wu2": wu2, "wo2": wo2, "gfin": final_norm,
    }

    n_pad = META_TILE - N_META
    meta = jnp.pad(meta_tokens.astype(x.dtype), ((n_pad, 0), (0, 0)))[None]
    zeros_zc = jnp.zeros((SUBLANES, D_CONV), F32)
    zeros_fc = jnp.zeros((F_ROWS, LANES), F32)
    _, _, k_m, vT_m, _, _, zc_m, fc_m = _ffn1_mix_in(
        meta, jnp.full((1,), n_pad, jnp.int32), zeros_zc, zeros_fc, p, META_TILE)
    k_meta = k_m[0, :, n_pad:, :]
    vT_meta = vT_m[0, :, :, n_pad:]

    h1, qT, k, vT, f_cum, oconv, _, _ = _ffn1_mix_in(
        x, jnp.zeros((1,), jnp.int32), zc_m, fc_m, p, TOKEN_TILE)

    fq0 = f_cum[:, :, 0::Q_BLOCK].reshape(nb * HEADS, seq // Q_BLOCK)
    flast = f_cum[:, :, K_BLOCK - 1::K_BLOCK].reshape(nb * HEADS, seq // K_BLOCK)
    c_bound = (16.0 * 1.02 * jnp.max(jnp.abs(q_norm)) * jnp.max(jnp.abs(k_norm)) + 1.0).reshape(1).astype(F32)

    o_t = _fox_attention(qT, k, vT, k_meta, vT_meta, fq0, flast, c_bound)
    o_t = o_t.reshape(nb, D_ATTN, seq)
    return _mix_out_ffn2(h1, o_t, oconv, p, TOKEN_TILE)
```

```python
import jax
import jax.numpy as jnp
from jax import lax
from jax.experimental import pallas as pl
from jax.experimental.pallas import tpu as pltpu

D_MODEL = 1024
N_META = 16
D_ATTN = 512
D_CONV = 512
HEADS = 8
HEAD_DIM = 64
CONV_WIDTH = 3
D_FF = 2816
EPS = 1e-6

F32 = jnp.float32
BF16 = jnp.bfloat16

LANES = 128
SUBLANES = 8
MXU_DIM = 256
VMEM_LIMIT_BYTES = 60000 * 1024

TOKEN_TILE = 512
META_TILE = LANES
FF_CHUNK = MXU_DIM
N_FF_CHUNKS = D_FF // FF_CHUNK
Q_BLOCK = MXU_DIM
K_BLOCK = MXU_DIM
AUG_DIM = LANES
HEAD_GROUP = 4
MASKED = -1e30
LOG2E = 1.4426950408889634
F_ROWS = 2 * SUBLANES

EXP_UNDERFLOW = -104.0
BOUNDED_LOGIT_MAX = 120.0

assert D_FF % FF_CHUNK == 0 and FF_CHUNK % LANES == 0 and HEADS % HEAD_GROUP == 0


def _dot(a, b):
    return jnp.dot(a, b, preferred_element_type=F32)


def _dot_nt(a, b):
    return lax.dot_general(a, b, (((1,), (1,)), ((), ())), preferred_element_type=F32)


def _rmsnorm_rows(x, gain):
    ms = jnp.mean(x * x, axis=-1, keepdims=True)
    return x * lax.rsqrt(ms + EPS) * gain


def _split3(x):
    hi = x.astype(BF16)
    r1 = x - hi.astype(F32)
    mid = r1.astype(BF16)
    lo = (r1 - mid.astype(F32)).astype(BF16)
    return hi, mid, lo


def _swiglu(xn_ref, win_ref, wout_ref, acc_ref):
    acc_ref[...] = jnp.zeros_like(acc_ref)
    for c in range(N_FF_CHUNKS):
        lo, hi = c * FF_CHUNK, (c + 1) * FF_CHUNK
        xn = xn_ref[...]
        g = _dot(xn, win_ref[:, lo:hi])
        u = _dot(xn, win_ref[:, D_FF + lo:D_FF + hi])
        a = (g * jax.nn.sigmoid(g) * u).astype(BF16)
        acc_ref[...] += _dot(a, wout_ref[lo:hi, :])


def _ffn1_mix_in_kernel(
        npad_ref,
        x_ref, zc_in_ref, fc_in_ref,
        g1_ref, win_ref, wout_ref,
        gmix_ref, wfkT_ref, wqT_ref, wvT_ref, wcu_ref, wb_ref, bf_ref, tri_ref,
        gq_ref, gk_ref, cw_ref, gconv_ref,
        h1_ref, qT_ref, k_ref, vT_ref, f_ref, oconv_ref, zc_out_ref, fc_out_ref,
        xn_ref, acc_ref, zc_ref, fc_ref):
    t = pl.program_id(1)
    tm = x_ref.shape[1]

    @pl.when(t == 0)
    def _():
        zc_ref[...] = zc_in_ref[...]
        fc_ref[...] = fc_in_ref[...]

    x = x_ref[0]
    xn_ref[...] = _rmsnorm_rows(x, g1_ref[...]).astype(BF16)
    _swiglu(xn_ref, win_ref, wout_ref, acc_ref)
    h1 = x + acc_ref[...]
    h1_ref[0] = h1

    xn2 = _rmsnorm_rows(h1, gmix_ref[...]).astype(BF16)
    fkT = _dot_nt(wfkT_ref[...], xn2)
    cu = _dot(xn2, wcu_ref[...])

    fl = fkT[0:F_ROWS] + bf_ref[...]
    logf = jnp.minimum(fl, 0.0) - jnp.log(1.0 + jnp.exp(-jnp.abs(fl)))
    pos = t * tm + lax.broadcasted_iota(jnp.int32, logf.shape, 1)
    logf = jnp.where(pos >= npad_ref[0], logf, 0.0)
    pieces = _dot(jnp.concatenate(_split3(logf), axis=0), tri_ref[...])
    csum = pieces[0:F_ROWS] + pieces[F_ROWS:2 * F_ROWS] + pieces[2 * F_ROWS:3 * F_ROWS]
    f_all = csum + jnp.concatenate([fc_ref[...]] * (tm // LANES), axis=1)
    fc_new = jnp.broadcast_to(f_all[:, tm - 1:tm], fc_ref.shape)
    fc_ref[...] = fc_new
    fc_out_ref[...] = fc_new
    f_ref[0] = f_all[:HEADS]

    f_hi, f_mid, f_lo = (p.astype(F32) for p in _split3(f_all * LOG2E))
    row = lax.broadcasted_iota(jnp.int32, (SUBLANES, tm), 0)
    ones_mid = jnp.where(row < 6, 1.0, 0.0)
    pad_rows = jnp.zeros((AUG_DIM - HEAD_DIM - SUBLANES, tm), F32)

    def head_pieces(h):
        return (jnp.broadcast_to(f_hi[h:h + 1], (SUBLANES, tm)),
                jnp.broadcast_to(f_mid[h:h + 1], (SUBLANES, tm)),
                jnp.broadcast_to(f_lo[h:h + 1], (SUBLANES, tm)))

    def head_rmsnorm(x_t, gain_t):
        x3 = x_t.reshape(HEADS, HEAD_DIM, tm)
        return x3 * lax.rsqrt(jnp.mean(x3 * x3, axis=1, keepdims=True) + EPS) * gain_t[None]

    kn = head_rmsnorm(fkT[F_ROWS:F_ROWS + D_ATTN], gk_ref[...])
    qT = _dot_nt(wqT_ref[...], xn2)
    gate_b = _dot(xn2, wb_ref[...])
    for h in range(HEADS):
        fh, fm, fo = head_pieces(h)
        aug_k = jnp.where(row == 3, -fh, jnp.where(row == 4, -fm, jnp.where(row == 5, -fo, ones_mid)))
        k_aug_t = jnp.concatenate([kn[h], aug_k, pad_rows], axis=0)
        k_ref[0, h] = k_aug_t.T.astype(BF16)

    vT = _dot_nt(wvT_ref[...], xn2)
    z = cu[:, 0:D_CONV] * cu[:, D_CONV:2 * D_CONV]
    zc = zc_ref[...]
    rowz = lax.broadcasted_iota(jnp.int32, z.shape, 0)
    prev1 = jnp.broadcast_to(zc[7:8], z.shape)
    prev2 = jnp.broadcast_to(zc[6:7], z.shape)
    z1 = jnp.where(rowz == 0, prev1, pltpu.roll(z, 1, axis=0))
    z2 = jnp.where(rowz == 0, prev2, jnp.where(rowz == 1, prev1, pltpu.roll(z, 2, axis=0)))
    cw = cw_ref[...]
    y = cw[0:1] * z2 + cw[1:2] * z1 + cw[2:3] * z
    oconv_ref[0] = _rmsnorm_rows(gate_b * y, gconv_ref[...]).astype(BF16)
    zc_new = z[tm - SUBLANES:tm]
    zc_ref[...] = zc_new
    zc_out_ref[...] = zc_new

    qn = head_rmsnorm(qT, gq_ref[...] * (HEAD_DIM ** -0.5 * LOG2E))
    v3 = vT.reshape(HEADS, HEAD_DIM, tm)
    for h in range(HEADS):
        fh, fm, fo = head_pieces(h)
        aug_q = jnp.where(row == 0, fh, jnp.where(row == 1, fm, jnp.where(row == 2, fo, ones_mid)))
        qT_ref[0, h] = jnp.concatenate([qn[h], aug_q, pad_rows], axis=0).astype(BF16)
        vT_ref[0, h] = v3[h].astype(BF16)


def _const_spec(shape):
    nd = len(shape)
    return pl.BlockSpec(shape, lambda *_: (0,) * nd, pipeline_mode=pl.Buffered(1))


def _ffn1_mix_in(x, n_pad, zc_in, fc_in, p, tm):
    nb, seq, _ = x.shape
    nt = seq // tm
    tri = jnp.triu(jnp.ones((tm, tm), BF16))
    gq_t = jnp.broadcast_to(p["gq"][:, None], (HEAD_DIM, tm))
    gk_t = jnp.broadcast_to(p["gk"][:, None], (HEAD_DIM, tm))

    def tile3(last):
        return pl.BlockSpec((1, tm, last), lambda b, t, *_: (b, t, 0))

    in_specs = [
        tile3(D_MODEL),
        _const_spec((SUBLANES, D_CONV)), _const_spec((F_ROWS, LANES)),
        _const_spec((1, D_MODEL)),
        _const_spec((D_MODEL, 2 * D_FF)), _const_spec((D_FF, D_MODEL)),
        _const_spec((1, D_MODEL)), _const_spec((F_ROWS + D_ATTN, D_MODEL)),
        _const_spec((D_ATTN, D_MODEL)), _const_spec((D_ATTN, D_MODEL)),
        _const_spec((D_MODEL, 2 * D_CONV)), _const_spec((D_MODEL, D_CONV)),
        _const_spec((F_ROWS, 1)), _const_spec((tm, tm)),
        _const_spec((HEAD_DIM, tm)), _const_spec((HEAD_DIM, tm)),
        _const_spec((SUBLANES, D_CONV)), _const_spec((1, D_CONV)),
    ]
    out_shape = [
        jax.ShapeDtypeStruct((nb, seq, D_MODEL), F32),
        jax.ShapeDtypeStruct((nb, HEADS, AUG_DIM, seq), BF16),
        jax.ShapeDtypeStruct((nb, HEADS, seq, AUG_DIM), BF16),
        jax.ShapeDtypeStruct((nb, HEADS, HEAD_DIM, seq), BF16),
        jax.ShapeDtypeStruct((nb, HEADS, seq), F32),
        jax.ShapeDtypeStruct((nb, seq, D_CONV), BF16),
        jax.ShapeDtypeStruct((SUBLANES, D_CONV), F32),
        jax.ShapeDtypeStruct((F_ROWS, LANES), F32),
    ]
    out_specs = [
        tile3(D_MODEL),
        pl.BlockSpec((1, HEADS, AUG_DIM, tm), lambda b, t, *_: (b, 0, 0, t)),
        pl.BlockSpec((1, HEADS, tm, AUG_DIM), lambda b, t, *_: (b, 0, t, 0)),
        pl.BlockSpec((1, HEADS, HEAD_DIM, tm), lambda b, t, *_: (b, 0, 0, t)),
        pl.BlockSpec((1, HEADS, tm), lambda b, t, *_: (b, 0, t)),
        tile3(D_CONV),
        pl.BlockSpec((SUBLANES, D_CONV), lambda b, t, *_: (0, 0)),
        pl.BlockSpec((F_ROWS, LANES), lambda b, t, *_: (0, 0)),
    ]
    grid_spec = pltpu.PrefetchScalarGridSpec(
        num_scalar_prefetch=1, grid=(nb, nt), in_specs=in_specs, out_specs=out_specs,
        scratch_shapes=[
            pltpu.VMEM((tm, D_MODEL), BF16),
            pltpu.VMEM((tm, D_MODEL), F32),
            pltpu.VMEM((SUBLANES, D_CONV), F32),
            pltpu.VMEM((F_ROWS, LANES), F32),
        ])
    return pl.pallas_call(
        _ffn1_mix_in_kernel, out_shape=out_shape, grid_spec=grid_spec, name="ffn1_mix_in",
        compiler_params=pltpu.CompilerParams(
            dimension_semantics=("arbitrary", "arbitrary"), vmem_limit_bytes=VMEM_LIMIT_BYTES),
    )(n_pad, x, zc_in, fc_in,
      p["g1"], p["win1"], p["wout1"],
      p["gmix"], p["wfkT"], p["wqT"], p["wvT"], p["wcu"], p["wb"], p["bf"], tri,
      gq_t, gk_t, p["cw"], p["gconv"])


def _fox_attention_kernel(fq0_ref, flast_ref, cb_ref,
                          qT_ref, k_ref, vT_ref, km_ref, vmT_ref,
                          o_ref,
                          acc_ref, l_ref):
    bh0 = pl.program_id(0) * HEADS + pl.program_id(1) * HEAD_GROUP
    heads = range(HEAD_GROUP)
    nq = qT_ref.shape[3] // Q_BLOCK
    c_bound = cb_ref[0]
    key_idx = lax.broadcasted_iota(jnp.int32, (K_BLOCK, Q_BLOCK), 0)
    qry_idx = lax.broadcasted_iota(jnp.int32, (K_BLOCK, Q_BLOCK), 1)
    causal = key_idx <= qry_idx

    def block_live(g, i, j):
        return fq0_ref[bh0 + g, i] - flast_ref[bh0 + g, jnp.maximum(j, 0)] + c_bound >= EXP_UNDERFLOW

    def sublane_partial_sum(p):
        return jnp.sum(p.reshape(p.shape[0] // SUBLANES, SUBLANES, p.shape[1]), axis=0)

    def bounded_tile(k_blk, v_t, q_t, masked):
        s = _dot(k_blk, q_t)
        if masked:
            s = jnp.where(causal, s, MASKED)
        p = jnp.exp2(s)
        return sublane_partial_sum(p), _dot(v_t, p.astype(BF16))

    def bounded_q_block(i, with_prev):
        q0 = pl.multiple_of(i * Q_BLOCK, Q_BLOCK)
        q_t = [qT_ref[0, g, :, pl.ds(q0, Q_BLOCK)] for g in heads]
        p0 = pl.multiple_of(jnp.maximum(i - 1, 0) * K_BLOCK, K_BLOCK)
        s_m = [_dot(km_ref[g], q_t[g]) for g in heads]
        s_d = [_dot(k_ref[0, g, pl.ds(q0, K_BLOCK), :], q_t[g]) for g in heads]
        s_p = [_dot(k_ref[0, g, pl.ds(p0, K_BLOCK), :], q_t[g]) for g in heads] if with_prev else None
        p_m = [jnp.exp2(s) for s in s_m]
        p_d = [jnp.exp2(jnp.where(causal, s, MASKED)) for s in s_d]
        p_p = [jnp.exp2(s) for s in s_p] if with_prev else None
        for g in heads:
            l_g = sublane_partial_sum(p_m[g]) + sublane_partial_sum(p_d[g])
            acc_g = (_dot(vmT_ref[g], p_m[g].astype(BF16))
                     + _dot(vT_ref[0, g, :, pl.ds(q0, K_BLOCK)], p_d[g].astype(BF16)))
            if with_prev:
                l_g = l_g + sublane_partial_sum(p_p[g])
                acc_g = acc_g + _dot(vT_ref[0, g, :, pl.ds(p0, K_BLOCK)], p_p[g].astype(BF16))
            l_ref[g] = l_g
            acc_ref[g] = acc_g

        def cond(j):
            live = block_live(0, i, j)
            for g in heads[1:]:
                live = jnp.logical_or(live, block_live(g, i, j))
            return jnp.logical_and(j >= 0, live)

        def body(j):
            k0 = pl.multiple_of(j * K_BLOCK, K_BLOCK)
            for g in heads:
                l_j, acc_j = bounded_tile(k_ref[0, g, pl.ds(k0, K_BLOCK), :],
                                          vT_ref[0, g, :, pl.ds(k0, K_BLOCK)], q_t[g], False)
                l_ref[g] += l_j
                acc_ref[g] += acc_j
            return j - 1

        lax.while_loop(cond, body, i - 2)
        for g in heads:
            l_tot = jnp.sum(l_ref[g], axis=0, keepdims=True)
            o_ref[0, g, :, pl.ds(q0, Q_BLOCK)] = (acc_ref[g] / l_tot).astype(o_ref.dtype)

    @pl.when(c_bound <= BOUNDED_LOGIT_MAX)
    def _():
        bounded_q_block(0, False)

        def step(i, carry):
            bounded_q_block(i, True)
            return carry

        lax.fori_loop(1, nq, step, 0)

    def online_q_block(g, i):
        q0 = pl.multiple_of(i * Q_BLOCK, Q_BLOCK)
        q_t = qT_ref[0, g, :, pl.ds(q0, Q_BLOCK)]
        s_m = _dot(km_ref[g], q_t)
        m = jnp.max(s_m, axis=0, keepdims=True)
        p_m = jnp.exp2(s_m - m)
        l = jnp.sum(p_m, axis=0, keepdims=True)
        acc = _dot(vmT_ref[g], p_m.astype(BF16))

        def online_step(state, k0, masked):
            m, l, acc = state
            s = _dot(k_ref[0, g, pl.ds(k0, K_BLOCK), :], q_t)
            if masked:
                s = jnp.where(causal, s, MASKED)
            m_new = jnp.maximum(m, jnp.max(s, axis=0, keepdims=True))
            alpha = jnp.exp2(m - m_new)
            p = jnp.exp2(s - m_new)
            l = alpha * l + jnp.sum(p, axis=0, keepdims=True)
            acc = alpha * acc + _dot(vT_ref[0, g, :, pl.ds(k0, K_BLOCK)], p.astype(BF16))
            return m_new, l, acc

        state = online_step((m, l, acc), q0, True)

        def cond(carry):
            return jnp.logical_and(carry[0] >= 0, block_live(g, i, carry[0]))

        def body(carry):
            j = carry[0]
            return (j - 1,) + online_step(carry[1:], pl.multiple_of(j * K_BLOCK, K_BLOCK), False)

        _, m, l, acc = lax.while_loop(cond, body, (i - 1,) + state)
        o_ref[0, g, :, pl.ds(q0, Q_BLOCK)] = (acc / l).astype(o_ref.dtype)

    @pl.when(c_bound > BOUNDED_LOGIT_MAX)
    def _():
        def head_loop(g, carry):
            def step(i, inner):
                online_q_block(g, i)
                return inner

            lax.fori_loop(0, nq, step, 0)
            return carry

        lax.fori_loop(0, HEAD_GROUP, head_loop, 0)


def _fox_attention(qT, k, vT, k_meta, vT_meta, fq0, flast, c_bound):
    nb, _, _, seq = qT.shape
    smem = pl.BlockSpec(memory_space=pltpu.SMEM)
    hg = HEAD_GROUP
    return pl.pallas_call(
        _fox_attention_kernel,
        out_shape=jax.ShapeDtypeStruct((nb, HEADS, HEAD_DIM, seq), BF16),
        grid=(nb, HEADS // hg),
        in_specs=[
            smem, smem, smem,
            pl.BlockSpec((1, hg, AUG_DIM, seq), lambda b, h: (b, h, 0, 0)),
            pl.BlockSpec((1, hg, seq, AUG_DIM), lambda b, h: (b, h, 0, 0)),
            pl.BlockSpec((1, hg, HEAD_DIM, seq), lambda b, h: (b, h, 0, 0)),
            pl.BlockSpec((hg, N_META, AUG_DIM), lambda b, h: (h, 0, 0)),
            pl.BlockSpec((hg, HEAD_DIM, N_META), lambda b, h: (h, 0, 0)),
        ],
        out_specs=pl.BlockSpec((1, hg, HEAD_DIM, seq), lambda b, h: (b, h, 0, 0)),
        scratch_shapes=[pltpu.VMEM((hg, HEAD_DIM, Q_BLOCK), F32),
                        pltpu.VMEM((hg, SUBLANES, Q_BLOCK), F32)],
        name="fox_attention",
        compiler_params=pltpu.CompilerParams(
            dimension_semantics=("arbitrary", "arbitrary"), vmem_limit_bytes=VMEM_LIMIT_BYTES),
    )(fq0, flast, c_bound, qT, k, vT, k_meta, vT_meta)


def _mix_out_ffn2_kernel(h1_ref, oT_ref, oconv_ref,
                         gattn_ref, woa_ref, woc_ref,
                         g2_ref, win_ref, wout_ref, gfin_ref,
                         out_ref,
                         xn_ref, acc_ref):
    o_t = oT_ref[0].astype(F32)
    ms = jnp.mean(o_t * o_t, axis=0, keepdims=True)
    o_n = (o_t * lax.rsqrt(ms + EPS) * gattn_ref[...]).T.astype(BF16)
    mix = _dot(o_n, woa_ref[...]) + _dot(oconv_ref[0], woc_ref[...])
    h2 = h1_ref[0] + mix
    xn_ref[...] = _rmsnorm_rows(h2, g2_ref[...]).astype(BF16)
    _swiglu(xn_ref, win_ref, wout_ref, acc_ref)
    h3 = h2 + acc_ref[...]
    out_ref[0] = _rmsnorm_rows(h3, gfin_ref[...])


def _mix_out_ffn2(h1, o_t, oconv, p, tm):
    nb, seq, _ = h1.shape
    gattn_t = jnp.broadcast_to(p["gattn"][:, None], (D_ATTN, tm))
    in_specs = [
        pl.BlockSpec((1, tm, D_MODEL), lambda b, t: (b, t, 0)),
        pl.BlockSpec((1, D_ATTN, tm), lambda b, t: (b, 0, t)),
        pl.BlockSpec((1, tm, D_CONV), lambda b, t: (b, t, 0)),
        _const_spec((D_ATTN, tm)), _const_spec((D_ATTN, D_MODEL)), _const_spec((D_CONV, D_MODEL)),
        _const_spec((1, D_MODEL)),
        _const_spec((D_MODEL, 2 * D_FF)), _const_spec((D_FF, D_MODEL)),
        _const_spec((1, D_MODEL)),
    ]
    return pl.pallas_call(
        _mix_out_ffn2_kernel,
        out_shape=jax.ShapeDtypeStruct((nb, seq, D_MODEL), F32),
        grid=(nb, seq // tm),
        in_specs=in_specs,
        out_specs=pl.BlockSpec((1, tm, D_MODEL), lambda b, t: (b, t, 0)),
        scratch_shapes=[pltpu.VMEM((tm, D_MODEL), BF16), pltpu.VMEM((tm, D_MODEL), F32)],
        name="mix_out_ffn2",
        compiler_params=pltpu.CompilerParams(
            dimension_semantics=("arbitrary", "arbitrary"), vmem_limit_bytes=VMEM_LIMIT_BYTES),
    )(h1, o_t, oconv, gattn_t, p["woa"], p["woc"], p["g2"], p["win2"], p["wout2"], p["gfin"])


def kernel(x, meta_tokens, ffn1_norm, ffn1_w_in, ffn1_w_out, mix_norm, w_mix_in, b_forget, q_norm, k_norm, conv_w, attn_out_norm, conv_out_norm, w_mix_out, ffn2_norm, ffn2_w_in, ffn2_w_out, final_norm):
    nb, seq, _ = x.shape
    wmix = w_mix_in[0]
    n_qkv = 3 * D_ATTN
    p = {
        "g1": ffn1_norm, "win1": ffn1_w_in[0].astype(BF16), "wout1": (0.5 * ffn1_w_out[0]).astype(BF16),
        "gmix": mix_norm,
        "wfkT": jnp.concatenate([jnp.pad(wmix[:, n_qkv:n_qkv + HEADS].T, ((0, F_ROWS - HEADS), (0, 0))),
                                 wmix[:, D_ATTN:2 * D_ATTN].T], axis=0).astype(BF16),
        "wqT": wmix[:, :D_ATTN].T.astype(BF16),
        "wvT": wmix[:, 2 * D_ATTN:n_qkv].T.astype(BF16),
        "wcu": wmix[:, n_qkv + HEADS + D_CONV:].astype(BF16),
        "wb": wmix[:, n_qkv + HEADS:n_qkv + HEADS + D_CONV].astype(BF16),
        "bf": jnp.pad(b_forget[0], (0, F_ROWS - HEADS))[:, None],
        "gq": q_norm[0], "gk": k_norm[0],
        "cw": jnp.pad(conv_w[0], ((0, SUBLANES - CONV_WIDTH), (0, 0))),
        "gconv": conv_out_norm,
        "gattn": attn_out_norm[0],
        "woa": w_mix_out[0, :D_ATTN].astype(BF16), "woc": w_mix_out[0, D_ATTN:].astype(BF16),
        "g2": ffn2_norm, "win2": ffn2_w_in[0].astype(BF16), "wout2": (0.5 * ffn2_w_out[0]).astype(BF16),
        "gfin": final_norm,
    }

    n_pad = META_TILE - N_META
    meta = jnp.pad(meta_tokens.astype(x.dtype), ((n_pad, 0), (0, 0)))[None]
    zeros_zc = jnp.zeros((SUBLANES, D_CONV), F32)
    zeros_fc = jnp.zeros((F_ROWS, LANES), F32)
    _, _, k_m, vT_m, _, _, zc_m, fc_m = _ffn1_mix_in(
        meta, jnp.full((1,), n_pad, jnp.int32), zeros_zc, zeros_fc, p, META_TILE)
    k_meta = k_m[0, :, n_pad:, :]
    vT_meta = vT_m[0, :, :, n_pad:]

    h1, qT, k, vT, f_cum, oconv, _, _ = _ffn1_mix_in(
        x, jnp.zeros((1,), jnp.int32), zc_m, fc_m, p, TOKEN_TILE)

    fq0 = f_cum[:, :, 0::Q_BLOCK].reshape(nb * HEADS, seq // Q_BLOCK)
    flast = f_cum[:, :, K_BLOCK - 1::K_BLOCK].reshape(nb * HEADS, seq // K_BLOCK)
    c_bound = (16.0 * 1.02 * jnp.max(jnp.abs(q_norm)) * jnp.max(jnp.abs(k_norm)) + 1.0).reshape(1).astype(F32)

    o_t = _fox_attention(qT, k, vT, k_meta, vT_meta, fq0, flast, c_bound)
    o_t = o_t.reshape(nb, D_ATTN, seq)
    return _mix_out_ffn2(h1, o_t, oconv, p, TOKEN_TILE)
```

```python
import jax
import jax.numpy as jnp
from jax import lax
from jax.experimental import pallas as pl
from jax.experimental.pallas import tpu as pltpu

D_MODEL = 1024
N_META = 16
D_ATTN = 512
D_CONV = 512
HEADS = 8
HEAD_DIM = 64
CONV_WIDTH = 3
D_FF = 2816
EPS = 1e-6

F32 = jnp.float32
BF16 = jnp.bfloat16

LANES = 128
SUBLANES = 8
MXU_DIM = 256
VMEM_LIMIT_BYTES = 60000 * 1024

TOKEN_TILE = 512
META_TILE = LANES
FF_CHUNK = MXU_DIM
N_FF_CHUNKS = D_FF // FF_CHUNK
Q_BLOCK = MXU_DIM
K_BLOCK = MXU_DIM
AUG_DIM = LANES
HEAD_GROUP = 4
Q_PER_STEP = 4
MASKED = -1e30
LOG2E = 1.4426950408889634
F_ROWS = 2 * SUBLANES

EXP_UNDERFLOW = -104.0
BOUNDED_LOGIT_MAX = 120.0

assert D_FF % FF_CHUNK == 0 and FF_CHUNK % LANES == 0 and HEADS % HEAD_GROUP == 0


def _dot(a, b):
    return jnp.dot(a, b, preferred_element_type=F32)


def _dot_nt(a, b):
    return lax.dot_general(a, b, (((1,), (1,)), ((), ())), preferred_element_type=F32)


def _rmsnorm_rows(x, gain):
    ms = jnp.mean(x * x, axis=-1, keepdims=True)
    return x * lax.rsqrt(ms + EPS) * gain


def _split3(x):
    hi = x.astype(BF16)
    r1 = x - hi.astype(F32)
    mid = r1.astype(BF16)
    lo = (r1 - mid.astype(F32)).astype(BF16)
    return hi, mid, lo


def _swiglu(xn_ref, win_ref, wout_ref, acc_ref):
    acc_ref[...] = jnp.zeros_like(acc_ref)
    for c in range(N_FF_CHUNKS):
        lo, hi = c * FF_CHUNK, (c + 1) * FF_CHUNK
        xn = xn_ref[...]
        g = _dot(xn, win_ref[:, lo:hi])
        u = _dot(xn, win_ref[:, D_FF + lo:D_FF + hi])
        a = (g * jax.nn.sigmoid(g) * u).astype(BF16)
        acc_ref[...] += _dot(a, wout_ref[lo:hi, :])


def _ffn1_mix_in_kernel(
        npad_ref,
        x_ref, zc_in_ref, fc_in_ref,
        g1_ref, win_ref, wout_ref,
        gmix_ref, wfkT_ref, wqT_ref, wvT_ref, wcu_ref, wb_ref, bf_ref, tri_ref,
        gq_ref, gk_ref, cw_ref, gconv_ref,
        h1_ref, qT_ref, k_ref, vT_ref, f_ref, oconv_ref, zc_out_ref, fc_out_ref,
        xn_ref, acc_ref, zc_ref, fc_ref):
    t = pl.program_id(1)
    tm = x_ref.shape[1]

    @pl.when(t == 0)
    def _():
        zc_ref[...] = zc_in_ref[...]
        fc_ref[...] = fc_in_ref[...]

    x = x_ref[0]
    xn_ref[...] = _rmsnorm_rows(x, g1_ref[...]).astype(BF16)
    _swiglu(xn_ref, win_ref, wout_ref, acc_ref)
    h1 = x + acc_ref[...]
    h1_ref[0] = h1

    xn2 = _rmsnorm_rows(h1, gmix_ref[...]).astype(BF16)
    fkT = _dot_nt(wfkT_ref[...], xn2)
    cu = _dot(xn2, wcu_ref[...])

    fl = fkT[0:F_ROWS] + bf_ref[...]
    logf = jnp.minimum(fl, 0.0) - jnp.log(1.0 + jnp.exp(-jnp.abs(fl)))
    pos = t * tm + lax.broadcasted_iota(jnp.int32, logf.shape, 1)
    logf = jnp.where(pos >= npad_ref[0], logf, 0.0)
    pieces = _dot(jnp.concatenate(_split3(logf), axis=0), tri_ref[...])
    csum = pieces[0:F_ROWS] + pieces[F_ROWS:2 * F_ROWS] + pieces[2 * F_ROWS:3 * F_ROWS]
    f_all = csum + jnp.concatenate([fc_ref[...]] * (tm // LANES), axis=1)
    fc_new = jnp.broadcast_to(f_all[:, tm - 1:tm], fc_ref.shape)
    fc_ref[...] = fc_new
    fc_out_ref[...] = fc_new
    f_ref[0] = f_all[:HEADS]

    f_hi, f_mid, f_lo = (p.astype(F32) for p in _split3(f_all * LOG2E))
    row = lax.broadcasted_iota(jnp.int32, (SUBLANES, tm), 0)
    ones_mid = jnp.where(row < 6, 1.0, 0.0)
    pad_rows = jnp.zeros((AUG_DIM - HEAD_DIM - SUBLANES, tm), F32)

    def head_pieces(h):
        return (jnp.broadcast_to(f_hi[h:h + 1], (SUBLANES, tm)),
                jnp.broadcast_to(f_mid[h:h + 1], (SUBLANES, tm)),
                jnp.broadcast_to(f_lo[h:h + 1], (SUBLANES, tm)))

    def head_rmsnorm(x_t, gain_t):
        x3 = x_t.reshape(HEADS, HEAD_DIM, tm)
        return x3 * lax.rsqrt(jnp.mean(x3 * x3, axis=1, keepdims=True) + EPS) * gain_t[None]

    kn = head_rmsnorm(fkT[F_ROWS:F_ROWS + D_ATTN], gk_ref[...])
    qT = _dot_nt(wqT_ref[...], xn2)
    gate_b = _dot(xn2, wb_ref[...])
    for h in range(HEADS):
        fh, fm, fo = head_pieces(h)
        aug_k = jnp.where(row == 3, -fh, jnp.where(row == 4, -fm, jnp.where(row == 5, -fo, ones_mid)))
        k_aug_t = jnp.concatenate([kn[h], aug_k, pad_rows], axis=0)
        k_ref[0, h] = k_aug_t.T.astype(BF16)

    vT = _dot_nt(wvT_ref[...], xn2)
    z = cu[:, 0:D_CONV] * cu[:, D_CONV:2 * D_CONV]
    zc = zc_ref[...]
    rowz = lax.broadcasted_iota(jnp.int32, z.shape, 0)
    prev1 = jnp.broadcast_to(zc[7:8], z.shape)
    prev2 = jnp.broadcast_to(zc[6:7], z.shape)
    z1 = jnp.where(rowz == 0, prev1, pltpu.roll(z, 1, axis=0))
    z2 = jnp.where(rowz == 0, prev2, jnp.where(rowz == 1, prev1, pltpu.roll(z, 2, axis=0)))
    cw = cw_ref[...]
    y = cw[0:1] * z2 + cw[1:2] * z1 + cw[2:3] * z
    oconv_ref[0] = _rmsnorm_rows(gate_b * y, gconv_ref[...]).astype(BF16)
    zc_new = z[tm - SUBLANES:tm]
    zc_ref[...] = zc_new
    zc_out_ref[...] = zc_new

    qn = head_rmsnorm(qT, gq_ref[...] * (HEAD_DIM ** -0.5 * LOG2E))
    v3 = vT.reshape(HEADS, HEAD_DIM, tm)
    for h in range(HEADS):
        fh, fm, fo = head_pieces(h)
        aug_q = jnp.where(row == 0, fh, jnp.where(row == 1, fm, jnp.where(row == 2, fo, ones_mid)))
        qT_ref[0, h] = jnp.concatenate([qn[h], aug_q, pad_rows], axis=0).astype(BF16)
        vT_ref[0, h] = v3[h].astype(BF16)


def _const_spec(shape):
    nd = len(shape)
    return pl.BlockSpec(shape, lambda *_: (0,) * nd, pipeline_mode=pl.Buffered(1))


def _ffn1_mix_in(x, n_pad, zc_in, fc_in, p, tm):
    nb, seq, _ = x.shape
    nt = seq // tm
    tri = jnp.triu(jnp.ones((tm, tm), BF16))
    gq_t = jnp.broadcast_to(p["gq"][:, None], (HEAD_DIM, tm))
    gk_t = jnp.broadcast_to(p["gk"][:, None], (HEAD_DIM, tm))

    def tile3(last):
        return pl.BlockSpec((1, tm, last), lambda b, t, *_: (b, t, 0))

    in_specs = [
        tile3(D_MODEL),
        _const_spec((SUBLANES, D_CONV)), _const_spec((F_ROWS, LANES)),
        _const_spec((1, D_MODEL)),
        _const_spec((D_MODEL, 2 * D_FF)), _const_spec((D_FF, D_MODEL)),
        _const_spec((1, D_MODEL)), _const_spec((F_ROWS + D_ATTN, D_MODEL)),
        _const_spec((D_ATTN, D_MODEL)), _const_spec((D_ATTN, D_MODEL)),
        _const_spec((D_MODEL, 2 * D_CONV)), _const_spec((D_MODEL, D_CONV)),
        _const_spec((F_ROWS, 1)), _const_spec((tm, tm)),
        _const_spec((HEAD_DIM, tm)), _const_spec((HEAD_DIM, tm)),
        _const_spec((SUBLANES, D_CONV)), _const_spec((1, D_CONV)),
    ]
    out_shape = [
        jax.ShapeDtypeStruct((nb, seq, D_MODEL), F32),
        jax.ShapeDtypeStruct((nb, HEADS, AUG_DIM, seq), BF16),
        jax.ShapeDtypeStruct((nb, HEADS, seq, AUG_DIM), BF16),
        jax.ShapeDtypeStruct((nb, HEADS, HEAD_DIM, seq), BF16),
        jax.ShapeDtypeStruct((nb, HEADS, seq), F32),
        jax.ShapeDtypeStruct((nb, seq, D_CONV), BF16),
        jax.ShapeDtypeStruct((SUBLANES, D_CONV), F32),
        jax.ShapeDtypeStruct((F_ROWS, LANES), F32),
    ]
    out_specs = [
        tile3(D_MODEL),
        pl.BlockSpec((1, HEADS, AUG_DIM, tm), lambda b, t, *_: (b, 0, 0, t)),
        pl.BlockSpec((1, HEADS, tm, AUG_DIM), lambda b, t, *_: (b, 0, t, 0)),
        pl.BlockSpec((1, HEADS, HEAD_DIM, tm), lambda b, t, *_: (b, 0, 0, t)),
        pl.BlockSpec((1, HEADS, tm), lambda b, t, *_: (b, 0, t)),
        tile3(D_CONV),
        pl.BlockSpec((SUBLANES, D_CONV), lambda b, t, *_: (0, 0)),
        pl.BlockSpec((F_ROWS, LANES), lambda b, t, *_: (0, 0)),
    ]
    grid_spec = pltpu.PrefetchScalarGridSpec(
        num_scalar_prefetch=1, grid=(nb, nt), in_specs=in_specs, out_specs=out_specs,
        scratch_shapes=[
            pltpu.VMEM((tm, D_MODEL), BF16),
            pltpu.VMEM((tm, D_MODEL), F32),
            pltpu.VMEM((SUBLANES, D_CONV), F32),
            pltpu.VMEM((F_ROWS, LANES), F32),
        ])
    return pl.pallas_call(
        _ffn1_mix_in_kernel, out_shape=out_shape, grid_spec=grid_spec, name="ffn1_mix_in",
        compiler_params=pltpu.CompilerParams(
            dimension_semantics=("arbitrary", "arbitrary"), vmem_limit_bytes=VMEM_LIMIT_BYTES),
    )(n_pad, x, zc_in, fc_in,
      p["g1"], p["win1"], p["wout1"],
      p["gmix"], p["wfkT"], p["wqT"], p["wvT"], p["wcu"], p["wb"], p["bf"], tri,
      gq_t, gk_t, p["cw"], p["gconv"])


def _fox_attention_kernel(fq0_ref, flast_ref, cb_ref,
                          qT_ref, k_ref, vT_ref, km_ref, vmT_ref,
                          o_ref,
                          acc_ref, l_ref):
    bh0 = pl.program_id(0) * HEADS + pl.program_id(1) * HEAD_GROUP
    heads = range(HEAD_GROUP)
    nq = qT_ref.shape[3] // Q_BLOCK
    c_bound = cb_ref[0]
    key_idx = lax.broadcasted_iota(jnp.int32, (K_BLOCK, Q_BLOCK), 0)
    qry_idx = lax.broadcasted_iota(jnp.int32, (K_BLOCK, Q_BLOCK), 1)
    causal = key_idx <= qry_idx

    def block_live(g, i, j):
        return fq0_ref[bh0 + g, i] - flast_ref[bh0 + g, jnp.maximum(j, 0)] + c_bound >= EXP_UNDERFLOW

    def sublane_partial_sum(p):
        return jnp.sum(p.reshape(p.shape[0] // SUBLANES, SUBLANES, p.shape[1]), axis=0)

    def bounded_tile(k_blk, v_t, q_t):
        p = jnp.exp2(_dot(k_blk, q_t))
        return sublane_partial_sum(p), _dot(v_t, p.astype(BF16))

    def bounded_q_blocks(i_first):
        chains = [(r, g) for r in range(Q_PER_STEP) for g in heads]
        blk = [i_first + r for r in range(Q_PER_STEP)]
        q0 = [pl.multiple_of(i * Q_BLOCK, Q_BLOCK) for i in blk]
        p0 = [pl.multiple_of(jnp.maximum(i - 1, 0) * K_BLOCK, K_BLOCK) for i in blk]
        first_mask = jnp.where(i_first >= 1, 0.0, MASKED)
        q_t = [qT_ref[0, g, :, pl.ds(q0[r], Q_BLOCK)] for r, g in chains]
        s_m = [_dot(km_ref[g], q_t[c]) for c, (r, g) in enumerate(chains)]
        s_d = [_dot(k_ref[0, g, pl.ds(q0[r], K_BLOCK), :], q_t[c]) for c, (r, g) in enumerate(chains)]
        s_p = [_dot(k_ref[0, g, pl.ds(p0[r], K_BLOCK), :], q_t[c]) for c, (r, g) in enumerate(chains)]
        p_m = [jnp.exp2(s) for s in s_m]
        p_d = [jnp.exp2(jnp.where(causal, s, MASKED)) for s in s_d]
        p_p = [jnp.exp2(s + first_mask if r == 0 else s) for s, (r, g) in zip(s_p, chains)]
        for c, (r, g) in enumerate(chains):
            l_ref[c] = sublane_partial_sum(p_m[c]) + sublane_partial_sum(p_d[c]) + sublane_partial_sum(p_p[c])
            acc_ref[c] = (_dot(vmT_ref[g], p_m[c].astype(BF16))
                          + _dot(vT_ref[0, g, :, pl.ds(q0[r], K_BLOCK)], p_d[c].astype(BF16))
                          + _dot(vT_ref[0, g, :, pl.ds(p0[r], K_BLOCK)], p_p[c].astype(BF16)))

        for r in range(Q_PER_STEP):
            i = blk[r]

            def cond(j, i=i):
                live = block_live(0, i, j)
                for g in heads[1:]:
                    live = jnp.logical_or(live, block_live(g, i, j))
                return jnp.logical_and(j >= 0, live)

            def body(j, r=r):
                k0 = pl.multiple_of(j * K_BLOCK, K_BLOCK)
                for g in heads:
                    c = r * HEAD_GROUP + g
                    l_j, acc_j = bounded_tile(k_ref[0, g, pl.ds(k0, K_BLOCK), :],
                                              vT_ref[0, g, :, pl.ds(k0, K_BLOCK)],
                                              qT_ref[0, g, :, pl.ds(q0[r], Q_BLOCK)])
                    l_ref[c] += l_j
                    acc_ref[c] += acc_j
                return j - 1

            lax.while_loop(cond, body, i - 2)
        for c, (r, g) in enumerate(chains):
            l_tot = jnp.sum(l_ref[c], axis=0, keepdims=True)
            o_ref[0, g, :, pl.ds(q0[r], Q_BLOCK)] = (acc_ref[c] / l_tot).astype(o_ref.dtype)

    @pl.when(c_bound <= BOUNDED_LOGIT_MAX)
    def _():
        def step(n, carry):
            bounded_q_blocks(n * Q_PER_STEP)
            return carry

        lax.fori_loop(0, nq // Q_PER_STEP, step, 0)

    def online_q_block(g, i):
        q0 = pl.multiple_of(i * Q_BLOCK, Q_BLOCK)
        q_t = qT_ref[0, g, :, pl.ds(q0, Q_BLOCK)]
        s_m = _dot(km_ref[g], q_t)
        m = jnp.max(s_m, axis=0, keepdims=True)
        p_m = jnp.exp2(s_m - m)
        l = jnp.sum(p_m, axis=0, keepdims=True)
        acc = _dot(vmT_ref[g], p_m.astype(BF16))

        def online_step(state, k0, masked):
            m, l, acc = state
            s = _dot(k_ref[0, g, pl.ds(k0, K_BLOCK), :], q_t)
            if masked:
                s = jnp.where(causal, s, MASKED)
            m_new = jnp.maximum(m, jnp.max(s, axis=0, keepdims=True))
            alpha = jnp.exp2(m - m_new)
            p = jnp.exp2(s - m_new)
            l = alpha * l + jnp.sum(p, axis=0, keepdims=True)
            acc = alpha * acc + _dot(vT_ref[0, g, :, pl.ds(k0, K_BLOCK)], p.astype(BF16))
            return m_new, l, acc

        state = online_step((m, l, acc), q0, True)

        def cond(carry):
            return jnp.logical_and(carry[0] >= 0, block_live(g, i, carry[0]))

        def body(carry):
            j = carry[0]
            return (j - 1,) + online_step(carry[1:], pl.multiple_of(j * K_BLOCK, K_BLOCK), False)

        _, m, l, acc = lax.while_loop(cond, body, (i - 1,) + state)
        o_ref[0, g, :, pl.ds(q0, Q_BLOCK)] = (acc / l).astype(o_ref.dtype)

    @pl.when(c_bound > BOUNDED_LOGIT_MAX)
    def _():
        def head_loop(g, carry):
            def step(i, inner):
                online_q_block(g, i)
                return inner

            lax.fori_loop(0, nq, step, 0)
            return carry

        lax.fori_loop(0, HEAD_GROUP, head_loop, 0)


def _fox_attention(qT, k, vT, k_meta, vT_meta, fq0, flast, c_bound):
    nb, _, _, seq = qT.shape
    assert seq % (Q_PER_STEP * Q_BLOCK) == 0 and Q_BLOCK == K_BLOCK
    smem = pl.BlockSpec(memory_space=pltpu.SMEM)
    hg = HEAD_GROUP
    return pl.pallas_call(
        _fox_attention_kernel,
        out_shape=jax.ShapeDtypeStruct((nb, HEADS, HEAD_DIM, seq), BF16),
        grid=(nb, HEADS // hg),
        in_specs=[
            smem, smem, smem,
            pl.BlockSpec((1, hg, AUG_DIM, seq), lambda b, h: (b, h, 0, 0)),
            pl.BlockSpec((1, hg, seq, AUG_DIM), lambda b, h: (b, h, 0, 0)),
            pl.BlockSpec((1, hg, HEAD_DIM, seq), lambda b, h: (b, h, 0, 0)),
            pl.BlockSpec((hg, N_META, AUG_DIM), lambda b, h: (h, 0, 0)),
            pl.BlockSpec((hg, HEAD_DIM, N_META), lambda b, h: (h, 0, 0)),
        ],
        out_specs=pl.BlockSpec((1, hg, HEAD_DIM, seq), lambda b, h: (b, h, 0, 0)),
        scratch_shapes=[pltpu.VMEM((Q_PER_STEP * hg, HEAD_DIM, Q_BLOCK), F32),
                        pltpu.VMEM((Q_PER_STEP * hg, SUBLANES, Q_BLOCK), F32)],
        name="fox_attention",
        compiler_params=pltpu.CompilerParams(
            dimension_semantics=("arbitrary", "arbitrary"), vmem_limit_bytes=VMEM_LIMIT_BYTES),
    )(fq0, flast, c_bound, qT, k, vT, k_meta, vT_meta)


def _mix_out_ffn2_kernel(h1_ref, oT_ref, oconv_ref,
                         gattn_ref, woa_ref, woc_ref,
                         g2_ref, win_ref, wout_ref, gfin_ref,
                         out_ref,
                         xn_ref, acc_ref):
    o_t = oT_ref[0].astype(F32)
    ms = jnp.mean(o_t * o_t, axis=0, keepdims=True)
    o_n = (o_t * lax.rsqrt(ms + EPS) * gattn_ref[...]).T.astype(BF16)
    mix = _dot(o_n, woa_ref[...]) + _dot(oconv_ref[0], woc_ref[...])
    h2 = h1_ref[0] + mix
    xn_ref[...] = _rmsnorm_rows(h2, g2_ref[...]).astype(BF16)
    _swiglu(xn_ref, win_ref, wout_ref, acc_ref)
    h3 = h2 + acc_ref[...]
    out_ref[0] = _rmsnorm_rows(h3, gfin_ref[...])


def _mix_out_ffn2(h1, o_t, oconv, p, tm):
    nb, seq, _ = h1.shape
    gattn_t = jnp.broadcast_to(p["gattn"][:, None], (D_ATTN, tm))
    in_specs = [
        pl.BlockSpec((1, tm, D_MODEL), lambda b, t: (b, t, 0)),
        pl.BlockSpec((1, D_ATTN, tm), lambda b, t: (b, 0, t)),
        pl.BlockSpec((1, tm, D_CONV), lambda b, t: (b, t, 0)),
        _const_spec((D_ATTN, tm)), _const_spec((D_ATTN, D_MODEL)), _const_spec((D_CONV, D_MODEL)),
        _const_spec((1, D_MODEL)),
        _const_spec((D_MODEL, 2 * D_FF)), _const_spec((D_FF, D_MODEL)),
        _const_spec((1, D_MODEL)),
    ]
    return pl.pallas_call(
        _mix_out_ffn2_kernel,
        out_shape=jax.ShapeDtypeStruct((nb, seq, D_MODEL), F32),
        grid=(nb, seq // tm),
        in_specs=in_specs,
        out_specs=pl.BlockSpec((1, tm, D_MODEL), lambda b, t: (b, t, 0)),
        scratch_shapes=[pltpu.VMEM((tm, D_MODEL), BF16), pltpu.VMEM((tm, D_MODEL), F32)],
        name="mix_out_ffn2",
        compiler_params=pltpu.CompilerParams(
            dimension_semantics=("arbitrary", "arbitrary"), vmem_limit_bytes=VMEM_LIMIT_BYTES),
    )(h1, o_t, oconv, gattn_t, p["woa"], p["woc"], p["g2"], p["win2"], p["wout2"], p["gfin"])


def kernel(x, meta_tokens, ffn1_norm, ffn1_w_in, ffn1_w_out, mix_norm, w_mix_in, b_forget, q_norm, k_norm, conv_w, attn_out_norm, conv_out_norm, w_mix_out, ffn2_norm, ffn2_w_in, ffn2_w_out, final_norm):
    nb, seq, _ = x.shape
    wmix = w_mix_in[0]
    n_qkv = 3 * D_ATTN
    p = {
        "g1": ffn1_norm, "win1": ffn1_w_in[0].astype(BF16), "wout1": (0.5 * ffn1_w_out[0]).astype(BF16),
        "gmix": mix_norm,
        "wfkT": jnp.concatenate([jnp.pad(wmix[:, n_qkv:n_qkv + HEADS].T, ((0, F_ROWS - HEADS), (0, 0))),
                                 wmix[:, D_ATTN:2 * D_ATTN].T], axis=0).astype(BF16),
        "wqT": wmix[:, :D_ATTN].T.astype(BF16),
        "wvT": wmix[:, 2 * D_ATTN:n_qkv].T.astype(BF16),
        "wcu": wmix[:, n_qkv + HEADS + D_CONV:].astype(BF16),
        "wb": wmix[:, n_qkv + HEADS:n_qkv + HEADS + D_CONV].astype(BF16),
        "bf": jnp.pad(b_forget[0], (0, F_ROWS - HEADS))[:, None],
        "gq": q_norm[0], "gk": k_norm[0],
        "cw": jnp.pad(conv_w[0], ((0, SUBLANES - CONV_WIDTH), (0, 0))),
        "gconv": conv_out_norm,
        "gattn": attn_out_norm[0],
        "woa": w_mix_out[0, :D_ATTN].astype(BF16), "woc": w_mix_out[0, D_ATTN:].astype(BF16),
        "g2": ffn2_norm, "win2": ffn2_w_in[0].astype(BF16), "wout2": (0.5 * ffn2_w_out[0]).astype(BF16),
        "gfin": final_norm,
    }

    n_pad = META_TILE - N_META
    meta = jnp.pad(meta_tokens.astype(x.dtype), ((n_pad, 0), (0, 0)))[None]
    zeros_zc = jnp.zeros((SUBLANES, D_CONV), F32)
    zeros_fc = jnp.zeros((F_ROWS, LANES), F32)
    _, _, k_m, vT_m, _, _, zc_m, fc_m = _ffn1_mix_in(
        meta, jnp.full((1,), n_pad, jnp.int32), zeros_zc, zeros_fc, p, META_TILE)
    k_meta = k_m[0, :, n_pad:, :]
    vT_meta = vT_m[0, :, :, n_pad:]

    h1, qT, k, vT, f_cum, oconv, _, _ = _ffn1_mix_in(
        x, jnp.zeros((1,), jnp.int32), zc_m, fc_m, p, TOKEN_TILE)

    fq0 = f_cum[:, :, 0::Q_BLOCK].reshape(nb * HEADS, seq // Q_BLOCK)
    flast = f_cum[:, :, K_BLOCK - 1::K_BLOCK].reshape(nb * HEADS, seq // K_BLOCK)
    c_bound = (16.0 * 1.02 * jnp.max(jnp.abs(q_norm)) * jnp.max(jnp.abs(k_norm)) + 1.0).reshape(1).astype(F32)

    o_t = _fox_attention(qT, k, vT, k_meta, vT_meta, fq0, flast, c_bound)
    o_t = o_t.reshape(nb, D_ATTN, seq)
    return _mix_out_ffn2(h1, o_t, oconv, p, TOKEN_TILE)
```

```python
import jax
import jax.numpy as jnp
from jax import lax
from jax.experimental import pallas as pl
from jax.experimental.pallas import tpu as pltpu

D_MODEL = 1024
N_META = 16
D_ATTN = 512
D_CONV = 512
HEADS = 8
HEAD_DIM = 64
CONV_WIDTH = 3
D_FF = 2816
EPS = 1e-6

F32 = jnp.float32
BF16 = jnp.bfloat16

LANES = 128
SUBLANES = 8
MXU_DIM = 256
VMEM_LIMIT_BYTES = 60000 * 1024

TOKEN_TILE = 512
META_TILE = LANES
FF_CHUNK = MXU_DIM
N_FF_CHUNKS = D_FF // FF_CHUNK
Q_BLOCK = MXU_DIM
K_BLOCK = MXU_DIM
AUG_DIM = LANES
HEAD_GROUP = 4
Q_PER_STEP = 4
MASKED = -1e30
LOG2E = 1.4426950408889634
F_ROWS = 2 * SUBLANES
MACARON_SCALE = 0.5
WIN_STAGE_ROWS = 64
WOUT_STAGE_ROWS = 256

EXP_UNDERFLOW = -104.0
BOUNDED_LOGIT_MAX = 120.0

assert D_FF % FF_CHUNK == 0 and FF_CHUNK % LANES == 0 and HEADS % HEAD_GROUP == 0


def _dot(a, b):
    return jnp.dot(a, b, preferred_element_type=F32)


def _dot_nt(a, b):
    return lax.dot_general(a, b, (((1,), (1,)), ((), ())), preferred_element_type=F32)


def _rmsnorm_rows(x, gain):
    ms = jnp.mean(x * x, axis=-1, keepdims=True)
    return x * lax.rsqrt(ms + EPS) * gain


def _split3(x):
    hi = x.astype(BF16)
    r1 = x - hi.astype(F32)
    mid = r1.astype(BF16)
    lo = (r1 - mid.astype(F32)).astype(BF16)
    return hi, mid, lo


def _load_weight_as_bf16(w_hbm, w_bf, stage, sem, scale=None):
    rows = stage.shape[1]
    n_chunks = w_hbm.shape[0] // rows
    assert n_chunks * rows == w_hbm.shape[0]

    def chunk_copy(i, slot):
        return pltpu.make_async_copy(w_hbm.at[pl.ds(i * rows, rows)], stage.at[slot], sem.at[slot])

    chunk_copy(0, 0).start()
    for i in range(n_chunks):
        slot = i % 2
        if i + 1 < n_chunks:
            chunk_copy(i + 1, 1 - slot).start()
        chunk_copy(i, slot).wait()
        w = stage[slot]
        w_bf[i * rows:(i + 1) * rows, :] = (w if scale is None else w * scale).astype(BF16)


def _swiglu(xn_ref, win_ref, wout_ref, acc_ref, residual):
    acc_ref[...] = residual
    for c in range(N_FF_CHUNKS):
        lo, hi = c * FF_CHUNK, (c + 1) * FF_CHUNK
        xn = xn_ref[...]
        g = _dot(xn, win_ref[:, lo:hi])
        u = _dot(xn, win_ref[:, D_FF + lo:D_FF + hi])
        a = (g * jax.nn.sigmoid(g) * u).astype(BF16)
        acc_ref[...] += _dot(a, wout_ref[lo:hi, :])


def _ffn1_mix_in_kernel(
        npad_ref,
        x_ref, zc_in_ref, fc_in_ref,
        g1_ref, win_hbm, wout_hbm,
        gmix_ref, wfkT_ref, wqT_ref, wvT_ref, wcu_ref, wb_ref, bf_ref, tri_ref,
        gq_ref, gk_ref, cw_ref, gconv_ref,
        h1_ref, qT_ref, k_ref, vT_ref, f_ref, oconv_ref, zc_out_ref, fc_out_ref,
        xn_ref, acc_ref, zc_ref, fc_ref, win_ref, wout_ref, win_stage, wout_stage, win_sem, wout_sem):
    t = pl.program_id(1)
    tm = x_ref.shape[1]

    @pl.when(jnp.logical_and(pl.program_id(0) == 0, t == 0))
    def _():
        _load_weight_as_bf16(win_hbm, win_ref, win_stage, win_sem)
        _load_weight_as_bf16(wout_hbm, wout_ref, wout_stage, wout_sem, MACARON_SCALE)

    @pl.when(t == 0)
    def _():
        zc_ref[...] = zc_in_ref[...]
        fc_ref[...] = fc_in_ref[...]

    x = x_ref[0]
    xn_ref[...] = _rmsnorm_rows(x, g1_ref[...]).astype(BF16)
    _swiglu(xn_ref, win_ref, wout_ref, acc_ref, x)
    h1 = acc_ref[...]
    h1_ref[0] = h1

    xn2 = _rmsnorm_rows(h1, gmix_ref[...]).astype(BF16)
    fkT = _dot_nt(wfkT_ref[...], xn2)
    cu = _dot(xn2, wcu_ref[...])

    fl = fkT[0:F_ROWS] + bf_ref[...]
    logf = jnp.minimum(fl, 0.0) - jnp.log(1.0 + jnp.exp(-jnp.abs(fl)))
    pos = t * tm + lax.broadcasted_iota(jnp.int32, logf.shape, 1)
    logf = jnp.where(pos >= npad_ref[0], logf, 0.0)
    pieces = _dot(jnp.concatenate(_split3(logf), axis=0), tri_ref[...])
    csum = pieces[0:F_ROWS] + pieces[F_ROWS:2 * F_ROWS] + pieces[2 * F_ROWS:3 * F_ROWS]
    f_all = csum + jnp.concatenate([fc_ref[...]] * (tm // LANES), axis=1)
    fc_new = jnp.broadcast_to(f_all[:, tm - 1:tm], fc_ref.shape)
    fc_ref[...] = fc_new
    fc_out_ref[...] = fc_new
    f_ref[0] = f_all[:HEADS]

    f_hi, f_mid, f_lo = (p.astype(F32) for p in _split3(f_all * LOG2E))
    row = lax.broadcasted_iota(jnp.int32, (SUBLANES, tm), 0)
    ones_mid = jnp.where(row < 6, 1.0, 0.0)
    pad_rows = jnp.zeros((AUG_DIM - HEAD_DIM - SUBLANES, tm), F32)

    def head_pieces(h):
        return (jnp.broadcast_to(f_hi[h:h + 1], (SUBLANES, tm)),
                jnp.broadcast_to(f_mid[h:h + 1], (SUBLANES, tm)),
                jnp.broadcast_to(f_lo[h:h + 1], (SUBLANES, tm)))

    def head_rmsnorm(x_t, gain_t):
        x3 = x_t.reshape(HEADS, HEAD_DIM, tm)
        return x3 * lax.rsqrt(jnp.mean(x3 * x3, axis=1, keepdims=True) + EPS) * gain_t[None]

    kn = head_rmsnorm(fkT[F_ROWS:F_ROWS + D_ATTN], gk_ref[...])
    qT = _dot_nt(wqT_ref[...], xn2)
    gate_b = _dot(xn2, wb_ref[...])
    for h in range(HEADS):
        fh, fm, fo = head_pieces(h)
        aug_k = jnp.where(row == 3, -fh, jnp.where(row == 4, -fm, jnp.where(row == 5, -fo, ones_mid)))
        k_aug_t = jnp.concatenate([kn[h], aug_k, pad_rows], axis=0)
        k_ref[0, h] = k_aug_t.T.astype(BF16)

    vT = _dot_nt(wvT_ref[...], xn2)
    z = cu[:, 0:D_CONV] * cu[:, D_CONV:2 * D_CONV]
    zc = zc_ref[...]
    rowz = lax.broadcasted_iota(jnp.int32, z.shape, 0)
    prev1 = jnp.broadcast_to(zc[7:8], z.shape)
    prev2 = jnp.broadcast_to(zc[6:7], z.shape)
    z1 = jnp.where(rowz == 0, prev1, pltpu.roll(z, 1, axis=0))
    z2 = jnp.where(rowz == 0, prev2, jnp.where(rowz == 1, prev1, pltpu.roll(z, 2, axis=0)))
    cw = cw_ref[...]
    y = cw[0:1] * z2 + cw[1:2] * z1 + cw[2:3] * z
    oconv_ref[0] = _rmsnorm_rows(gate_b * y, gconv_ref[...]).astype(BF16)
    zc_new = z[tm - SUBLANES:tm]
    zc_ref[...] = zc_new
    zc_out_ref[...] = zc_new

    qn = head_rmsnorm(qT, gq_ref[...] * (HEAD_DIM ** -0.5 * LOG2E))
    v3 = vT.reshape(HEADS, HEAD_DIM, tm)
    for h in range(HEADS):
        fh, fm, fo = head_pieces(h)
        aug_q = jnp.where(row == 0, fh, jnp.where(row == 1, fm, jnp.where(row == 2, fo, ones_mid)))
        qT_ref[0, h] = jnp.concatenate([qn[h], aug_q, pad_rows], axis=0).astype(BF16)
        vT_ref[0, h] = v3[h].astype(BF16)


def _ffn_weight_scratch():
    return [pltpu.VMEM((D_MODEL, 2 * D_FF), BF16), pltpu.VMEM((D_FF, D_MODEL), BF16),
            pltpu.VMEM((2, WIN_STAGE_ROWS, 2 * D_FF), F32), pltpu.VMEM((2, WOUT_STAGE_ROWS, D_MODEL), F32),
            pltpu.SemaphoreType.DMA((2,)), pltpu.SemaphoreType.DMA((2,))]


def _const_spec(shape):
    nd = len(shape)
    return pl.BlockSpec(shape, lambda *_: (0,) * nd, pipeline_mode=pl.Buffered(1))


def _ffn1_mix_in(x, n_pad, zc_in, fc_in, p, tm):
    nb, seq, _ = x.shape
    nt = seq // tm
    tri = jnp.triu(jnp.ones((tm, tm), BF16))
    gq_t = jnp.broadcast_to(p["gq"][:, None], (HEAD_DIM, tm))
    gk_t = jnp.broadcast_to(p["gk"][:, None], (HEAD_DIM, tm))

    def tile3(last):
        return pl.BlockSpec((1, tm, last), lambda b, t, *_: (b, t, 0))

    in_specs = [
        tile3(D_MODEL),
        _const_spec((SUBLANES, D_CONV)), _const_spec((F_ROWS, LANES)),
        _const_spec((1, D_MODEL)),
        pl.BlockSpec(memory_space=pl.ANY), pl.BlockSpec(memory_space=pl.ANY),
        _const_spec((1, D_MODEL)), _const_spec((F_ROWS + D_ATTN, D_MODEL)),
        _const_spec((D_ATTN, D_MODEL)), _const_spec((D_ATTN, D_MODEL)),
        _const_spec((D_MODEL, 2 * D_CONV)), _const_spec((D_MODEL, D_CONV)),
        _const_spec((F_ROWS, 1)), _const_spec((tm, tm)),
        _const_spec((HEAD_DIM, tm)), _const_spec((HEAD_DIM, tm)),
        _const_spec((SUBLANES, D_CONV)), _const_spec((1, D_CONV)),
    ]
    out_shape = [
        jax.ShapeDtypeStruct((nb, seq, D_MODEL), F32),
        jax.ShapeDtypeStruct((nb, HEADS, AUG_DIM, seq), BF16),
        jax.ShapeDtypeStruct((nb, HEADS, seq, AUG_DIM), BF16),
        jax.ShapeDtypeStruct((nb, HEADS, HEAD_DIM, seq), BF16),
        jax.ShapeDtypeStruct((nb, HEADS, seq), F32),
        jax.ShapeDtypeStruct((nb, seq, D_CONV), BF16),
        jax.ShapeDtypeStruct((SUBLANES, D_CONV), F32),
        jax.ShapeDtypeStruct((F_ROWS, LANES), F32),
    ]
    out_specs = [
        tile3(D_MODEL),
        pl.BlockSpec((1, HEADS, AUG_DIM, tm), lambda b, t, *_: (b, 0, 0, t)),
        pl.BlockSpec((1, HEADS, tm, AUG_DIM), lambda b, t, *_: (b, 0, t, 0)),
        pl.BlockSpec((1, HEADS, HEAD_DIM, tm), lambda b, t, *_: (b, 0, 0, t)),
        pl.BlockSpec((1, HEADS, tm), lambda b, t, *_: (b, 0, t)),
        tile3(D_CONV),
        pl.BlockSpec((SUBLANES, D_CONV), lambda b, t, *_: (0, 0)),
        pl.BlockSpec((F_ROWS, LANES), lambda b, t, *_: (0, 0)),
    ]
    grid_spec = pltpu.PrefetchScalarGridSpec(
        num_scalar_prefetch=1, grid=(nb, nt), in_specs=in_specs, out_specs=out_specs,
        scratch_shapes=[
            pltpu.VMEM((tm, D_MODEL), BF16),
            pltpu.VMEM((tm, D_MODEL), F32),
            pltpu.VMEM((SUBLANES, D_CONV), F32),
            pltpu.VMEM((F_ROWS, LANES), F32),
        ] + _ffn_weight_scratch())
    return pl.pallas_call(
        _ffn1_mix_in_kernel, out_shape=out_shape, grid_spec=grid_spec, name="ffn1_mix_in",
        compiler_params=pltpu.CompilerParams(
            dimension_semantics=("arbitrary", "arbitrary"), vmem_limit_bytes=VMEM_LIMIT_BYTES),
    )(n_pad, x, zc_in, fc_in,
      p["g1"], p["win1"], p["wout1"],
      p["gmix"], p["wfkT"], p["wqT"], p["wvT"], p["wcu"], p["wb"], p["bf"], tri,
      gq_t, gk_t, p["cw"], p["gconv"])


def _fox_attention_kernel(fq0_ref, flast_ref, cb_ref,
                          qT_ref, k_ref, vT_ref, km_ref, vmT_ref,
                          o_ref,
                          acc_ref, l_ref):
    bh0 = pl.program_id(0) * HEADS + pl.program_id(1) * HEAD_GROUP
    heads = range(HEAD_GROUP)
    nq = qT_ref.shape[3] // Q_BLOCK
    c_bound = cb_ref[0]
    key_idx = lax.broadcasted_iota(jnp.int32, (K_BLOCK, Q_BLOCK), 0)
    qry_idx = lax.broadcasted_iota(jnp.int32, (K_BLOCK, Q_BLOCK), 1)
    causal = key_idx <= qry_idx

    def block_live(g, i, j):
        return fq0_ref[bh0 + g, i] - flast_ref[bh0 + g, jnp.maximum(j, 0)] + c_bound >= EXP_UNDERFLOW

    def sublane_partial_sum(p):
        return jnp.sum(p.reshape(p.shape[0] // SUBLANES, SUBLANES, p.shape[1]), axis=0)

    def bounded_tile(k_blk, v_t, q_t):
        p = jnp.exp2(_dot(k_blk, q_t))
        return sublane_partial_sum(p), _dot(v_t, p.astype(BF16))

    def bounded_q_blocks(i_first):
        chains = [(r, g) for r in range(Q_PER_STEP) for g in heads]
        blk = [i_first + r for r in range(Q_PER_STEP)]
        q0 = [pl.multiple_of(i * Q_BLOCK, Q_BLOCK) for i in blk]
        p0 = [pl.multiple_of(jnp.maximum(i - 1, 0) * K_BLOCK, K_BLOCK) for i in blk]
        first_mask = jnp.where(i_first >= 1, 0.0, MASKED)
        q_t = [qT_ref[0, g, :, pl.ds(q0[r], Q_BLOCK)] for r, g in chains]
        s_m = [_dot(km_ref[g], q_t[c]) for c, (r, g) in enumerate(chains)]
        s_d = [_dot(k_ref[0, g, pl.ds(q0[r], K_BLOCK), :], q_t[c]) for c, (r, g) in enumerate(chains)]
        s_p = [_dot(k_ref[0, g, pl.ds(p0[r], K_BLOCK), :], q_t[c]) for c, (r, g) in enumerate(chains)]
        p_m = [jnp.exp2(s) for s in s_m]
        p_d = [jnp.exp2(jnp.where(causal, s, MASKED)) for s in s_d]
        p_p = [jnp.exp2(s + first_mask if r == 0 else s) for s, (r, g) in zip(s_p, chains)]
        for c, (r, g) in enumerate(chains):
            l_ref[c] = sublane_partial_sum(p_m[c]) + sublane_partial_sum(p_d[c]) + sublane_partial_sum(p_p[c])
            acc_ref[c] = (_dot(vmT_ref[g], p_m[c].astype(BF16))
                          + _dot(vT_ref[0, g, :, pl.ds(q0[r], K_BLOCK)], p_d[c].astype(BF16))
                          + _dot(vT_ref[0, g, :, pl.ds(p0[r], K_BLOCK)], p_p[c].astype(BF16)))

        for r in range(Q_PER_STEP):
            i = blk[r]

            def cond(j, i=i):
                live = block_live(0, i, j)
                for g in heads[1:]:
                    live = jnp.logical_or(live, block_live(g, i, j))
                return jnp.logical_and(j >= 0, live)

            def body(j, r=r):
                k0 = pl.multiple_of(j * K_BLOCK, K_BLOCK)
                for g in heads:
                    c = r * HEAD_GROUP + g
                    l_j, acc_j = bounded_tile(k_ref[0, g, pl.ds(k0, K_BLOCK), :],
                                              vT_ref[0, g, :, pl.ds(k0, K_BLOCK)],
                                              qT_ref[0, g, :, pl.ds(q0[r], Q_BLOCK)])
                    l_ref[c] += l_j
                    acc_ref[c] += acc_j
                return j - 1

            lax.while_loop(cond, body, i - 2)
        for c, (r, g) in enumerate(chains):
            l_tot = jnp.sum(l_ref[c], axis=0, keepdims=True)
            o_ref[0, g, :, pl.ds(q0[r], Q_BLOCK)] = (acc_ref[c] / l_tot).astype(o_ref.dtype)

    @pl.when(c_bound <= BOUNDED_LOGIT_MAX)
    def _():
        def step(n, carry):
            bounded_q_blocks(n * Q_PER_STEP)
            return carry

        lax.fori_loop(0, nq // Q_PER_STEP, step, 0)

    def online_q_block(g, i):
        q0 = pl.multiple_of(i * Q_BLOCK, Q_BLOCK)
        q_t = qT_ref[0, g, :, pl.ds(q0, Q_BLOCK)]
        s_m = _dot(km_ref[g], q_t)
        m = jnp.max(s_m, axis=0, keepdims=True)
        p_m = jnp.exp2(s_m - m)
        l = jnp.sum(p_m, axis=0, keepdims=True)
        acc = _dot(vmT_ref[g], p_m.astype(BF16))

        def online_step(state, k0, masked):
            m, l, acc = state
            s = _dot(k_ref[0, g, pl.ds(k0, K_BLOCK), :], q_t)
            if masked:
                s = jnp.where(causal, s, MASKED)
            m_new = jnp.maximum(m, jnp.max(s, axis=0, keepdims=True))
            alpha = jnp.exp2(m - m_new)
            p = jnp.exp2(s - m_new)
            l = alpha * l + jnp.sum(p, axis=0, keepdims=True)
            acc = alpha * acc + _dot(vT_ref[0, g, :, pl.ds(k0, K_BLOCK)], p.astype(BF16))
            return m_new, l, acc

        state = online_step((m, l, acc), q0, True)

        def cond(carry):
            return jnp.logical_and(carry[0] >= 0, block_live(g, i, carry[0]))

        def body(carry):
            j = carry[0]
            return (j - 1,) + online_step(carry[1:], pl.multiple_of(j * K_BLOCK, K_BLOCK), False)

        _, m, l, acc = lax.while_loop(cond, body, (i - 1,) + state)
        o_ref[0, g, :, pl.ds(q0, Q_BLOCK)] = (acc / l).astype(o_ref.dtype)

    @pl.when(c_bound > BOUNDED_LOGIT_MAX)
    def _():
        def head_loop(g, carry):
            def step(i, inner):
                online_q_block(g, i)
                return inner

            lax.fori_loop(0, nq, step, 0)
            return carry

        lax.fori_loop(0, HEAD_GROUP, head_loop, 0)


def _fox_attention(qT, k, vT, k_meta, vT_meta, fq0, flast, c_bound):
    nb, _, _, seq = qT.shape
    assert seq % (Q_PER_STEP * Q_BLOCK) == 0 and Q_BLOCK == K_BLOCK
    smem = pl.BlockSpec(memory_space=pltpu.SMEM)
    hg = HEAD_GROUP
    return pl.pallas_call(
        _fox_attention_kernel,
        out_shape=jax.ShapeDtypeStruct((nb, HEADS, HEAD_DIM, seq), BF16),
        grid=(nb, HEADS // hg),
        in_specs=[
            smem, smem, smem,
            pl.BlockSpec((1, hg, AUG_DIM, seq), lambda b, h: (b, h, 0, 0)),
            pl.BlockSpec((1, hg, seq, AUG_DIM), lambda b, h: (b, h, 0, 0)),
            pl.BlockSpec((1, hg, HEAD_DIM, seq), lambda b, h: (b, h, 0, 0)),
            pl.BlockSpec((hg, N_META, AUG_DIM), lambda b, h: (h, 0, 0)),
            pl.BlockSpec((hg, HEAD_DIM, N_META), lambda b, h: (h, 0, 0)),
        ],
        out_specs=pl.BlockSpec((1, hg, HEAD_DIM, seq), lambda b, h: (b, h, 0, 0)),
        scratch_shapes=[pltpu.VMEM((Q_PER_STEP * hg, HEAD_DIM, Q_BLOCK), F32),
                        pltpu.VMEM((Q_PER_STEP * hg, SUBLANES, Q_BLOCK), F32)],
        name="fox_attention",
        compiler_params=pltpu.CompilerParams(
            dimension_semantics=("arbitrary", "arbitrary"), vmem_limit_bytes=VMEM_LIMIT_BYTES),
    )(fq0, flast, c_bound, qT, k, vT, k_meta, vT_meta)


def _mix_out_ffn2_kernel(h1_ref, oT_ref, oconv_ref,
                         gattn_ref, woa_ref, woc_ref,
                         g2_ref, win_hbm, wout_hbm, gfin_ref,
                         out_ref,
                         xn_ref, acc_ref, win_ref, wout_ref, win_stage, wout_stage, win_sem, wout_sem):
    @pl.when(jnp.logical_and(pl.program_id(0) == 0, pl.program_id(1) == 0))
    def _():
        _load_weight_as_bf16(win_hbm, win_ref, win_stage, win_sem)
        _load_weight_as_bf16(wout_hbm, wout_ref, wout_stage, wout_sem, MACARON_SCALE)

    o_t = oT_ref[0].astype(F32)
    ms = jnp.mean(o_t * o_t, axis=0, keepdims=True)
    o_n = (o_t * lax.rsqrt(ms + EPS) * gattn_ref[...]).T.astype(BF16)
    mix = _dot(o_n, woa_ref[...]) + _dot(oconv_ref[0], woc_ref[...])
    h2 = h1_ref[0] + mix
    xn_ref[...] = _rmsnorm_rows(h2, g2_ref[...]).astype(BF16)
    _swiglu(xn_ref, win_ref, wout_ref, acc_ref, h2)
    out_ref[0] = _rmsnorm_rows(acc_ref[...], gfin_ref[...])


def _mix_out_ffn2(h1, o_t, oconv, p, tm):
    nb, seq, _ = h1.shape
    gattn_t = jnp.broadcast_to(p["gattn"][:, None], (D_ATTN, tm))
    in_specs = [
        pl.BlockSpec((1, tm, D_MODEL), lambda b, t: (b, t, 0)),
        pl.BlockSpec((1, D_ATTN, tm), lambda b, t: (b, 0, t)),
        pl.BlockSpec((1, tm, D_CONV), lambda b, t: (b, t, 0)),
        _const_spec((D_ATTN, tm)), _const_spec((D_ATTN, D_MODEL)), _const_spec((D_CONV, D_MODEL)),
        _const_spec((1, D_MODEL)),
        pl.BlockSpec(memory_space=pl.ANY), pl.BlockSpec(memory_space=pl.ANY),
        _const_spec((1, D_MODEL)),
    ]
    return pl.pallas_call(
        _mix_out_ffn2_kernel,
        out_shape=jax.ShapeDtypeStruct((nb, seq, D_MODEL), F32),
        grid=(nb, seq // tm),
        in_specs=in_specs,
        out_specs=pl.BlockSpec((1, tm, D_MODEL), lambda b, t: (b, t, 0)),
        scratch_shapes=[pltpu.VMEM((tm, D_MODEL), BF16), pltpu.VMEM((tm, D_MODEL), F32)] + _ffn_weight_scratch(),
        name="mix_out_ffn2",
        compiler_params=pltpu.CompilerParams(
            dimension_semantics=("arbitrary", "arbitrary"), vmem_limit_bytes=VMEM_LIMIT_BYTES),
    )(h1, o_t, oconv, gattn_t, p["woa"], p["woc"], p["g2"], p["win2"], p["wout2"], p["gfin"])


def kernel(x, meta_tokens, ffn1_norm, ffn1_w_in, ffn1_w_out, mix_norm, w_mix_in, b_forget, q_norm, k_norm, conv_w, attn_out_norm, conv_out_norm, w_mix_out, ffn2_norm, ffn2_w_in, ffn2_w_out, final_norm):
    nb, seq, _ = x.shape
    wmix = w_mix_in[0]
    n_qkv = 3 * D_ATTN
    p = {
        "g1": ffn1_norm, "win1": ffn1_w_in[0], "wout1": ffn1_w_out[0],
        "gmix": mix_norm,
        "wfkT": jnp.concatenate([jnp.pad(wmix[:, n_qkv:n_qkv + HEADS].T, ((0, F_ROWS - HEADS), (0, 0))),
                                 wmix[:, D_ATTN:2 * D_ATTN].T], axis=0).astype(BF16),
        "wqT": wmix[:, :D_ATTN].T.astype(BF16),
        "wvT": wmix[:, 2 * D_ATTN:n_qkv].T.astype(BF16),
        "wcu": wmix[:, n_qkv + HEADS + D_CONV:].astype(BF16),
        "wb": wmix[:, n_qkv + HEADS:n_qkv + HEADS + D_CONV].astype(BF16),
        "bf": jnp.pad(b_forget[0], (0, F_ROWS - HEADS))[:, None],
        "gq": q_norm[0], "gk": k_norm[0],
        "cw": jnp.pad(conv_w[0], ((0, SUBLANES - CONV_WIDTH), (0, 0))),
        "gconv": conv_out_norm,
        "gattn": attn_out_norm[0],
        "woa": w_mix_out[0, :D_ATTN].astype(BF16), "woc": w_mix_out[0, D_ATTN:].astype(BF16),
        "g2": ffn2_norm, "win2": ffn2_w_in[0], "wout2": ffn2_w_out[0],
        "gfin": final_norm,
    }

    n_pad = META_TILE - N_META
    meta = jnp.pad(meta_tokens.astype(x.dtype), ((n_pad, 0), (0, 0)))[None]
    zeros_zc = jnp.zeros((SUBLANES, D_CONV), F32)
    zeros_fc = jnp.zeros((F_ROWS, LANES), F32)
    _, _, k_m, vT_m, _, _, zc_m, fc_m = _ffn1_mix_in(
        meta, jnp.full((1,), n_pad, jnp.int32), zeros_zc, zeros_fc, p, META_TILE)
    k_meta = k_m[0, :, n_pad:, :]
    vT_meta = vT_m[0, :, :, n_pad:]

    h1, qT, k, vT, f_cum, oconv, _, _ = _ffn1_mix_in(
        x, jnp.zeros((1,), jnp.int32), zc_m, fc_m, p, TOKEN_TILE)

    fq0 = f_cum[:, :, 0::Q_BLOCK].reshape(nb * HEADS, seq // Q_BLOCK)
    flast = f_cum[:, :, K_BLOCK - 1::K_BLOCK].reshape(nb * HEADS, seq // K_BLOCK)
    c_bound = (16.0 * 1.02 * jnp.max(jnp.abs(q_norm)) * jnp.max(jnp.abs(k_norm)) + 1.0).reshape(1).astype(F32)

    o_t = _fox_attention(qT, k, vT, k_meta, vT_meta, fq0, flast, c_bound)
    o_t = o_t.reshape(nb, D_ATTN, seq)
    return _mix_out_ffn2(h1, o_t, oconv, p, TOKEN_TILE)
```

```python
import jax
import jax.numpy as jnp
from jax import lax
from jax.experimental import pallas as pl
from jax.experimental.pallas import tpu as pltpu

D_MODEL = 1024
N_META = 16
D_ATTN = 512
D_CONV = 512
HEADS = 8
HEAD_DIM = 64
CONV_WIDTH = 3
D_FF = 2816
EPS = 1e-6

F32 = jnp.float32
BF16 = jnp.bfloat16

LANES = 128
SUBLANES = 8
MXU_DIM = 256
VMEM_LIMIT_BYTES = 60000 * 1024

TOKEN_TILE = 512
META_TILE = LANES
FF_CHUNK = MXU_DIM
N_FF_CHUNKS = D_FF // FF_CHUNK
Q_BLOCK = MXU_DIM
K_BLOCK = MXU_DIM
AUG_DIM = LANES
HEAD_GROUP = 4
Q_PER_STEP = 4
MASKED = -1e30
LOG2E = 1.4426950408889634
F_ROWS = 2 * SUBLANES
MACARON_SCALE = 0.5
WIN_STAGE_ROWS = 64
WOUT_STAGE_ROWS = 256
STAGE_SLOTS = 4

EXP_UNDERFLOW = -104.0
BOUNDED_LOGIT_MAX = 120.0

assert D_FF % FF_CHUNK == 0 and FF_CHUNK % LANES == 0 and HEADS % HEAD_GROUP == 0


def _dot(a, b):
    return jnp.dot(a, b, preferred_element_type=F32)


def _dot_nt(a, b):
    return lax.dot_general(a, b, (((1,), (1,)), ((), ())), preferred_element_type=F32)


def _rmsnorm_rows(x, gain):
    ms = jnp.mean(x * x, axis=-1, keepdims=True)
    return x * lax.rsqrt(ms + EPS) * gain


def _split3(x):
    hi = x.astype(BF16)
    r1 = x - hi.astype(F32)
    mid = r1.astype(BF16)
    lo = (r1 - mid.astype(F32)).astype(BF16)
    return hi, mid, lo


def _load_weight_as_bf16(w_hbm, w_bf, stage, sem, scale=None):
    n_slots, rows = stage.shape[0], stage.shape[1]
    n_chunks = w_hbm.shape[0] // rows
    assert n_chunks * rows == w_hbm.shape[0]

    def chunk_copy(i):
        slot = i % n_slots
        return pltpu.make_async_copy(w_hbm.at[pl.ds(i * rows, rows)], stage.at[slot], sem.at[slot])

    for i in range(min(n_slots - 1, n_chunks)):
        chunk_copy(i).start()
    for i in range(n_chunks):
        if i + n_slots - 1 < n_chunks:
            chunk_copy(i + n_slots - 1).start()
        chunk_copy(i).wait()
        w = stage[i % n_slots]
        w_bf[i * rows:(i + 1) * rows, :] = (w if scale is None else w * scale).astype(BF16)


def _swiglu(xn_ref, win_ref, wout_ref, acc_ref, residual):
    acc_ref[...] = residual
    for c in range(N_FF_CHUNKS):
        lo, hi = c * FF_CHUNK, (c + 1) * FF_CHUNK
        xn = xn_ref[...]
        g = _dot(xn, win_ref[:, lo:hi])
        u = _dot(xn, win_ref[:, D_FF + lo:D_FF + hi])
        a = (g * jax.nn.sigmoid(g) * u).astype(BF16)
        acc_ref[...] += _dot(a, wout_ref[lo:hi, :])


def _ffn1_mix_in_kernel(
        npad_ref,
        x_ref, zc_in_ref, fc_in_ref,
        g1_ref, win_hbm, wout_hbm,
        gmix_ref, wfkT_ref, wqT_ref, wvT_ref, wcu_ref, wb_ref, bf_ref, tri_ref,
        gq_ref, gk_ref, cw_ref, gconv_ref,
        h1_ref, qT_ref, k_ref, vT_ref, f_ref, oconv_ref, zc_out_ref, fc_out_ref,
        xn_ref, acc_ref, zc_ref, fc_ref, win_ref, wout_ref, win_stage, wout_stage, win_sem, wout_sem):
    t = pl.program_id(1)
    tm = x_ref.shape[1]

    @pl.when(jnp.logical_and(pl.program_id(0) == 0, t == 0))
    def _():
        _load_weight_as_bf16(win_hbm, win_ref, win_stage, win_sem)
        _load_weight_as_bf16(wout_hbm, wout_ref, wout_stage, wout_sem, MACARON_SCALE)

    @pl.when(t == 0)
    def _():
        zc_ref[...] = zc_in_ref[...]
        fc_ref[...] = fc_in_ref[...]

    x = x_ref[0]
    xn_ref[...] = _rmsnorm_rows(x, g1_ref[...]).astype(BF16)
    _swiglu(xn_ref, win_ref, wout_ref, acc_ref, x)
    h1 = acc_ref[...]
    h1_ref[0] = h1

    xn2 = _rmsnorm_rows(h1, gmix_ref[...]).astype(BF16)
    fkT = _dot_nt(wfkT_ref[...], xn2)
    cu = _dot(xn2, wcu_ref[...])

    fl = fkT[0:F_ROWS] + bf_ref[...]
    logf = jnp.minimum(fl, 0.0) - jnp.log(1.0 + jnp.exp(-jnp.abs(fl)))
    pos = t * tm + lax.broadcasted_iota(jnp.int32, logf.shape, 1)
    logf = jnp.where(pos >= npad_ref[0], logf, 0.0)
    pieces = _dot(jnp.concatenate(_split3(logf), axis=0), tri_ref[...])
    csum = pieces[0:F_ROWS] + pieces[F_ROWS:2 * F_ROWS] + pieces[2 * F_ROWS:3 * F_ROWS]
    f_all = csum + jnp.concatenate([fc_ref[...]] * (tm // LANES), axis=1)
    fc_new = jnp.broadcast_to(f_all[:, tm - 1:tm], fc_ref.shape)
    fc_ref[...] = fc_new
    fc_out_ref[...] = fc_new
    f_ref[0] = f_all[:HEADS]

    f_hi, f_mid, f_lo = (p.astype(F32) for p in _split3(f_all * LOG2E))
    row = lax.broadcasted_iota(jnp.int32, (SUBLANES, tm), 0)
    ones_mid = jnp.where(row < 6, 1.0, 0.0)
    pad_rows = jnp.zeros((AUG_DIM - HEAD_DIM - SUBLANES, tm), F32)

    def head_pieces(h):
        return (jnp.broadcast_to(f_hi[h:h + 1], (SUBLANES, tm)),
                jnp.broadcast_to(f_mid[h:h + 1], (SUBLANES, tm)),
                jnp.broadcast_to(f_lo[h:h + 1], (SUBLANES, tm)))

    def head_rmsnorm(x_t, gain_t):
        x3 = x_t.reshape(HEADS, HEAD_DIM, tm)
        return x3 * lax.rsqrt(jnp.mean(x3 * x3, axis=1, keepdims=True) + EPS) * gain_t[None]

    kn = head_rmsnorm(fkT[F_ROWS:F_ROWS + D_ATTN], gk_ref[...])
    qT = _dot_nt(wqT_ref[...], xn2)
    gate_b = _dot(xn2, wb_ref[...])
    for h in range(HEADS):
        fh, fm, fo = head_pieces(h)
        aug_k = jnp.where(row == 3, -fh, jnp.where(row == 4, -fm, jnp.where(row == 5, -fo, ones_mid)))
        k_aug_t = jnp.concatenate([kn[h], aug_k, pad_rows], axis=0)
        k_ref[0, h] = k_aug_t.T.astype(BF16)

    vT = _dot_nt(wvT_ref[...], xn2)
    z = cu[:, 0:D_CONV] * cu[:, D_CONV:2 * D_CONV]
    zc = zc_ref[...]
    rowz = lax.broadcasted_iota(jnp.int32, z.shape, 0)
    prev1 = jnp.broadcast_to(zc[7:8], z.shape)
    prev2 = jnp.broadcast_to(zc[6:7], z.shape)
    z1 = jnp.where(rowz == 0, prev1, pltpu.roll(z, 1, axis=0))
    z2 = jnp.where(rowz == 0, prev2, jnp.where(rowz == 1, prev1, pltpu.roll(z, 2, axis=0)))
    cw = cw_ref[...]
    y = cw[0:1] * z2 + cw[1:2] * z1 + cw[2:3] * z
    oconv_ref[0] = _rmsnorm_rows(gate_b * y, gconv_ref[...]).astype(BF16)
    zc_new = z[tm - SUBLANES:tm]
    zc_ref[...] = zc_new
    zc_out_ref[...] = zc_new

    qn = head_rmsnorm(qT, gq_ref[...] * (HEAD_DIM ** -0.5 * LOG2E))
    v3 = vT.reshape(HEADS, HEAD_DIM, tm)
    for h in range(HEADS):
        fh, fm, fo = head_pieces(h)
        aug_q = jnp.where(row == 0, fh, jnp.where(row == 1, fm, jnp.where(row == 2, fo, ones_mid)))
        qT_ref[0, h] = jnp.concatenate([qn[h], aug_q, pad_rows], axis=0).astype(BF16)
        vT_ref[0, h] = v3[h].astype(BF16)


def _ffn_weight_scratch():
    return [pltpu.VMEM((D_MODEL, 2 * D_FF), BF16), pltpu.VMEM((D_FF, D_MODEL), BF16),
            pltpu.VMEM((STAGE_SLOTS, WIN_STAGE_ROWS, 2 * D_FF), F32),
            pltpu.VMEM((STAGE_SLOTS, WOUT_STAGE_ROWS, D_MODEL), F32),
            pltpu.SemaphoreType.DMA((STAGE_SLOTS,)), pltpu.SemaphoreType.DMA((STAGE_SLOTS,))]


def _const_spec(shape):
    nd = len(shape)
    return pl.BlockSpec(shape, lambda *_: (0,) * nd, pipeline_mode=pl.Buffered(1))


def _ffn1_mix_in(x, n_pad, zc_in, fc_in, p, tm):
    nb, seq, _ = x.shape
    nt = seq // tm
    tri = jnp.triu(jnp.ones((tm, tm), BF16))
    gq_t = jnp.broadcast_to(p["gq"][:, None], (HEAD_DIM, tm))
    gk_t = jnp.broadcast_to(p["gk"][:, None], (HEAD_DIM, tm))

    def tile3(last):
        return pl.BlockSpec((1, tm, last), lambda b, t, *_: (b, t, 0))

    in_specs = [
        tile3(D_MODEL),
        _const_spec((SUBLANES, D_CONV)), _const_spec((F_ROWS, LANES)),
        _const_spec((1, D_MODEL)),
        pl.BlockSpec(memory_space=pl.ANY), pl.BlockSpec(memory_space=pl.ANY),
        _const_spec((1, D_MODEL)), _const_spec((F_ROWS + D_ATTN, D_MODEL)),
        _const_spec((D_ATTN, D_MODEL)), _const_spec((D_ATTN, D_MODEL)),
        _const_spec((D_MODEL, 2 * D_CONV)), _const_spec((D_MODEL, D_CONV)),
        _const_spec((F_ROWS, 1)), _const_spec((tm, tm)),
        _const_spec((HEAD_DIM, tm)), _const_spec((HEAD_DIM, tm)),
        _const_spec((SUBLANES, D_CONV)), _const_spec((1, D_CONV)),
    ]
    out_shape = [
        jax.ShapeDtypeStruct((nb, seq, D_MODEL), F32),
        jax.ShapeDtypeStruct((nb, HEADS, AUG_DIM, seq), BF16),
        jax.ShapeDtypeStruct((nb, HEADS, seq, AUG_DIM), BF16),
        jax.ShapeDtypeStruct((nb, HEADS, HEAD_DIM, seq), BF16),
        jax.ShapeDtypeStruct((nb, HEADS, seq), F32),
        jax.ShapeDtypeStruct((nb, seq, D_CONV), BF16),
        jax.ShapeDtypeStruct((SUBLANES, D_CONV), F32),
        jax.ShapeDtypeStruct((F_ROWS, LANES), F32),
    ]
    out_specs = [
        tile3(D_MODEL),
        pl.BlockSpec((1, HEADS, AUG_DIM, tm), lambda b, t, *_: (b, 0, 0, t)),
        pl.BlockSpec((1, HEADS, tm, AUG_DIM), lambda b, t, *_: (b, 0, t, 0)),
        pl.BlockSpec((1, HEADS, HEAD_DIM, tm), lambda b, t, *_: (b, 0, 0, t)),
        pl.BlockSpec((1, HEADS, tm), lambda b, t, *_: (b, 0, t)),
        tile3(D_CONV),
        pl.BlockSpec((SUBLANES, D_CONV), lambda b, t, *_: (0, 0)),
        pl.BlockSpec((F_ROWS, LANES), lambda b, t, *_: (0, 0)),
    ]
    grid_spec = pltpu.PrefetchScalarGridSpec(
        num_scalar_prefetch=1, grid=(nb, nt), in_specs=in_specs, out_specs=out_specs,
        scratch_shapes=[
            pltpu.VMEM((tm, D_MODEL), BF16),
            pltpu.VMEM((tm, D_MODEL), F32),
            pltpu.VMEM((SUBLANES, D_CONV), F32),
            pltpu.VMEM((F_ROWS, LANES), F32),
        ] + _ffn_weight_scratch())
    return pl.pallas_call(
        _ffn1_mix_in_kernel, out_shape=out_shape, grid_spec=grid_spec, name="ffn1_mix_in",
        compiler_params=pltpu.CompilerParams(
            dimension_semantics=("arbitrary", "arbitrary"), vmem_limit_bytes=VMEM_LIMIT_BYTES),
    )(n_pad, x, zc_in, fc_in,
      p["g1"], p["win1"], p["wout1"],
      p["gmix"], p["wfkT"], p["wqT"], p["wvT"], p["wcu"], p["wb"], p["bf"], tri,
      gq_t, gk_t, p["cw"], p["gconv"])


def _fox_attention_kernel(fq0_ref, flast_ref, cb_ref,
                          qT_ref, k_ref, vT_ref, km_ref, vmT_ref,
                          o_ref,
                          acc_ref, l_ref):
    bh0 = pl.program_id(0) * HEADS + pl.program_id(1) * HEAD_GROUP
    heads = range(HEAD_GROUP)
    nq = qT_ref.shape[3] // Q_BLOCK
    c_bound = cb_ref[0]
    key_idx = lax.broadcasted_iota(jnp.int32, (K_BLOCK, Q_BLOCK), 0)
    qry_idx = lax.broadcasted_iota(jnp.int32, (K_BLOCK, Q_BLOCK), 1)
    causal = key_idx <= qry_idx

    def block_live(g, i, j):
        return fq0_ref[bh0 + g, i] - flast_ref[bh0 + g, jnp.maximum(j, 0)] + c_bound >= EXP_UNDERFLOW

    def sublane_partial_sum(p):
        return jnp.sum(p.reshape(p.shape[0] // SUBLANES, SUBLANES, p.shape[1]), axis=0)

    def bounded_tile(k_blk, v_t, q_t):
        p = jnp.exp2(_dot(k_blk, q_t))
        return sublane_partial_sum(p), _dot(v_t, p.astype(BF16))

    def bounded_q_blocks(i_first):
        chains = [(r, g) for r in range(Q_PER_STEP) for g in heads]
        blk = [i_first + r for r in range(Q_PER_STEP)]
        q0 = [pl.multiple_of(i * Q_BLOCK, Q_BLOCK) for i in blk]
        p0 = [pl.multiple_of(jnp.maximum(i - 1, 0) * K_BLOCK, K_BLOCK) for i in blk]
        first_mask = jnp.where(i_first >= 1, 0.0, MASKED)
        q_t = [qT_ref[0, g, :, pl.ds(q0[r], Q_BLOCK)] for r, g in chains]
        s_m = [_dot(km_ref[g], q_t[c]) for c, (r, g) in enumerate(chains)]
        s_d = [_dot(k_ref[0, g, pl.ds(q0[r], K_BLOCK), :], q_t[c]) for c, (r, g) in enumerate(chains)]
        s_p = [_dot(k_ref[0, g, pl.ds(p0[r], K_BLOCK), :], q_t[c]) for c, (r, g) in enumerate(chains)]
        p_m = [jnp.exp2(s) for s in s_m]
        p_d = [jnp.exp2(jnp.where(causal, s, MASKED)) for s in s_d]
        p_p = [jnp.exp2(s + first_mask if r == 0 else s) for s, (r, g) in zip(s_p, chains)]
        for c, (r, g) in enumerate(chains):
            l_ref[c] = sublane_partial_sum(p_m[c]) + sublane_partial_sum(p_d[c]) + sublane_partial_sum(p_p[c])
            acc_ref[c] = (_dot(vmT_ref[g], p_m[c].astype(BF16))
                          + _dot(vT_ref[0, g, :, pl.ds(q0[r], K_BLOCK)], p_d[c].astype(BF16))
                          + _dot(vT_ref[0, g, :, pl.ds(p0[r], K_BLOCK)], p_p[c].astype(BF16)))

        for r in range(Q_PER_STEP):
            i = blk[r]

            def cond(j, i=i):
                live = block_live(0, i, j)
                for g in heads[1:]:
                    live = jnp.logical_or(live, block_live(g, i, j))
                return jnp.logical_and(j >= 0, live)

            def body(j, r=r):
                k0 = pl.multiple_of(j * K_BLOCK, K_BLOCK)
                for g in heads:
                    c = r * HEAD_GROUP + g
                    l_j, acc_j = bounded_tile(k_ref[0, g, pl.ds(k0, K_BLOCK), :],
                                              vT_ref[0, g, :, pl.ds(k0, K_BLOCK)],
                                              qT_ref[0, g, :, pl.ds(q0[r], Q_BLOCK)])
                    l_ref[c] += l_j
                    acc_ref[c] += acc_j
                return j - 1

            lax.while_loop(cond, body, i - 2)
        for c, (r, g) in enumerate(chains):
            l_tot = jnp.sum(l_ref[c], axis=0, keepdims=True)
            o_ref[0, g, :, pl.ds(q0[r], Q_BLOCK)] = (acc_ref[c] / l_tot).astype(o_ref.dtype)

    @pl.when(c_bound <= BOUNDED_LOGIT_MAX)
    def _():
        def step(n, carry):
            bounded_q_blocks(n * Q_PER_STEP)
            return carry

        lax.fori_loop(0, nq // Q_PER_STEP, step, 0)

    def online_q_block(g, i):
        q0 = pl.multiple_of(i * Q_BLOCK, Q_BLOCK)
        q_t = qT_ref[0, g, :, pl.ds(q0, Q_BLOCK)]
        s_m = _dot(km_ref[g], q_t)
        m = jnp.max(s_m, axis=0, keepdims=True)
        p_m = jnp.exp2(s_m - m)
        l = jnp.sum(p_m, axis=0, keepdims=True)
        acc = _dot(vmT_ref[g], p_m.astype(BF16))

        def online_step(state, k0, masked):
            m, l, acc = state
            s = _dot(k_ref[0, g, pl.ds(k0, K_BLOCK), :], q_t)
            if masked:
                s = jnp.where(causal, s, MASKED)
            m_new = jnp.maximum(m, jnp.max(s, axis=0, keepdims=True))
            alpha = jnp.exp2(m - m_new)
            p = jnp.exp2(s - m_new)
            l = alpha * l + jnp.sum(p, axis=0, keepdims=True)
            acc = alpha * acc + _dot(vT_ref[0, g, :, pl.ds(k0, K_BLOCK)], p.astype(BF16))
            return m_new, l, acc

        state = online_step((m, l, acc), q0, True)

        def cond(carry):
            return jnp.logical_and(carry[0] >= 0, block_live(g, i, carry[0]))

        def body(carry):
            j = carry[0]
            return (j - 1,) + online_step(carry[1:], pl.multiple_of(j * K_BLOCK, K_BLOCK), False)

        _, m, l, acc = lax.while_loop(cond, body, (i - 1,) + state)
        o_ref[0, g, :, pl.ds(q0, Q_BLOCK)] = (acc / l).astype(o_ref.dtype)

    @pl.when(c_bound > BOUNDED_LOGIT_MAX)
    def _():
        def head_loop(g, carry):
            def step(i, inner):
                online_q_block(g, i)
                return inner

            lax.fori_loop(0, nq, step, 0)
            return carry

        lax.fori_loop(0, HEAD_GROUP, head_loop, 0)


def _fox_attention(qT, k, vT, k_meta, vT_meta, fq0, flast, c_bound):
    nb, _, _, seq = qT.shape
    assert seq % (Q_PER_STEP * Q_BLOCK) == 0 and Q_BLOCK == K_BLOCK
    smem = pl.BlockSpec(memory_space=pltpu.SMEM)
    hg = HEAD_GROUP
    return pl.pallas_call(
        _fox_attention_kernel,
        out_shape=jax.ShapeDtypeStruct((nb, HEADS, HEAD_DIM, seq), BF16),
        grid=(nb, HEADS // hg),
        in_specs=[
            smem, smem, smem,
            pl.BlockSpec((1, hg, AUG_DIM, seq), lambda b, h: (b, h, 0, 0)),
            pl.BlockSpec((1, hg, seq, AUG_DIM), lambda b, h: (b, h, 0, 0)),
            pl.BlockSpec((1, hg, HEAD_DIM, seq), lambda b, h: (b, h, 0, 0)),
            pl.BlockSpec((hg, N_META, AUG_DIM), lambda b, h: (h, 0, 0)),
            pl.BlockSpec((hg, HEAD_DIM, N_META), lambda b, h: (h, 0, 0)),
        ],
        out_specs=pl.BlockSpec((1, hg, HEAD_DIM, seq), lambda b, h: (b, h, 0, 0)),
        scratch_shapes=[pltpu.VMEM((Q_PER_STEP * hg, HEAD_DIM, Q_BLOCK), F32),
                        pltpu.VMEM((Q_PER_STEP * hg, SUBLANES, Q_BLOCK), F32)],
        name="fox_attention",
        compiler_params=pltpu.CompilerParams(
            dimension_semantics=("arbitrary", "arbitrary"), vmem_limit_bytes=VMEM_LIMIT_BYTES),
    )(fq0, flast, c_bound, qT, k, vT, k_meta, vT_meta)


def _mix_out_ffn2_kernel(h1_ref, oT_ref, oconv_ref,
                         gattn_ref, woa_ref, woc_ref,
                         g2_ref, win_hbm, wout_hbm, gfin_ref,
                         out_ref,
                         xn_ref, acc_ref, win_ref, wout_ref, win_stage, wout_stage, win_sem, wout_sem):
    @pl.when(jnp.logical_and(pl.program_id(0) == 0, pl.program_id(1) == 0))
    def _():
        _load_weight_as_bf16(win_hbm, win_ref, win_stage, win_sem)
        _load_weight_as_bf16(wout_hbm, wout_ref, wout_stage, wout_sem, MACARON_SCALE)

    o_t = oT_ref[0].astype(F32)
    ms = jnp.mean(o_t * o_t, axis=0, keepdims=True)
    o_n = (o_t * lax.rsqrt(ms + EPS) * gattn_ref[...]).T.astype(BF16)
    mix = _dot(o_n, woa_ref[...]) + _dot(oconv_ref[0], woc_ref[...])
    h2 = h1_ref[0] + mix
    xn_ref[...] = _rmsnorm_rows(h2, g2_ref[...]).astype(BF16)
    _swiglu(xn_ref, win_ref, wout_ref, acc_ref, h2)
    out_ref[0] = _rmsnorm_rows(acc_ref[...], gfin_ref[...])


def _mix_out_ffn2(h1, o_t, oconv, p, tm):
    nb, seq, _ = h1.shape
    gattn_t = jnp.broadcast_to(p["gattn"][:, None], (D_ATTN, tm))
    in_specs = [
        pl.BlockSpec((1, tm, D_MODEL), lambda b, t: (b, t, 0)),
        pl.BlockSpec((1, D_ATTN, tm), lambda b, t: (b, 0, t)),
        pl.BlockSpec((1, tm, D_CONV), lambda b, t: (b, t, 0)),
        _const_spec((D_ATTN, tm)), _const_spec((D_ATTN, D_MODEL)), _const_spec((D_CONV, D_MODEL)),
        _const_spec((1, D_MODEL)),
        pl.BlockSpec(memory_space=pl.ANY), pl.BlockSpec(memory_space=pl.ANY),
        _const_spec((1, D_MODEL)),
    ]
    return pl.pallas_call(
        _mix_out_ffn2_kernel,
        out_shape=jax.ShapeDtypeStruct((nb, seq, D_MODEL), F32),
        grid=(nb, seq // tm),
        in_specs=in_specs,
        out_specs=pl.BlockSpec((1, tm, D_MODEL), lambda b, t: (b, t, 0)),
        scratch_shapes=[pltpu.VMEM((tm, D_MODEL), BF16), pltpu.VMEM((tm, D_MODEL), F32)] + _ffn_weight_scratch(),
        name="mix_out_ffn2",
        compiler_params=pltpu.CompilerParams(
            dimension_semantics=("arbitrary", "arbitrary"), vmem_limit_bytes=VMEM_LIMIT_BYTES),
    )(h1, o_t, oconv, gattn_t, p["woa"], p["woc"], p["g2"], p["win2"], p["wout2"], p["gfin"])


def kernel(x, meta_tokens, ffn1_norm, ffn1_w_in, ffn1_w_out, mix_norm, w_mix_in, b_forget, q_norm, k_norm, conv_w, attn_out_norm, conv_out_norm, w_mix_out, ffn2_norm, ffn2_w_in, ffn2_w_out, final_norm):
    nb, seq, _ = x.shape
    wmix = w_mix_in[0]
    n_qkv = 3 * D_ATTN
    p = {
        "g1": ffn1_norm, "win1": ffn1_w_in[0], "wout1": ffn1_w_out[0],
        "gmix": mix_norm,
        "wfkT": jnp.concatenate([jnp.pad(wmix[:, n_qkv:n_qkv + HEADS].T, ((0, F_ROWS - HEADS), (0, 0))),
                                 wmix[:, D_ATTN:2 * D_ATTN].T], axis=0).astype(BF16),
        "wqT": wmix[:, :D_ATTN].T.astype(BF16),
        "wvT": wmix[:, 2 * D_ATTN:n_qkv].T.astype(BF16),
        "wcu": wmix[:, n_qkv + HEADS + D_CONV:].astype(BF16),
        "wb": wmix[:, n_qkv + HEADS:n_qkv + HEADS + D_CONV].astype(BF16),
        "bf": jnp.pad(b_forget[0], (0, F_ROWS - HEADS))[:, None],
        "gq": q_norm[0], "gk": k_norm[0],
        "cw": jnp.pad(conv_w[0], ((0, SUBLANES - CONV_WIDTH), (0, 0))),
        "gconv": conv_out_norm,
        "gattn": attn_out_norm[0],
        "woa": w_mix_out[0, :D_ATTN].astype(BF16), "woc": w_mix_out[0, D_ATTN:].astype(BF16),
        "g2": ffn2_norm, "win2": ffn2_w_in[0], "wout2": ffn2_w_out[0],
        "gfin": final_norm,
    }

    n_pad = META_TILE - N_META
    meta = jnp.pad(meta_tokens.astype(x.dtype), ((n_pad, 0), (0, 0)))[None]
    zeros_zc = jnp.zeros((SUBLANES, D_CONV), F32)
    zeros_fc = jnp.zeros((F_ROWS, LANES), F32)
    _, _, k_m, vT_m, _, _, zc_m, fc_m = _ffn1_mix_in(
        meta, jnp.full((1,), n_pad, jnp.int32), zeros_zc, zeros_fc, p, META_TILE)
    k_meta = k_m[0, :, n_pad:, :]
    vT_meta = vT_m[0, :, :, n_pad:]

    h1, qT, k, vT, f_cum, oconv, _, _ = _ffn1_mix_in(
        x, jnp.zeros((1,), jnp.int32), zc_m, fc_m, p, TOKEN_TILE)

    fq0 = f_cum[:, :, 0::Q_BLOCK].reshape(nb * HEADS, seq // Q_BLOCK)
    flast = f_cum[:, :, K_BLOCK - 1::K_BLOCK].reshape(nb * HEADS, seq // K_BLOCK)
    c_bound = (16.0 * 1.02 * jnp.max(jnp.abs(q_norm)) * jnp.max(jnp.abs(k_norm)) + 1.0).reshape(1).astype(F32)

    o_t = _fox_attention(qT, k, vT, k_meta, vT_meta, fq0, flast, c_bound)
    o_t = o_t.reshape(nb, D_ATTN, seq)
    return _mix_out_ffn2(h1, o_t, oconv, p, TOKEN_TILE)
```

```python
import jax
import jax.numpy as jnp
from jax import lax
from jax.experimental import pallas as pl
from jax.experimental.pallas import tpu as pltpu

D_MODEL = 1024
N_META = 16
D_ATTN = 512
D_CONV = 512
HEADS = 8
HEAD_DIM = 64
CONV_WIDTH = 3
D_FF = 2816
EPS = 1e-6

F32 = jnp.float32
BF16 = jnp.bfloat16

LANES = 128
SUBLANES = 8
MXU_DIM = 256
VMEM_LIMIT_BYTES = 60000 * 1024

TOKEN_TILE = 512
META_TILE = LANES
FF_CHUNK = MXU_DIM
N_FF_CHUNKS = D_FF // FF_CHUNK
Q_BLOCK = MXU_DIM
K_BLOCK = MXU_DIM
AUG_DIM = LANES
HEAD_GROUP = 4
Q_PER_STEP = 4
MASKED = -1e30
LOG2E = 1.4426950408889634
F_ROWS = 2 * SUBLANES
MACARON_SCALE = 0.5
WIN_STAGE_ROWS = 64
WOUT_STAGE_ROWS = 256
STAGE_SLOTS = 4

EXP_UNDERFLOW = -104.0
BOUNDED_LOGIT_MAX = 120.0

assert D_FF % FF_CHUNK == 0 and FF_CHUNK % LANES == 0 and HEADS % HEAD_GROUP == 0


def _dot(a, b):
    return jnp.dot(a, b, preferred_element_type=F32)


def _dot_nt(a, b):
    return lax.dot_general(a, b, (((1,), (1,)), ((), ())), preferred_element_type=F32)


def _rmsnorm_rows(x, gain):
    ms = jnp.mean(x * x, axis=-1, keepdims=True)
    return x * lax.rsqrt(ms + EPS) * gain


def _split3(x):
    hi = x.astype(BF16)
    r1 = x - hi.astype(F32)
    mid = r1.astype(BF16)
    lo = (r1 - mid.astype(F32)).astype(BF16)
    return hi, mid, lo


def _load_weight_as_bf16(w_hbm, w_bf, stage, sem, scale=None):
    n_slots, rows = stage.shape[0], stage.shape[1]
    n_chunks = w_hbm.shape[0] // rows
    assert n_chunks * rows == w_hbm.shape[0]

    def chunk_copy(i):
        slot = i % n_slots
        return pltpu.make_async_copy(w_hbm.at[pl.ds(i * rows, rows)], stage.at[slot], sem.at[slot])

    for i in range(min(n_slots - 1, n_chunks)):
        chunk_copy(i).start()
    for i in range(n_chunks):
        if i + n_slots - 1 < n_chunks:
            chunk_copy(i + n_slots - 1).start()
        chunk_copy(i).wait()
        w = stage[i % n_slots]
        w_bf[i * rows:(i + 1) * rows, :] = (w if scale is None else w * scale).astype(BF16)


def _swiglu(xn_ref, win_ref, wout_ref, acc_ref, residual):
    acc_ref[...] = residual
    for c in range(N_FF_CHUNKS):
        lo, hi = c * FF_CHUNK, (c + 1) * FF_CHUNK
        xn = xn_ref[...]
        g = _dot(xn, win_ref[:, lo:hi])
        u = _dot(xn, win_ref[:, D_FF + lo:D_FF + hi])
        a = (g * jax.nn.sigmoid(g) * u).astype(BF16)
        acc_ref[...] += _dot(a, wout_ref[lo:hi, :])


def _ffn1_mix_in_kernel(
        npad_ref,
        x_ref, zc_in_ref, fc_in_ref,
        g1_ref, win_hbm, wout_hbm,
        gmix_ref, wfkT_ref, wqT_ref, wvT_ref, wcu_ref, wb_ref, bf_ref, tri_ref,
        gq_ref, gk_ref, cw_ref, gconv_ref,
        h1_ref, qT_ref, k_ref, vT_ref, f_ref, oconv_ref, zc_out_ref, fc_out_ref,
        xn_ref, acc_ref, zc_ref, fc_ref, win_ref, wout_ref, win_stage, wout_stage, win_sem, wout_sem):
    t = pl.program_id(1)
    tm = x_ref.shape[1]

    @pl.when(jnp.logical_and(pl.program_id(0) == 0, t == 0))
    def _():
        _load_weight_as_bf16(win_hbm, win_ref, win_stage, win_sem)
        _load_weight_as_bf16(wout_hbm, wout_ref, wout_stage, wout_sem, MACARON_SCALE)

    @pl.when(t == 0)
    def _():
        zc_ref[...] = zc_in_ref[...]
        fc_ref[...] = fc_in_ref[...]

    x = x_ref[0]
    xn_ref[...] = _rmsnorm_rows(x, g1_ref[...]).astype(BF16)
    _swiglu(xn_ref, win_ref, wout_ref, acc_ref, x)
    h1 = acc_ref[...]
    h1_ref[0] = h1

    xn2 = _rmsnorm_rows(h1, gmix_ref[...]).astype(BF16)
    fkT = _dot_nt(wfkT_ref[...], xn2)
    cu = _dot(xn2, wcu_ref[...])

    fl = fkT[0:F_ROWS] + bf_ref[...]
    logf = jnp.minimum(fl, 0.0) - jnp.log(1.0 + jnp.exp(-jnp.abs(fl)))
    pos = t * tm + lax.broadcasted_iota(jnp.int32, logf.shape, 1)
    logf = jnp.where(pos >= npad_ref[0], logf, 0.0)
    pieces = _dot(jnp.concatenate(_split3(logf), axis=0), tri_ref[...])
    csum = pieces[0:F_ROWS] + pieces[F_ROWS:2 * F_ROWS] + pieces[2 * F_ROWS:3 * F_ROWS]
    f_all = csum + jnp.concatenate([fc_ref[...]] * (tm // LANES), axis=1)
    fc_new = jnp.broadcast_to(f_all[:, tm - 1:tm], fc_ref.shape)
    fc_ref[...] = fc_new
    fc_out_ref[...] = fc_new
    f_ref[0] = f_all[:HEADS]

    f_hi, f_mid, f_lo = (p.astype(F32) for p in _split3(f_all * LOG2E))
    row = lax.broadcasted_iota(jnp.int32, (SUBLANES, tm), 0)
    ones_mid = jnp.where(row < 6, 1.0, 0.0)
    pad_rows = jnp.zeros((AUG_DIM - HEAD_DIM - SUBLANES, tm), F32)

    def head_pieces(h):
        return (jnp.broadcast_to(f_hi[h:h + 1], (SUBLANES, tm)),
                jnp.broadcast_to(f_mid[h:h + 1], (SUBLANES, tm)),
                jnp.broadcast_to(f_lo[h:h + 1], (SUBLANES, tm)))

    def head_rmsnorm(x_t, gain_t):
        x3 = x_t.reshape(HEADS, HEAD_DIM, tm)
        return x3 * lax.rsqrt(jnp.mean(x3 * x3, axis=1, keepdims=True) + EPS) * gain_t[None]

    kn = head_rmsnorm(fkT[F_ROWS:F_ROWS + D_ATTN], gk_ref[...])
    qT = _dot_nt(wqT_ref[...], xn2)
    gate_b = _dot(xn2, wb_ref[...])
    for h in range(HEADS):
        fh, fm, fo = head_pieces(h)
        aug_k = jnp.where(row == 3, -fh, jnp.where(row == 4, -fm, jnp.where(row == 5, -fo, ones_mid)))
        k_aug_t = jnp.concatenate([kn[h], aug_k, pad_rows], axis=0)
        k_ref[0, h] = k_aug_t.T.astype(BF16)

    vT = _dot_nt(wvT_ref[...], xn2)
    z = cu[:, 0:D_CONV] * cu[:, D_CONV:2 * D_CONV]
    zc = zc_ref[...]
    rowz = lax.broadcasted_iota(jnp.int32, z.shape, 0)
    prev1 = jnp.broadcast_to(zc[7:8], z.shape)
    prev2 = jnp.broadcast_to(zc[6:7], z.shape)
    z1 = jnp.where(rowz == 0, prev1, pltpu.roll(z, 1, axis=0))
    z2 = jnp.where(rowz == 0, prev2, jnp.where(rowz == 1, prev1, pltpu.roll(z, 2, axis=0)))
    cw = cw_ref[...]
    y = cw[0:1] * z2 + cw[1:2] * z1 + cw[2:3] * z
    oconv_ref[0] = _rmsnorm_rows(gate_b * y, gconv_ref[...]).astype(BF16)
    zc_new = z[tm - SUBLANES:tm]
    zc_ref[...] = zc_new
    zc_out_ref[...] = zc_new

    qn = head_rmsnorm(qT, gq_ref[...] * (HEAD_DIM ** -0.5 * LOG2E))
    v3 = vT.reshape(HEADS, HEAD_DIM, tm)
    for h in range(HEADS):
        fh, fm, fo = head_pieces(h)
        aug_q = jnp.where(row == 0, fh, jnp.where(row == 1, fm, jnp.where(row == 2, fo, ones_mid)))
        qT_ref[0, h] = jnp.concatenate([qn[h], aug_q, pad_rows], axis=0).astype(BF16)
        vT_ref[0, h] = v3[h].astype(BF16)


def _ffn_weight_scratch():
    return [pltpu.VMEM((D_MODEL, 2 * D_FF), BF16), pltpu.VMEM((D_FF, D_MODEL), BF16),
            pltpu.VMEM((STAGE_SLOTS, WIN_STAGE_ROWS, 2 * D_FF), F32),
            pltpu.VMEM((STAGE_SLOTS, WOUT_STAGE_ROWS, D_MODEL), F32),
            pltpu.SemaphoreType.DMA((STAGE_SLOTS,)), pltpu.SemaphoreType.DMA((STAGE_SLOTS,))]


def _const_spec(shape):
    nd = len(shape)
    return pl.BlockSpec(shape, lambda *_: (0,) * nd, pipeline_mode=pl.Buffered(1))


def _ffn1_mix_in(x, n_pad, zc_in, fc_in, p, tm):
    nb, seq, _ = x.shape
    nt = seq // tm
    tri = jnp.triu(jnp.ones((tm, tm), BF16))
    gq_t = jnp.broadcast_to(p["gq"][:, None], (HEAD_DIM, tm))
    gk_t = jnp.broadcast_to(p["gk"][:, None], (HEAD_DIM, tm))

    def tile3(last):
        return pl.BlockSpec((1, tm, last), lambda b, t, *_: (b, t, 0))

    in_specs = [
        tile3(D_MODEL),
        _const_spec((SUBLANES, D_CONV)), _const_spec((F_ROWS, LANES)),
        _const_spec((1, D_MODEL)),
        pl.BlockSpec(memory_space=pl.ANY), pl.BlockSpec(memory_space=pl.ANY),
        _const_spec((1, D_MODEL)), _const_spec((F_ROWS + D_ATTN, D_MODEL)),
        _const_spec((D_ATTN, D_MODEL)), _const_spec((D_ATTN, D_MODEL)),
        _const_spec((D_MODEL, 2 * D_CONV)), _const_spec((D_MODEL, D_CONV)),
        _const_spec((F_ROWS, 1)), _const_spec((tm, tm)),
        _const_spec((HEAD_DIM, tm)), _const_spec((HEAD_DIM, tm)),
        _const_spec((SUBLANES, D_CONV)), _const_spec((1, D_CONV)),
    ]
    out_shape = [
        jax.ShapeDtypeStruct((nb, seq, D_MODEL), F32),
        jax.ShapeDtypeStruct((nb, HEADS, AUG_DIM, seq), BF16),
        jax.ShapeDtypeStruct((nb, HEADS, seq, AUG_DIM), BF16),
        jax.ShapeDtypeStruct((nb, HEADS, HEAD_DIM, seq), BF16),
        jax.ShapeDtypeStruct((nb, HEADS, seq), F32),
        jax.ShapeDtypeStruct((nb, seq, D_CONV), BF16),
        jax.ShapeDtypeStruct((SUBLANES, D_CONV), F32),
        jax.ShapeDtypeStruct((F_ROWS, LANES), F32),
    ]
    out_specs = [
        tile3(D_MODEL),
        pl.BlockSpec((1, HEADS, AUG_DIM, tm), lambda b, t, *_: (b, 0, 0, t)),
        pl.BlockSpec((1, HEADS, tm, AUG_DIM), lambda b, t, *_: (b, 0, t, 0)),
        pl.BlockSpec((1, HEADS, HEAD_DIM, tm), lambda b, t, *_: (b, 0, 0, t)),
        pl.BlockSpec((1, HEADS, tm), lambda b, t, *_: (b, 0, t)),
        tile3(D_CONV),
        pl.BlockSpec((SUBLANES, D_CONV), lambda b, t, *_: (0, 0)),
        pl.BlockSpec((F_ROWS, LANES), lambda b, t, *_: (0, 0)),
    ]
    grid_spec = pltpu.PrefetchScalarGridSpec(
        num_scalar_prefetch=1, grid=(nb, nt), in_specs=in_specs, out_specs=out_specs,
        scratch_shapes=[
            pltpu.VMEM((tm, D_MODEL), BF16),
            pltpu.VMEM((tm, D_MODEL), F32),
            pltpu.VMEM((SUBLANES, D_CONV), F32),
            pltpu.VMEM((F_ROWS, LANES), F32),
        ] + _ffn_weight_scratch())
    return pl.pallas_call(
        _ffn1_mix_in_kernel, out_shape=out_shape, grid_spec=grid_spec, name="ffn1_mix_in",
        compiler_params=pltpu.CompilerParams(
            dimension_semantics=("arbitrary", "arbitrary"), vmem_limit_bytes=VMEM_LIMIT_BYTES),
    )(n_pad, x, zc_in, fc_in,
      p["g1"], p["win1"], p["wout1"],
      p["gmix"], p["wfkT"], p["wqT"], p["wvT"], p["wcu"], p["wb"], p["bf"], tri,
      gq_t, gk_t, p["cw"], p["gconv"])


def _fox_attention_kernel(fq0_ref, flast_ref, fmeta_ref, cb_ref,
                          qT_ref, k_ref, vT_ref, km_ref, vmT_ref,
                          o_ref,
                          acc_ref, l_ref):
    bh0 = pl.program_id(0) * HEADS + pl.program_id(1) * HEAD_GROUP
    heads = range(HEAD_GROUP)
    nq = qT_ref.shape[3] // Q_BLOCK
    c_bound = cb_ref[0]
    key_idx = lax.broadcasted_iota(jnp.int32, (K_BLOCK, Q_BLOCK), 0)
    qry_idx = lax.broadcasted_iota(jnp.int32, (K_BLOCK, Q_BLOCK), 1)
    causal = key_idx <= qry_idx

    def block_live(g, i, j):
        return fq0_ref[bh0 + g, i] - flast_ref[bh0 + g, jnp.maximum(j, 0)] + c_bound >= EXP_UNDERFLOW

    def meta_live(g, i):
        head = pl.program_id(1) * HEAD_GROUP + g
        return fq0_ref[bh0 + g, i] - fmeta_ref[head] + c_bound >= EXP_UNDERFLOW

    def sublane_partial_sum(p):
        return jnp.sum(p.reshape(p.shape[0] // SUBLANES, SUBLANES, p.shape[1]), axis=0)

    def bounded_tile(k_blk, v_t, q_t):
        p = jnp.exp2(_dot(k_blk, q_t))
        return sublane_partial_sum(p), _dot(v_t, p.astype(BF16))

    def bounded_q_blocks(i_first):
        chains = [(r, g) for r in range(Q_PER_STEP) for g in heads]
        blk = [i_first + r for r in range(Q_PER_STEP)]
        q0 = [pl.multiple_of(i * Q_BLOCK, Q_BLOCK) for i in blk]
        p0 = [pl.multiple_of(jnp.maximum(i - 1, 0) * K_BLOCK, K_BLOCK) for i in blk]
        first_mask = jnp.where(i_first >= 1, 0.0, MASKED)
        q_t = [qT_ref[0, g, :, pl.ds(q0[r], Q_BLOCK)] for r, g in chains]
        s_d = [_dot(k_ref[0, g, pl.ds(q0[r], K_BLOCK), :], q_t[c]) for c, (r, g) in enumerate(chains)]
        s_p = [_dot(k_ref[0, g, pl.ds(p0[r], K_BLOCK), :], q_t[c]) for c, (r, g) in enumerate(chains)]
        p_d = [jnp.exp2(jnp.where(causal, s, MASKED)) for s in s_d]
        p_p = [jnp.exp2(s + first_mask if r == 0 else s) for s, (r, g) in zip(s_p, chains)]
        for c, (r, g) in enumerate(chains):
            l_ref[c] = sublane_partial_sum(p_d[c]) + sublane_partial_sum(p_p[c])
            acc_ref[c] = (_dot(vT_ref[0, g, :, pl.ds(q0[r], K_BLOCK)], p_d[c].astype(BF16))
                          + _dot(vT_ref[0, g, :, pl.ds(p0[r], K_BLOCK)], p_p[c].astype(BF16)))

        for r in range(Q_PER_STEP):
            i = blk[r]

            def cond(j, i=i):
                live = block_live(0, i, j)
                for g in heads[1:]:
                    live = jnp.logical_or(live, block_live(g, i, j))
                return jnp.logical_and(j >= 0, live)

            def body(j, r=r):
                k0 = pl.multiple_of(j * K_BLOCK, K_BLOCK)
                for g in heads:
                    c = r * HEAD_GROUP + g
                    l_j, acc_j = bounded_tile(k_ref[0, g, pl.ds(k0, K_BLOCK), :],
                                              vT_ref[0, g, :, pl.ds(k0, K_BLOCK)],
                                              qT_ref[0, g, :, pl.ds(q0[r], Q_BLOCK)])
                    l_ref[c] += l_j
                    acc_ref[c] += acc_j
                return j - 1

            lax.while_loop(cond, body, i - 2)

            any_meta_live = meta_live(0, i)
            for g in heads[1:]:
                any_meta_live = jnp.logical_or(any_meta_live, meta_live(g, i))

            @pl.when(any_meta_live)
            def _(r=r):
                for g in heads:
                    c = r * HEAD_GROUP + g
                    l_m, acc_m = bounded_tile(km_ref[g], vmT_ref[g], qT_ref[0, g, :, pl.ds(q0[r], Q_BLOCK)])
                    l_ref[c] += l_m
                    acc_ref[c] += acc_m

        for c, (r, g) in enumerate(chains):
            l_tot = jnp.sum(l_ref[c], axis=0, keepdims=True)
            o_ref[0, g, :, pl.ds(q0[r], Q_BLOCK)] = (acc_ref[c] / l_tot).astype(o_ref.dtype)

    @pl.when(c_bound <= BOUNDED_LOGIT_MAX)
    def _():
        def step(n, carry):
            bounded_q_blocks(n * Q_PER_STEP)
            return carry

        lax.fori_loop(0, nq // Q_PER_STEP, step, 0)

    def online_q_block(g, i):
        q0 = pl.multiple_of(i * Q_BLOCK, Q_BLOCK)
        q_t = qT_ref[0, g, :, pl.ds(q0, Q_BLOCK)]
        s_m = _dot(km_ref[g], q_t)
        m = jnp.max(s_m, axis=0, keepdims=True)
        p_m = jnp.exp2(s_m - m)
        l = jnp.sum(p_m, axis=0, keepdims=True)
        acc = _dot(vmT_ref[g], p_m.astype(BF16))

        def online_step(state, k0, masked):
            m, l, acc = state
            s = _dot(k_ref[0, g, pl.ds(k0, K_BLOCK), :], q_t)
            if masked:
                s = jnp.where(causal, s, MASKED)
            m_new = jnp.maximum(m, jnp.max(s, axis=0, keepdims=True))
            alpha = jnp.exp2(m - m_new)
            p = jnp.exp2(s - m_new)
            l = alpha * l + jnp.sum(p, axis=0, keepdims=True)
            acc = alpha * acc + _dot(vT_ref[0, g, :, pl.ds(k0, K_BLOCK)], p.astype(BF16))
            return m_new, l, acc

        state = online_step((m, l, acc), q0, True)

        def cond(carry):
            return jnp.logical_and(carry[0] >= 0, block_live(g, i, carry[0]))

        def body(carry):
            j = carry[0]
            return (j - 1,) + online_step(carry[1:], pl.multiple_of(j * K_BLOCK, K_BLOCK), False)

        _, m, l, acc = lax.while_loop(cond, body, (i - 1,) + state)
        o_ref[0, g, :, pl.ds(q0, Q_BLOCK)] = (acc / l).astype(o_ref.dtype)

    @pl.when(c_bound > BOUNDED_LOGIT_MAX)
    def _():
        def head_loop(g, carry):
            def step(i, inner):
                online_q_block(g, i)
                return inner

            lax.fori_loop(0, nq, step, 0)
            return carry

        lax.fori_loop(0, HEAD_GROUP, head_loop, 0)


def _fox_attention(qT, k, vT, k_meta, vT_meta, fq0, flast, fmeta, c_bound):
    nb, _, _, seq = qT.shape
    assert seq % (Q_PER_STEP * Q_BLOCK) == 0 and Q_BLOCK == K_BLOCK
    smem = pl.BlockSpec(memory_space=pltpu.SMEM)
    hg = HEAD_GROUP
    return pl.pallas_call(
        _fox_attention_kernel,
        out_shape=jax.ShapeDtypeStruct((nb, HEADS, HEAD_DIM, seq), BF16),
        grid=(nb, HEADS // hg),
        in_specs=[
            smem, smem, smem, smem,
            pl.BlockSpec((1, hg, AUG_DIM, seq), lambda b, h: (b, h, 0, 0)),
            pl.BlockSpec((1, hg, seq, AUG_DIM), lambda b, h: (b, h, 0, 0)),
            pl.BlockSpec((1, hg, HEAD_DIM, seq), lambda b, h: (b, h, 0, 0)),
            pl.BlockSpec((hg, N_META, AUG_DIM), lambda b, h: (h, 0, 0)),
            pl.BlockSpec((hg, HEAD_DIM, N_META), lambda b, h: (h, 0, 0)),
        ],
        out_specs=pl.BlockSpec((1, hg, HEAD_DIM, seq), lambda b, h: (b, h, 0, 0)),
        scratch_shapes=[pltpu.VMEM((Q_PER_STEP * hg, HEAD_DIM, Q_BLOCK), F32),
                        pltpu.VMEM((Q_PER_STEP * hg, SUBLANES, Q_BLOCK), F32)],
        name="fox_attention",
        compiler_params=pltpu.CompilerParams(
            dimension_semantics=("arbitrary", "arbitrary"), vmem_limit_bytes=VMEM_LIMIT_BYTES),
    )(fq0, flast, fmeta, c_bound, qT, k, vT, k_meta, vT_meta)


def _mix_out_ffn2_kernel(h1_ref, oT_ref, oconv_ref,
                         gattn_ref, woa_ref, woc_ref,
                         g2_ref, win_hbm, wout_hbm, gfin_ref,
                         out_ref,
                         xn_ref, acc_ref, win_ref, wout_ref, win_stage, wout_stage, win_sem, wout_sem):
    @pl.when(jnp.logical_and(pl.program_id(0) == 0, pl.program_id(1) == 0))
    def _():
        _load_weight_as_bf16(win_hbm, win_ref, win_stage, win_sem)
        _load_weight_as_bf16(wout_hbm, wout_ref, wout_stage, wout_sem, MACARON_SCALE)

    o_t = oT_ref[0].astype(F32)
    ms = jnp.mean(o_t * o_t, axis=0, keepdims=True)
    o_n = (o_t * lax.rsqrt(ms + EPS) * gattn_ref[...]).T.astype(BF16)
    mix = _dot(o_n, woa_ref[...]) + _dot(oconv_ref[0], woc_ref[...])
    h2 = h1_ref[0] + mix
    xn_ref[...] = _rmsnorm_rows(h2, g2_ref[...]).astype(BF16)
    _swiglu(xn_ref, win_ref, wout_ref, acc_ref, h2)
    out_ref[0] = _rmsnorm_rows(acc_ref[...], gfin_ref[...])


def _mix_out_ffn2(h1, o_t, oconv, p, tm):
    nb, seq, _ = h1.shape
    gattn_t = jnp.broadcast_to(p["gattn"][:, None], (D_ATTN, tm))
    in_specs = [
        pl.BlockSpec((1, tm, D_MODEL), lambda b, t: (b, t, 0)),
        pl.BlockSpec((1, D_ATTN, tm), lambda b, t: (b, 0, t)),
        pl.BlockSpec((1, tm, D_CONV), lambda b, t: (b, t, 0)),
        _const_spec((D_ATTN, tm)), _const_spec((D_ATTN, D_MODEL)), _const_spec((D_CONV, D_MODEL)),
        _const_spec((1, D_MODEL)),
        pl.BlockSpec(memory_space=pl.ANY), pl.BlockSpec(memory_space=pl.ANY),
        _const_spec((1, D_MODEL)),
    ]
    return pl.pallas_call(
        _mix_out_ffn2_kernel,
        out_shape=jax.ShapeDtypeStruct((nb, seq, D_MODEL), F32),
        grid=(nb, seq // tm),
        in_specs=in_specs,
        out_specs=pl.BlockSpec((1, tm, D_MODEL), lambda b, t: (b, t, 0)),
        scratch_shapes=[pltpu.VMEM((tm, D_MODEL), BF16), pltpu.VMEM((tm, D_MODEL), F32)] + _ffn_weight_scratch(),
        name="mix_out_ffn2",
        compiler_params=pltpu.CompilerParams(
            dimension_semantics=("arbitrary", "arbitrary"), vmem_limit_bytes=VMEM_LIMIT_BYTES),
    )(h1, o_t, oconv, gattn_t, p["woa"], p["woc"], p["g2"], p["win2"], p["wout2"], p["gfin"])


def kernel(x, meta_tokens, ffn1_norm, ffn1_w_in, ffn1_w_out, mix_norm, w_mix_in, b_forget, q_norm, k_norm, conv_w, attn_out_norm, conv_out_norm, w_mix_out, ffn2_norm, ffn2_w_in, ffn2_w_out, final_norm):
    nb, seq, _ = x.shape
    wmix = w_mix_in[0]
    n_qkv = 3 * D_ATTN
    p = {
        "g1": ffn1_norm, "win1": ffn1_w_in[0], "wout1": ffn1_w_out[0],
        "gmix": mix_norm,
        "wfkT": jnp.concatenate([jnp.pad(wmix[:, n_qkv:n_qkv + HEADS].T, ((0, F_ROWS - HEADS), (0, 0))),
                                 wmix[:, D_ATTN:2 * D_ATTN].T], axis=0).astype(BF16),
        "wqT": wmix[:, :D_ATTN].T.astype(BF16),
        "wvT": wmix[:, 2 * D_ATTN:n_qkv].T.astype(BF16),
        "wcu": wmix[:, n_qkv + HEADS + D_CONV:].astype(BF16),
        "wb": wmix[:, n_qkv + HEADS:n_qkv + HEADS + D_CONV].astype(BF16),
        "bf": jnp.pad(b_forget[0], (0, F_ROWS - HEADS))[:, None],
        "gq": q_norm[0], "gk": k_norm[0],
        "cw": jnp.pad(conv_w[0], ((0, SUBLANES - CONV_WIDTH), (0, 0))),
        "gconv": conv_out_norm,
        "gattn": attn_out_norm[0],
        "woa": w_mix_out[0, :D_ATTN].astype(BF16), "woc": w_mix_out[0, D_ATTN:].astype(BF16),
        "g2": ffn2_norm, "win2": ffn2_w_in[0], "wout2": ffn2_w_out[0],
        "gfin": final_norm,
    }

    n_pad = META_TILE - N_META
    meta = jnp.pad(meta_tokens.astype(x.dtype), ((n_pad, 0), (0, 0)))[None]
    zeros_zc = jnp.zeros((SUBLANES, D_CONV), F32)
    zeros_fc = jnp.zeros((F_ROWS, LANES), F32)
    _, _, k_m, vT_m, _, _, zc_m, fc_m = _ffn1_mix_in(
        meta, jnp.full((1,), n_pad, jnp.int32), zeros_zc, zeros_fc, p, META_TILE)
    k_meta = k_m[0, :, n_pad:, :]
    vT_meta = vT_m[0, :, :, n_pad:]

    h1, qT, k, vT, f_cum, oconv, _, _ = _ffn1_mix_in(
        x, jnp.zeros((1,), jnp.int32), zc_m, fc_m, p, TOKEN_TILE)

    fq0 = f_cum[:, :, 0::Q_BLOCK].reshape(nb * HEADS, seq // Q_BLOCK)
    flast = f_cum[:, :, K_BLOCK - 1::K_BLOCK].reshape(nb * HEADS, seq // K_BLOCK)
    c_bound = (16.0 * 1.02 * jnp.max(jnp.abs(q_norm)) * jnp.max(jnp.abs(k_norm)) + 1.0).reshape(1).astype(F32)

    o_t = _fox_attention(qT, k, vT, k_meta, vT_meta, fq0, flast, fc_m[:HEADS, 0], c_bound)
    o_t = o_t.reshape(nb, D_ATTN, seq)
    return _mix_out_ffn2(h1, o_t, oconv, p, TOKEN_TILE)
```

```python
from typing import Any, NamedTuple

import jax
import jax.numpy as jnp
from jax import lax
from jax.experimental import pallas as pl
from jax.experimental.pallas import tpu as pltpu

D_MODEL = 1024
N_META = 16
D_ATTN = 512
D_CONV = 512
HEADS = 8
HEAD_DIM = 64
CONV_WIDTH = 3
D_FF = 2816
EPS = 1e-6

F32 = jnp.float32
BF16 = jnp.bfloat16

LANES = 128
SUBLANES = 8
MXU_DIM = 256
VMEM_LIMIT_BYTES = 60000 * 1024

TOKEN_TILE = 512
META_TILE = LANES
FF_CHUNK = MXU_DIM
N_FF_CHUNKS = D_FF // FF_CHUNK
Q_BLOCK = MXU_DIM
K_BLOCK = MXU_DIM
AUG_DIM = LANES
HEAD_GROUP = 4
Q_PER_STEP = 4
MASKED = -1e30
LOG2E = 1.4426950408889634
F_ROWS = 2 * SUBLANES
MACARON_SCALE = 0.5
WIN_STAGE_ROWS = 64
WOUT_STAGE_ROWS = 256
STAGE_SLOTS = 4

EXP_UNDERFLOW = -104.0
BOUNDED_LOGIT_MAX = 120.0

assert D_FF % FF_CHUNK == 0 and FF_CHUNK % LANES == 0 and HEADS % HEAD_GROUP == 0


def _dot(a, b):
    return jnp.dot(a, b, preferred_element_type=F32)


def _dot_nt(a, b):
    return lax.dot_general(a, b, (((1,), (1,)), ((), ())), preferred_element_type=F32)


def _rmsnorm_rows(x, gain):
    ms = jnp.mean(x * x, axis=-1, keepdims=True)
    return x * lax.rsqrt(ms + EPS) * gain


def _split3(x):
    hi = x.astype(BF16)
    r1 = x - hi.astype(F32)
    mid = r1.astype(BF16)
    lo = (r1 - mid.astype(F32)).astype(BF16)
    return hi, mid, lo


def _load_weight_as_bf16(w_hbm, w_bf, stage, sem, scale=None):
    n_slots, rows = stage.shape[0], stage.shape[1]
    n_chunks = w_hbm.shape[0] // rows
    assert n_chunks * rows == w_hbm.shape[0]

    def chunk_copy(i):
        slot = i % n_slots
        return pltpu.make_async_copy(w_hbm.at[pl.ds(i * rows, rows)], stage.at[slot], sem.at[slot])

    for i in range(min(n_slots - 1, n_chunks)):
        chunk_copy(i).start()
    for i in range(n_chunks):
        if i + n_slots - 1 < n_chunks:
            chunk_copy(i + n_slots - 1).start()
        chunk_copy(i).wait()
        w = stage[i % n_slots]
        w_bf[i * rows:(i + 1) * rows, :] = (w if scale is None else w * scale).astype(BF16)


def _swiglu(xn_ref, win_ref, wout_ref, acc_ref, residual):
    acc_ref[...] = residual
    for c in range(N_FF_CHUNKS):
        lo, hi = c * FF_CHUNK, (c + 1) * FF_CHUNK
        xn = xn_ref[...]
        g = _dot(xn, win_ref[:, lo:hi])
        u = _dot(xn, win_ref[:, D_FF + lo:D_FF + hi])
        a = (g * jax.nn.sigmoid(g) * u).astype(BF16)
        acc_ref[...] += _dot(a, wout_ref[lo:hi, :])


class _MixWeights(NamedTuple):
    g1: Any
    win: Any
    wout: Any
    gmix: Any
    wfkT: Any
    wqT: Any
    wvT: Any
    wcu: Any
    wb: Any
    bf: Any
    tri: Any
    gq: Any
    gk: Any
    cw: Any
    gconv: Any


class _TileOut(NamedTuple):
    h1: Any
    q: Any
    k: Any
    v: Any
    f: Any
    oconv: Any


def _token_tile(x, n_valid, w, xn_ref, acc_ref, zc_ref, fc_ref, out):
    tm = x.shape[0]

    xn_ref[...] = _rmsnorm_rows(x, w.g1[...]).astype(BF16)
    _swiglu(xn_ref, w.win, w.wout, acc_ref, x)
    h1 = acc_ref[...]
    if out.h1 is not None:
        out.h1[0] = h1

    xn2 = _rmsnorm_rows(h1, w.gmix[...]).astype(BF16)
    fkT = _dot_nt(w.wfkT[...], xn2)
    cu = _dot(xn2, w.wcu[...])

    fl = fkT[0:F_ROWS] + w.bf[...]
    logf = jnp.minimum(fl, 0.0) - jnp.log(1.0 + jnp.exp(-jnp.abs(fl)))
    if n_valid < tm:
        logf = jnp.where(lax.broadcasted_iota(jnp.int32, logf.shape, 1) < n_valid, logf, 0.0)
    pieces = _dot(jnp.concatenate(_split3(logf), axis=0), w.tri[0:tm, 0:tm])
    csum = pieces[0:F_ROWS] + pieces[F_ROWS:2 * F_ROWS] + pieces[2 * F_ROWS:3 * F_ROWS]
    f_all = csum + jnp.concatenate([fc_ref[...]] * (tm // LANES), axis=1)
    fc_ref[...] = jnp.broadcast_to(f_all[:, tm - 1:tm], fc_ref.shape)
    if out.f is not None:
        out.f[0] = f_all[:HEADS]

    f_hi, f_mid, f_lo = (p.astype(F32) for p in _split3(f_all * LOG2E))
    row = lax.broadcasted_iota(jnp.int32, (SUBLANES, tm), 0)
    ones_mid = jnp.where(row < 6, 1.0, 0.0)
    pad_rows = jnp.zeros((AUG_DIM - HEAD_DIM - SUBLANES, tm), F32)

    def head_pieces(h):
        return (jnp.broadcast_to(f_hi[h:h + 1], (SUBLANES, tm)),
                jnp.broadcast_to(f_mid[h:h + 1], (SUBLANES, tm)),
                jnp.broadcast_to(f_lo[h:h + 1], (SUBLANES, tm)))

    def head_rmsnorm(x_t, gain_t):
        x3 = x_t.reshape(HEADS, HEAD_DIM, tm)
        return x3 * lax.rsqrt(jnp.mean(x3 * x3, axis=1, keepdims=True) + EPS) * gain_t[None]

    kn = head_rmsnorm(fkT[F_ROWS:F_ROWS + D_ATTN], w.gk[:, 0:tm])
    if out.q is not None:
        qT = _dot_nt(w.wqT[...], xn2)
    if out.oconv is not None:
        gate_b = _dot(xn2, w.wb[...])
    for h in range(HEADS):
        fh, fm, fo = head_pieces(h)
        aug_k = jnp.where(row == 3, -fh, jnp.where(row == 4, -fm, jnp.where(row == 5, -fo, ones_mid)))
        k_aug_t = jnp.concatenate([kn[h], aug_k, pad_rows], axis=0)
        out.k(h, k_aug_t.T.astype(BF16))

    vT = _dot_nt(w.wvT[...], xn2)
    z = cu[:, 0:D_CONV] * cu[:, D_CONV:2 * D_CONV]
    if out.oconv is not None:
        zc = zc_ref[...]
        rowz = lax.broadcasted_iota(jnp.int32, z.shape, 0)
        prev1 = jnp.broadcast_to(zc[7:8], z.shape)
        prev2 = jnp.broadcast_to(zc[6:7], z.shape)
        z1 = jnp.where(rowz == 0, prev1, pltpu.roll(z, 1, axis=0))
        z2 = jnp.where(rowz == 0, prev2, jnp.where(rowz == 1, prev1, pltpu.roll(z, 2, axis=0)))
        cw = w.cw[...]
        y = cw[0:1] * z2 + cw[1:2] * z1 + cw[2:3] * z
        out.oconv[0] = _rmsnorm_rows(gate_b * y, w.gconv[...]).astype(BF16)
    zc_ref[...] = z[n_valid - SUBLANES:n_valid]

    if out.q is not None:
        qn = head_rmsnorm(qT, w.gq[:, 0:tm] * (HEAD_DIM ** -0.5 * LOG2E))
    v3 = vT.reshape(HEADS, HEAD_DIM, tm)
    for h in range(HEADS):
        if out.q is not None:
            fh, fm, fo = head_pieces(h)
            aug_q = jnp.where(row == 0, fh, jnp.where(row == 1, fm, jnp.where(row == 2, fo, ones_mid)))
            out.q[0, h] = jnp.concatenate([qn[h], aug_q, pad_rows], axis=0).astype(BF16)
        out.v(h, v3[h].astype(BF16))


def _ffn1_mix_in_kernel(
        x_ref, meta_ref,
        g1_ref, win_hbm, wout_hbm,
        gmix_ref, wfkT_ref, wqT_ref, wvT_ref, wcu_ref, wb_ref, bf_ref, tri_ref,
        gq_ref, gk_ref, cw_ref, gconv_ref,
        h1_ref, qT_ref, k_ref, vT_ref, f_ref, oconv_ref, km_ref, vmT_ref, fmeta_ref,
        xn_ref, acc_ref, zc_ref, fc_ref, zc0_ref, fc0_ref,
        win_ref, wout_ref, win_stage, wout_stage, win_sem, wout_sem):
    t = pl.program_id(1)
    w = _MixWeights(g1_ref, win_ref, wout_ref, gmix_ref, wfkT_ref, wqT_ref, wvT_ref, wcu_ref, wb_ref,
                    bf_ref, tri_ref, gq_ref, gk_ref, cw_ref, gconv_ref)

    def put_k(h, k_aug):
        k_ref[0, h] = k_aug

    def put_v(h, v_t):
        vT_ref[0, h] = v_t

    def put_meta_k(h, k_aug):
        km_ref[h] = k_aug[0:N_META]

    def put_meta_v(h, v_t):
        vmT_ref[h] = v_t[:, 0:N_META]

    @pl.when(jnp.logical_and(pl.program_id(0) == 0, t == 0))
    def _():
        _load_weight_as_bf16(win_hbm, win_ref, win_stage, win_sem)
        _load_weight_as_bf16(wout_hbm, wout_ref, wout_stage, wout_sem, MACARON_SCALE)
        rows = meta_ref.shape[0]
        zc_ref[...] = jnp.zeros_like(zc_ref)
        fc_ref[...] = jnp.zeros_like(fc_ref)
        _token_tile(meta_ref[...], N_META, w, xn_ref.at[pl.ds(0, rows)], acc_ref.at[pl.ds(0, rows)],
                    zc_ref, fc_ref, _TileOut(None, None, put_meta_k, put_meta_v, None, None))
        zc0_ref[...] = zc_ref[...]
        fc0_ref[...] = fc_ref[...]
        fmeta_ref[...] = fc_ref[...]

    @pl.when(t == 0)
    def _():
        zc_ref[...] = zc0_ref[...]
        fc_ref[...] = fc0_ref[...]

    _token_tile(x_ref[0], x_ref.shape[1], w, xn_ref, acc_ref, zc_ref, fc_ref,
                _TileOut(h1_ref, qT_ref, put_k, put_v, f_ref, oconv_ref))


def _ffn_weight_scratch():
    return [pltpu.VMEM((D_MODEL, 2 * D_FF), BF16), pltpu.VMEM((D_FF, D_MODEL), BF16),
            pltpu.VMEM((STAGE_SLOTS, WIN_STAGE_ROWS, 2 * D_FF), F32),
            pltpu.VMEM((STAGE_SLOTS, WOUT_STAGE_ROWS, D_MODEL), F32),
            pltpu.SemaphoreType.DMA((STAGE_SLOTS,)), pltpu.SemaphoreType.DMA((STAGE_SLOTS,))]


def _const_spec(shape):
    nd = len(shape)
    return pl.BlockSpec(shape, lambda *_: (0,) * nd, pipeline_mode=pl.Buffered(1))


def _ffn1_mix_in(x, meta, p, tm):
    nb, seq, _ = x.shape
    nt = seq // tm
    assert tm % LANES == 0 and meta.shape[0] <= tm
    tri = jnp.triu(jnp.ones((tm, tm), BF16))
    gq_t = jnp.broadcast_to(p["gq"][:, None], (HEAD_DIM, tm))
    gk_t = jnp.broadcast_to(p["gk"][:, None], (HEAD_DIM, tm))

    def tile3(last):
        return pl.BlockSpec((1, tm, last), lambda b, t: (b, t, 0))

    in_specs = [
        tile3(D_MODEL), _const_spec(meta.shape),
        _const_spec((1, D_MODEL)),
        pl.BlockSpec(memory_space=pl.ANY), pl.BlockSpec(memory_space=pl.ANY),
        _const_spec((1, D_MODEL)), _const_spec((F_ROWS + D_ATTN, D_MODEL)),
        _const_spec((D_ATTN, D_MODEL)), _const_spec((D_ATTN, D_MODEL)),
        _const_spec((D_MODEL, 2 * D_CONV)), _const_spec((D_MODEL, D_CONV)),
        _const_spec((F_ROWS, 1)), _const_spec((tm, tm)),
        _const_spec((HEAD_DIM, tm)), _const_spec((HEAD_DIM, tm)),
        _const_spec((SUBLANES, D_CONV)), _const_spec((1, D_CONV)),
    ]
    out_shape = [
        jax.ShapeDtypeStruct((nb, seq, D_MODEL), F32),
        jax.ShapeDtypeStruct((nb, HEADS, AUG_DIM, seq), BF16),
        jax.ShapeDtypeStruct((nb, HEADS, seq, AUG_DIM), BF16),
        jax.ShapeDtypeStruct((nb, HEADS, HEAD_DIM, seq), BF16),
        jax.ShapeDtypeStruct((nb, HEADS, seq), F32),
        jax.ShapeDtypeStruct((nb, seq, D_CONV), BF16),
        jax.ShapeDtypeStruct((HEADS, N_META, AUG_DIM), BF16),
        jax.ShapeDtypeStruct((HEADS, HEAD_DIM, N_META), BF16),
        jax.ShapeDtypeStruct((F_ROWS, LANES), F32),
    ]
    out_specs = [
        tile3(D_MODEL),
        pl.BlockSpec((1, HEADS, AUG_DIM, tm), lambda b, t: (b, 0, 0, t)),
        pl.BlockSpec((1, HEADS, tm, AUG_DIM), lambda b, t: (b, 0, t, 0)),
        pl.BlockSpec((1, HEADS, HEAD_DIM, tm), lambda b, t: (b, 0, 0, t)),
        pl.BlockSpec((1, HEADS, tm), lambda b, t: (b, 0, t)),
        tile3(D_CONV),
        pl.BlockSpec((HEADS, N_META, AUG_DIM), lambda b, t: (0, 0, 0)),
        pl.BlockSpec((HEADS, HEAD_DIM, N_META), lambda b, t: (0, 0, 0)),
        pl.BlockSpec((F_ROWS, LANES), lambda b, t: (0, 0)),
    ]
    scratch_shapes = [
        pltpu.VMEM((tm, D_MODEL), BF16),
        pltpu.VMEM((tm, D_MODEL), F32),
        pltpu.VMEM((SUBLANES, D_CONV), F32),
        pltpu.VMEM((F_ROWS, LANES), F32),
        pltpu.VMEM((SUBLANES, D_CONV), F32),
        pltpu.VMEM((F_ROWS, LANES), F32),
    ] + _ffn_weight_scratch()
    return pl.pallas_call(
        _ffn1_mix_in_kernel, out_shape=out_shape, grid=(nb, nt), in_specs=in_specs, out_specs=out_specs,
        scratch_shapes=scratch_shapes, name="ffn1_mix_in",
        compiler_params=pltpu.CompilerParams(
            dimension_semantics=("arbitrary", "arbitrary"), vmem_limit_bytes=VMEM_LIMIT_BYTES),
    )(x, meta,
      p["g1"], p["win1"], p["wout1"],
      p["gmix"], p["wfkT"], p["wqT"], p["wvT"], p["wcu"], p["wb"], p["bf"], tri,
      gq_t, gk_t, p["cw"], p["gconv"])


def _fox_attention_kernel(fq0_ref, flast_ref, fmeta_ref, cb_ref,
                          qT_ref, k_ref, vT_ref, km_ref, vmT_ref,
                          o_ref,
                          acc_ref, l_ref):
    bh0 = pl.program_id(0) * HEADS + pl.program_id(1) * HEAD_GROUP
    heads = range(HEAD_GROUP)
    nq = qT_ref.shape[3] // Q_BLOCK
    c_bound = cb_ref[0]
    key_idx = lax.broadcasted_iota(jnp.int32, (K_BLOCK, Q_BLOCK), 0)
    qry_idx = lax.broadcasted_iota(jnp.int32, (K_BLOCK, Q_BLOCK), 1)
    causal = key_idx <= qry_idx

    def block_live(g, i, j):
        return fq0_ref[bh0 + g, i] - flast_ref[bh0 + g, jnp.maximum(j, 0)] + c_bound >= EXP_UNDERFLOW

    def meta_live(g, i):
        head = pl.program_id(1) * HEAD_GROUP + g
        return fq0_ref[bh0 + g, i] - fmeta_ref[head] + c_bound >= EXP_UNDERFLOW

    def sublane_partial_sum(p):
        return jnp.sum(p.reshape(p.shape[0] // SUBLANES, SUBLANES, p.shape[1]), axis=0)

    def bounded_tile(k_blk, v_t, q_t):
        p = jnp.exp2(_dot(k_blk, q_t))
        return sublane_partial_sum(p), _dot(v_t, p.astype(BF16))

    def bounded_q_blocks(i_first):
        chains = [(r, g) for r in range(Q_PER_STEP) for g in heads]
        blk = [i_first + r for r in range(Q_PER_STEP)]
        q0 = [pl.multiple_of(i * Q_BLOCK, Q_BLOCK) for i in blk]
        p0 = [pl.multiple_of(jnp.maximum(i - 1, 0) * K_BLOCK, K_BLOCK) for i in blk]
        first_mask = jnp.where(i_first >= 1, 0.0, MASKED)
        q_t = [qT_ref[0, g, :, pl.ds(q0[r], Q_BLOCK)] for r, g in chains]
        s_d = [_dot(k_ref[0, g, pl.ds(q0[r], K_BLOCK), :], q_t[c]) for c, (r, g) in enumerate(chains)]
        s_p = [_dot(k_ref[0, g, pl.ds(p0[r], K_BLOCK), :], q_t[c]) for c, (r, g) in enumerate(chains)]
        p_d = [jnp.exp2(jnp.where(causal, s, MASKED)) for s in s_d]
        p_p = [jnp.exp2(s + first_mask if r == 0 else s) for s, (r, g) in zip(s_p, chains)]
        for c, (r, g) in enumerate(chains):
            l_ref[c] = sublane_partial_sum(p_d[c]) + sublane_partial_sum(p_p[c])
            acc_ref[c] = (_dot(vT_ref[0, g, :, pl.ds(q0[r], K_BLOCK)], p_d[c].astype(BF16))
                          + _dot(vT_ref[0, g, :, pl.ds(p0[r], K_BLOCK)], p_p[c].astype(BF16)))

        for r in range(Q_PER_STEP):
            i = blk[r]

            def cond(j, i=i):
                live = block_live(0, i, j)
                for g in heads[1:]:
                    live = jnp.logical_or(live, block_live(g, i, j))
                return jnp.logical_and(j >= 0, live)

            def body(j, r=r):
                k0 = pl.multiple_of(j * K_BLOCK, K_BLOCK)
                for g in heads:
                    c = r * HEAD_GROUP + g
                    l_j, acc_j = bounded_tile(k_ref[0, g, pl.ds(k0, K_BLOCK), :],
                                              vT_ref[0, g, :, pl.ds(k0, K_BLOCK)],
                                              qT_ref[0, g, :, pl.ds(q0[r], Q_BLOCK)])
                    l_ref[c] += l_j
                    acc_ref[c] += acc_j
                return j - 1

            lax.while_loop(cond, body, i - 2)

            any_meta_live = meta_live(0, i)
            for g in heads[1:]:
                any_meta_live = jnp.logical_or(any_meta_live, meta_live(g, i))

            @pl.when(any_meta_live)
            def _(r=r):
                for g in heads:
                    c = r * HEAD_GROUP + g
                    l_m, acc_m = bounded_tile(km_ref[g], vmT_ref[g], qT_ref[0, g, :, pl.ds(q0[r], Q_BLOCK)])
                    l_ref[c] += l_m
                    acc_ref[c] += acc_m

        for c, (r, g) in enumerate(chains):
            l_tot = jnp.sum(l_ref[c], axis=0, keepdims=True)
            o_ref[0, g, :, pl.ds(q0[r], Q_BLOCK)] = (acc_ref[c] / l_tot).astype(o_ref.dtype)

    @pl.when(c_bound <= BOUNDED_LOGIT_MAX)
    def _():
        def step(n, carry):
            bounded_q_blocks(n * Q_PER_STEP)
            return carry

        lax.fori_loop(0, nq // Q_PER_STEP, step, 0)

    def online_q_block(g, i):
        q0 = pl.multiple_of(i * Q_BLOCK, Q_BLOCK)
        q_t = qT_ref[0, g, :, pl.ds(q0, Q_BLOCK)]
        s_m = _dot(km_ref[g], q_t)
        m = jnp.max(s_m, axis=0, keepdims=True)
        p_m = jnp.exp2(s_m - m)
        l = jnp.sum(p_m, axis=0, keepdims=True)
        acc = _dot(vmT_ref[g], p_m.astype(BF16))

        def online_step(state, k0, masked):
            m, l, acc = state
            s = _dot(k_ref[0, g, pl.ds(k0, K_BLOCK), :], q_t)
            if masked:
                s = jnp.where(causal, s, MASKED)
            m_new = jnp.maximum(m, jnp.max(s, axis=0, keepdims=True))
            alpha = jnp.exp2(m - m_new)
            p = jnp.exp2(s - m_new)
            l = alpha * l + jnp.sum(p, axis=0, keepdims=True)
            acc = alpha * acc + _dot(vT_ref[0, g, :, pl.ds(k0, K_BLOCK)], p.astype(BF16))
            return m_new, l, acc

        state = online_step((m, l, acc), q0, True)

        def cond(carry):
            return jnp.logical_and(carry[0] >= 0, block_live(g, i, carry[0]))

        def body(carry):
            j = carry[0]
            return (j - 1,) + online_step(carry[1:], pl.multiple_of(j * K_BLOCK, K_BLOCK), False)

        _, m, l, acc = lax.while_loop(cond, body, (i - 1,) + state)
        o_ref[0, g, :, pl.ds(q0, Q_BLOCK)] = (acc / l).astype(o_ref.dtype)

    @pl.when(c_bound > BOUNDED_LOGIT_MAX)
    def _():
        def head_loop(g, carry):
            def step(i, inner):
                online_q_block(g, i)
                return inner

            lax.fori_loop(0, nq, step, 0)
            return carry

        lax.fori_loop(0, HEAD_GROUP, head_loop, 0)


def _fox_attention(qT, k, vT, k_meta, vT_meta, fq0, flast, fmeta, c_bound):
    nb, _, _, seq = qT.shape
    assert seq % (Q_PER_STEP * Q_BLOCK) == 0 and Q_BLOCK == K_BLOCK
    smem = pl.BlockSpec(memory_space=pltpu.SMEM)
    hg = HEAD_GROUP
    return pl.pallas_call(
        _fox_attention_kernel,
        out_shape=jax.ShapeDtypeStruct((nb, HEADS, HEAD_DIM, seq), BF16),
        grid=(nb, HEADS // hg),
        in_specs=[
            smem, smem, smem, smem,
            pl.BlockSpec((1, hg, AUG_DIM, seq), lambda b, h: (b, h, 0, 0)),
            pl.BlockSpec((1, hg, seq, AUG_DIM), lambda b, h: (b, h, 0, 0)),
            pl.BlockSpec((1, hg, HEAD_DIM, seq), lambda b, h: (b, h, 0, 0)),
            pl.BlockSpec((hg, N_META, AUG_DIM), lambda b, h: (h, 0, 0)),
            pl.BlockSpec((hg, HEAD_DIM, N_META), lambda b, h: (h, 0, 0)),
        ],
        out_specs=pl.BlockSpec((1, hg, HEAD_DIM, seq), lambda b, h: (b, h, 0, 0)),
        scratch_shapes=[pltpu.VMEM((Q_PER_STEP * hg, HEAD_DIM, Q_BLOCK), F32),
                        pltpu.VMEM((Q_PER_STEP * hg, SUBLANES, Q_BLOCK), F32)],
        name="fox_attention",
        compiler_params=pltpu.CompilerParams(
            dimension_semantics=("arbitrary", "arbitrary"), vmem_limit_bytes=VMEM_LIMIT_BYTES),
    )(fq0, flast, fmeta, c_bound, qT, k, vT, k_meta, vT_meta)


def _mix_out_ffn2_kernel(h1_ref, oT_ref, oconv_ref,
                         gattn_ref, woa_ref, woc_ref,
                         g2_ref, win_hbm, wout_hbm, gfin_ref,
                         out_ref,
                         xn_ref, acc_ref, win_ref, wout_ref, win_stage, wout_stage, win_sem, wout_sem):
    @pl.when(jnp.logical_and(pl.program_id(0) == 0, pl.program_id(1) == 0))
    def _():
        _load_weight_as_bf16(win_hbm, win_ref, win_stage, win_sem)
        _load_weight_as_bf16(wout_hbm, wout_ref, wout_stage, wout_sem, MACARON_SCALE)

    o_t = oT_ref[0].astype(F32)
    ms = jnp.mean(o_t * o_t, axis=0, keepdims=True)
    o_n = (o_t * lax.rsqrt(ms + EPS) * gattn_ref[...]).T.astype(BF16)
    mix = _dot(o_n, woa_ref[...]) + _dot(oconv_ref[0], woc_ref[...])
    h2 = h1_ref[0] + mix
    xn_ref[...] = _rmsnorm_rows(h2, g2_ref[...]).astype(BF16)
    _swiglu(xn_ref, win_ref, wout_ref, acc_ref, h2)
    out_ref[0] = _rmsnorm_rows(acc_ref[...], gfin_ref[...])


def _mix_out_ffn2(h1, o_t, oconv, p, tm):
    nb, seq, _ = h1.shape
    gattn_t = jnp.broadcast_to(p["gattn"][:, None], (D_ATTN, tm))
    in_specs = [
        pl.BlockSpec((1, tm, D_MODEL), lambda b, t: (b, t, 0)),
        pl.BlockSpec((1, D_ATTN, tm), lambda b, t: (b, 0, t)),
        pl.BlockSpec((1, tm, D_CONV), lambda b, t: (b, t, 0)),
        _const_spec((D_ATTN, tm)), _const_spec((D_ATTN, D_MODEL)), _const_spec((D_CONV, D_MODEL)),
        _const_spec((1, D_MODEL)),
        pl.BlockSpec(memory_space=pl.ANY), pl.BlockSpec(memory_space=pl.ANY),
        _const_spec((1, D_MODEL)),
    ]
    return pl.pallas_call(
        _mix_out_ffn2_kernel,
        out_shape=jax.ShapeDtypeStruct((nb, seq, D_MODEL), F32),
        grid=(nb, seq // tm),
        in_specs=in_specs,
        out_specs=pl.BlockSpec((1, tm, D_MODEL), lambda b, t: (b, t, 0)),
        scratch_shapes=[pltpu.VMEM((tm, D_MODEL), BF16), pltpu.VMEM((tm, D_MODEL), F32)] + _ffn_weight_scratch(),
        name="mix_out_ffn2",
        compiler_params=pltpu.CompilerParams(
            dimension_semantics=("arbitrary", "arbitrary"), vmem_limit_bytes=VMEM_LIMIT_BYTES),
    )(h1, o_t, oconv, gattn_t, p["woa"], p["woc"], p["g2"], p["win2"], p["wout2"], p["gfin"])


def kernel(x, meta_tokens, ffn1_norm, ffn1_w_in, ffn1_w_out, mix_norm, w_mix_in, b_forget, q_norm, k_norm, conv_w, attn_out_norm, conv_out_norm, w_mix_out, ffn2_norm, ffn2_w_in, ffn2_w_out, final_norm):
    nb, seq, _ = x.shape
    wmix = w_mix_in[0]
    n_qkv = 3 * D_ATTN
    p = {
        "g1": ffn1_norm, "win1": ffn1_w_in[0], "wout1": ffn1_w_out[0],
        "gmix": mix_norm,
        "wfkT": jnp.concatenate([jnp.pad(wmix[:, n_qkv:n_qkv + HEADS].T, ((0, F_ROWS - HEADS), (0, 0))),
                                 wmix[:, D_ATTN:2 * D_ATTN].T], axis=0).astype(BF16),
        "wqT": wmix[:, :D_ATTN].T.astype(BF16),
        "wvT": wmix[:, 2 * D_ATTN:n_qkv].T.astype(BF16),
        "wcu": wmix[:, n_qkv + HEADS + D_CONV:].astype(BF16),
        "wb": wmix[:, n_qkv + HEADS:n_qkv + HEADS + D_CONV].astype(BF16),
        "bf": jnp.pad(b_forget[0], (0, F_ROWS - HEADS))[:, None],
        "gq": q_norm[0], "gk": k_norm[0],
        "cw": jnp.pad(conv_w[0], ((0, SUBLANES - CONV_WIDTH), (0, 0))),
        "gconv": conv_out_norm,
        "gattn": attn_out_norm[0],
        "woa": w_mix_out[0, :D_ATTN].astype(BF16), "woc": w_mix_out[0, D_ATTN:].astype(BF16),
        "g2": ffn2_norm, "win2": ffn2_w_in[0], "wout2": ffn2_w_out[0],
        "gfin": final_norm,
    }

    meta = jnp.pad(meta_tokens.astype(x.dtype), ((0, META_TILE - N_META), (0, 0)))
    h1, qT, k, vT, f_cum, oconv, k_meta, vT_meta, f_meta = _ffn1_mix_in(x, meta, p, TOKEN_TILE)

    fq0 = f_cum[:, :, 0::Q_BLOCK].reshape(nb * HEADS, seq // Q_BLOCK)
    flast = f_cum[:, :, K_BLOCK - 1::K_BLOCK].reshape(nb * HEADS, seq // K_BLOCK)
    c_bound = (16.0 * 1.02 * jnp.max(jnp.abs(q_norm)) * jnp.max(jnp.abs(k_norm)) + 1.0).reshape(1).astype(F32)

    o_t = _fox_attention(qT, k, vT, k_meta, vT_meta, fq0, flast, f_meta[:HEADS, 0], c_bound)
    o_t = o_t.reshape(nb, D_ATTN, seq)
    return _mix_out_ffn2(h1, o_t, oconv, p, TOKEN_TILE)
```

```python
from typing import Any, NamedTuple

import jax
import jax.numpy as jnp
from jax import lax
from jax.experimental import pallas as pl
from jax.experimental.pallas import tpu as pltpu

D_MODEL = 1024
N_META = 16
D_ATTN = 512
D_CONV = 512
HEADS = 8
HEAD_DIM = 64
CONV_WIDTH = 3
D_FF = 2816
EPS = 1e-6

F32 = jnp.float32
BF16 = jnp.bfloat16

LANES = 128
SUBLANES = 8
MXU_DIM = 256
VMEM_LIMIT_BYTES = 60000 * 1024

TOKEN_TILE = 512
META_TILE = LANES
FF_CHUNK = MXU_DIM
N_FF_CHUNKS = D_FF // FF_CHUNK
Q_BLOCK = MXU_DIM
K_BLOCK = MXU_DIM
AUG_DIM = LANES
HEAD_GROUP = 4
Q_PER_STEP = 4
MASKED = -1e30
LOG2E = 1.4426950408889634
F_ROWS = 2 * SUBLANES
MACARON_SCALE = 0.5
WIN_STAGE_ROWS = 64
WOUT_STAGE_ROWS = 256
STAGE_SLOTS = 4

EXP_UNDERFLOW = -104.0
BOUNDED_LOGIT_MAX = 120.0

assert D_FF % FF_CHUNK == 0 and FF_CHUNK % LANES == 0 and HEADS % HEAD_GROUP == 0


def _dot(a, b):
    return jnp.dot(a, b, preferred_element_type=F32)


def _dot_nt(a, b):
    return lax.dot_general(a, b, (((1,), (1,)), ((), ())), preferred_element_type=F32)


def _rmsnorm_rows(x, gain):
    ms = jnp.mean(x * x, axis=-1, keepdims=True)
    return x * lax.rsqrt(ms + EPS) * gain


def _split3(x):
    hi = x.astype(BF16)
    r1 = x - hi.astype(F32)
    mid = r1.astype(BF16)
    lo = (r1 - mid.astype(F32)).astype(BF16)
    return hi, mid, lo


def _load_weight_as_bf16(w_hbm, w_bf, stage, sem, scale=None):
    n_slots, rows = stage.shape[0], stage.shape[1]
    n_chunks = w_hbm.shape[0] // rows
    assert n_chunks * rows == w_hbm.shape[0]

    def chunk_copy(i):
        slot = i % n_slots
        return pltpu.make_async_copy(w_hbm.at[pl.ds(i * rows, rows)], stage.at[slot], sem.at[slot])

    for i in range(min(n_slots - 1, n_chunks)):
        chunk_copy(i).start()
    for i in range(n_chunks):
        if i + n_slots - 1 < n_chunks:
            chunk_copy(i + n_slots - 1).start()
        chunk_copy(i).wait()
        w = stage[i % n_slots]
        w_bf[i * rows:(i + 1) * rows, :] = (w if scale is None else w * scale).astype(BF16)


def _swiglu(xn_ref, win_ref, wout_ref, acc_ref, residual):
    acc_ref[...] = residual
    for c in range(N_FF_CHUNKS):
        lo, hi = c * FF_CHUNK, (c + 1) * FF_CHUNK
        xn = xn_ref[...]
        g = _dot(xn, win_ref[:, lo:hi])
        u = _dot(xn, win_ref[:, D_FF + lo:D_FF + hi])
        a = (g * jax.nn.sigmoid(g) * u).astype(BF16)
        acc_ref[...] += _dot(a, wout_ref[lo:hi, :])


class _MixWeights(NamedTuple):
    g1: Any
    win: Any
    wout: Any
    gmix: Any
    wfkT: Any
    wqT: Any
    wvT: Any
    wcu: Any
    wb: Any
    bf: Any
    tri: Any
    gq: Any
    gk: Any
    cw: Any
    gconv: Any


class _TileOut(NamedTuple):
    h1: Any
    q: Any
    k: Any
    v: Any
    f: Any
    oconv: Any


def _token_tile(x, n_valid, w, xn_ref, acc_ref, zc_ref, fc_ref, out):
    tm = x.shape[0]

    xn_ref[...] = _rmsnorm_rows(x, w.g1[...]).astype(BF16)
    _swiglu(xn_ref, w.win, w.wout, acc_ref, x)
    h1 = acc_ref[...]
    if out.h1 is not None:
        out.h1[0] = h1

    xn2 = _rmsnorm_rows(h1, w.gmix[...]).astype(BF16)
    fkT = _dot_nt(w.wfkT[...], xn2)
    cu = _dot(xn2, w.wcu[...])

    fl = fkT[0:F_ROWS] + w.bf[...]
    logf = jnp.minimum(fl, 0.0) - jnp.log(1.0 + jnp.exp(-jnp.abs(fl)))
    if n_valid < tm:
        logf = jnp.where(lax.broadcasted_iota(jnp.int32, logf.shape, 1) < n_valid, logf, 0.0)
    pieces = _dot(jnp.concatenate(_split3(logf), axis=0), w.tri[0:tm, 0:tm])
    csum = pieces[0:F_ROWS] + pieces[F_ROWS:2 * F_ROWS] + pieces[2 * F_ROWS:3 * F_ROWS]
    f_all = csum + jnp.concatenate([fc_ref[...]] * (tm // LANES), axis=1)
    fc_ref[...] = jnp.broadcast_to(f_all[:, tm - 1:tm], fc_ref.shape)
    if out.f is not None:
        out.f[0] = f_all[:HEADS]

    f_hi, f_mid, f_lo = (p.astype(F32) for p in _split3(f_all * LOG2E))
    row = lax.broadcasted_iota(jnp.int32, (SUBLANES, tm), 0)
    ones_mid = jnp.where(row < 6, 1.0, 0.0)
    pad_rows = jnp.zeros((AUG_DIM - HEAD_DIM - SUBLANES, tm), F32)

    def head_pieces(h):
        return (jnp.broadcast_to(f_hi[h:h + 1], (SUBLANES, tm)),
                jnp.broadcast_to(f_mid[h:h + 1], (SUBLANES, tm)),
                jnp.broadcast_to(f_lo[h:h + 1], (SUBLANES, tm)))

    def head_rmsnorm(x_t, gain_t):
        x3 = x_t.reshape(HEADS, HEAD_DIM, tm)
        return x3 * lax.rsqrt(jnp.mean(x3 * x3, axis=1, keepdims=True) + EPS) * gain_t[None]

    kn = head_rmsnorm(fkT[F_ROWS:F_ROWS + D_ATTN], w.gk[:, 0:tm])
    if out.q is not None:
        qT = _dot_nt(w.wqT[...], xn2)
    if out.oconv is not None:
        gate_b = _dot(xn2, w.wb[...])
    for h in range(HEADS):
        fh, fm, fo = head_pieces(h)
        aug_k = jnp.where(row == 3, -fh, jnp.where(row == 4, -fm, jnp.where(row == 5, -fo, ones_mid)))
        k_aug_t = jnp.concatenate([kn[h], aug_k, pad_rows], axis=0)
        out.k(h, k_aug_t.T.astype(BF16))

    vT = _dot_nt(w.wvT[...], xn2)
    z = cu[:, 0:D_CONV] * cu[:, D_CONV:2 * D_CONV]
    if out.oconv is not None:
        zc = zc_ref[...]
        rowz = lax.broadcasted_iota(jnp.int32, z.shape, 0)
        prev1 = jnp.broadcast_to(zc[7:8], z.shape)
        prev2 = jnp.broadcast_to(zc[6:7], z.shape)
        z1 = jnp.where(rowz == 0, prev1, pltpu.roll(z, 1, axis=0))
        z2 = jnp.where(rowz == 0, prev2, jnp.where(rowz == 1, prev1, pltpu.roll(z, 2, axis=0)))
        cw = w.cw[...]
        y = cw[0:1] * z2 + cw[1:2] * z1 + cw[2:3] * z
        out.oconv[0] = _rmsnorm_rows(gate_b * y, w.gconv[...]).astype(BF16)
    zc_ref[...] = z[n_valid - SUBLANES:n_valid]

    if out.q is not None:
        qn = head_rmsnorm(qT, w.gq[:, 0:tm] * (HEAD_DIM ** -0.5 * LOG2E))
    v3 = vT.reshape(HEADS, HEAD_DIM, tm)
    for h in range(HEADS):
        if out.q is not None:
            fh, fm, fo = head_pieces(h)
            aug_q = jnp.where(row == 0, fh, jnp.where(row == 1, fm, jnp.where(row == 2, fo, ones_mid)))
            out.q[0, h] = jnp.concatenate([qn[h], aug_q, pad_rows], axis=0).astype(BF16)
        out.v(h, v3[h].astype(BF16))


def _ffn1_mix_in_kernel(
        x_ref, meta_ref,
        g1_ref, win_hbm, wout_hbm,
        gmix_ref, wfkT_ref, wqT_ref, wvT_ref, wcu_ref, wb_ref, bf_ref, tri_ref,
        gq_ref, gk_ref, cw_ref, gconv_ref,
        h1_ref, qT_ref, k_ref, vT_ref, f_ref, oconv_ref, km_ref, vmT_ref, fmeta_ref,
        xn_ref, acc_ref, zc_ref, fc_ref, zc0_ref, fc0_ref,
        win_ref, wout_ref, win_stage, wout_stage, win_sem, wout_sem):
    t = pl.program_id(1)
    w = _MixWeights(g1_ref, win_ref, wout_ref, gmix_ref, wfkT_ref, wqT_ref, wvT_ref, wcu_ref, wb_ref,
                    bf_ref, tri_ref, gq_ref, gk_ref, cw_ref, gconv_ref)

    def put_k(h, k_aug):
        k_ref[0, h] = k_aug

    def put_v(h, v_t):
        vT_ref[0, h] = v_t

    def put_meta_k(h, k_aug):
        km_ref[h] = k_aug[0:N_META]

    def put_meta_v(h, v_t):
        vmT_ref[h] = v_t[:, 0:N_META]

    @pl.when(jnp.logical_and(pl.program_id(0) == 0, t == 0))
    def _():
        _load_weight_as_bf16(win_hbm, win_ref, win_stage, win_sem)
        _load_weight_as_bf16(wout_hbm, wout_ref, wout_stage, wout_sem, MACARON_SCALE)
        rows = meta_ref.shape[0]
        zc_ref[...] = jnp.zeros_like(zc_ref)
        fc_ref[...] = jnp.zeros_like(fc_ref)
        _token_tile(meta_ref[...], N_META, w, xn_ref.at[pl.ds(0, rows)], acc_ref.at[pl.ds(0, rows)],
                    zc_ref, fc_ref, _TileOut(None, None, put_meta_k, put_meta_v, None, None))
        zc0_ref[...] = zc_ref[...]
        fc0_ref[...] = fc_ref[...]
        fmeta_ref[...] = fc_ref[...]

    @pl.when(t == 0)
    def _():
        zc_ref[...] = zc0_ref[...]
        fc_ref[...] = fc0_ref[...]

    _token_tile(x_ref[0], x_ref.shape[1], w, xn_ref, acc_ref, zc_ref, fc_ref,
                _TileOut(h1_ref, qT_ref, put_k, put_v, f_ref, oconv_ref))


def _ffn_weight_scratch():
    return [pltpu.VMEM((D_MODEL, 2 * D_FF), BF16), pltpu.VMEM((D_FF, D_MODEL), BF16),
            pltpu.VMEM((STAGE_SLOTS, WIN_STAGE_ROWS, 2 * D_FF), F32),
            pltpu.VMEM((STAGE_SLOTS, WOUT_STAGE_ROWS, D_MODEL), F32),
            pltpu.SemaphoreType.DMA((STAGE_SLOTS,)), pltpu.SemaphoreType.DMA((STAGE_SLOTS,))]


def _const_spec(shape):
    nd = len(shape)
    return pl.BlockSpec(shape, lambda *_: (0,) * nd, pipeline_mode=pl.Buffered(1))


def _ffn1_mix_in(x, meta, p, tm):
    nb, seq, _ = x.shape
    nt = seq // tm
    assert tm % LANES == 0 and meta.shape[0] <= tm
    tri = jnp.triu(jnp.ones((tm, tm), BF16))
    gq_t = jnp.broadcast_to(p["gq"][:, None], (HEAD_DIM, tm))
    gk_t = jnp.broadcast_to(p["gk"][:, None], (HEAD_DIM, tm))

    def tile3(last):
        return pl.BlockSpec((1, tm, last), lambda b, t: (b, t, 0))

    in_specs = [
        tile3(D_MODEL), _const_spec(meta.shape),
        _const_spec((1, D_MODEL)),
        pl.BlockSpec(memory_space=pl.ANY), pl.BlockSpec(memory_space=pl.ANY),
        _const_spec((1, D_MODEL)), _const_spec((F_ROWS + D_ATTN, D_MODEL)),
        _const_spec((D_ATTN, D_MODEL)), _const_spec((D_ATTN, D_MODEL)),
        _const_spec((D_MODEL, 2 * D_CONV)), _const_spec((D_MODEL, D_CONV)),
        _const_spec((F_ROWS, 1)), _const_spec((tm, tm)),
        _const_spec((HEAD_DIM, tm)), _const_spec((HEAD_DIM, tm)),
        _const_spec((SUBLANES, D_CONV)), _const_spec((1, D_CONV)),
    ]
    out_shape = [
        jax.ShapeDtypeStruct((nb, seq, D_MODEL), F32),
        jax.ShapeDtypeStruct((nb, HEADS, AUG_DIM, seq), BF16),
        jax.ShapeDtypeStruct((nb, HEADS, seq, AUG_DIM), BF16),
        jax.ShapeDtypeStruct((nb, HEADS, HEAD_DIM, seq), BF16),
        jax.ShapeDtypeStruct((nb, HEADS, seq), F32),
        jax.ShapeDtypeStruct((nb, seq, D_CONV), BF16),
        jax.ShapeDtypeStruct((HEADS, N_META, AUG_DIM), BF16),
        jax.ShapeDtypeStruct((HEADS, HEAD_DIM, N_META), BF16),
        jax.ShapeDtypeStruct((F_ROWS, LANES), F32),
    ]
    out_specs = [
        tile3(D_MODEL),
        pl.BlockSpec((1, HEADS, AUG_DIM, tm), lambda b, t: (b, 0, 0, t)),
        pl.BlockSpec((1, HEADS, tm, AUG_DIM), lambda b, t: (b, 0, t, 0)),
        pl.BlockSpec((1, HEADS, HEAD_DIM, tm), lambda b, t: (b, 0, 0, t)),
        pl.BlockSpec((1, HEADS, tm), lambda b, t: (b, 0, t)),
        tile3(D_CONV),
        pl.BlockSpec((HEADS, N_META, AUG_DIM), lambda b, t: (0, 0, 0)),
        pl.BlockSpec((HEADS, HEAD_DIM, N_META), lambda b, t: (0, 0, 0)),
        pl.BlockSpec((F_ROWS, LANES), lambda b, t: (0, 0)),
    ]
    scratch_shapes = [
        pltpu.VMEM((tm, D_MODEL), BF16),
        pltpu.VMEM((tm, D_MODEL), F32),
        pltpu.VMEM((SUBLANES, D_CONV), F32),
        pltpu.VMEM((F_ROWS, LANES), F32),
        pltpu.VMEM((SUBLANES, D_CONV), F32),
        pltpu.VMEM((F_ROWS, LANES), F32),
    ] + _ffn_weight_scratch()
    return pl.pallas_call(
        _ffn1_mix_in_kernel, out_shape=out_shape, grid=(nb, nt), in_specs=in_specs, out_specs=out_specs,
        scratch_shapes=scratch_shapes, name="ffn1_mix_in",
        compiler_params=pltpu.CompilerParams(
            dimension_semantics=("arbitrary", "arbitrary"), vmem_limit_bytes=VMEM_LIMIT_BYTES),
    )(x, meta,
      p["g1"], p["win1"], p["wout1"],
      p["gmix"], p["wfkT"], p["wqT"], p["wvT"], p["wcu"], p["wb"], p["bf"], tri,
      gq_t, gk_t, p["cw"], p["gconv"])


def _fox_attention_kernel(fq0_ref, flast_ref, fmeta_ref, cb_ref,
                          qT_ref, k_ref, vT_ref, km_ref, vmT_ref,
                          o_ref,
                          acc_ref, l_ref):
    bh0 = pl.program_id(0) * HEADS + pl.program_id(1) * HEAD_GROUP
    heads = range(HEAD_GROUP)
    nq = qT_ref.shape[3] // Q_BLOCK
    c_bound = cb_ref[0]
    key_idx = lax.broadcasted_iota(jnp.int32, (K_BLOCK, Q_BLOCK), 0)
    qry_idx = lax.broadcasted_iota(jnp.int32, (K_BLOCK, Q_BLOCK), 1)
    causal = key_idx <= qry_idx

    def block_live(g, i, j):
        return fq0_ref[bh0 + g, i] - flast_ref[bh0 + g, jnp.maximum(j, 0)] + c_bound >= EXP_UNDERFLOW

    def meta_live(g, i):
        head = pl.program_id(1) * HEAD_GROUP + g
        return fq0_ref[bh0 + g, i] - fmeta_ref[head] + c_bound >= EXP_UNDERFLOW

    def sublane_partial_sum(p):
        return jnp.sum(p.reshape(p.shape[0] // SUBLANES, SUBLANES, p.shape[1]), axis=0)

    def bounded_tile(k_blk, v_t, q_t):
        p = jnp.exp2(_dot(k_blk, q_t))
        return sublane_partial_sum(p), _dot(v_t, p.astype(BF16))

    def bounded_q_blocks(i_first):
        chains = [(r, g) for r in range(Q_PER_STEP) for g in heads]
        blk = [i_first + r for r in range(Q_PER_STEP)]
        q0 = [pl.multiple_of(i * Q_BLOCK, Q_BLOCK) for i in blk]
        p0 = [pl.multiple_of(jnp.maximum(i - 1, 0) * K_BLOCK, K_BLOCK) for i in blk]
        first_mask = jnp.where(i_first >= 1, 0.0, MASKED)
        q_t = [qT_ref[0, g, :, pl.ds(q0[r], Q_BLOCK)] for r, g in chains]
        s_d = [_dot(k_ref[0, g, pl.ds(q0[r], K_BLOCK), :], q_t[c]) for c, (r, g) in enumerate(chains)]
        s_p = [_dot(k_ref[0, g, pl.ds(p0[r], K_BLOCK), :], q_t[c]) for c, (r, g) in enumerate(chains)]
        p_d = [jnp.exp2(jnp.where(causal, s, MASKED)) for s in s_d]
        p_p = [jnp.exp2(s + first_mask if r == 0 else s) for s, (r, g) in zip(s_p, chains)]
        for c, (r, g) in enumerate(chains):
            l_ref[c] = sublane_partial_sum(p_d[c]) + sublane_partial_sum(p_p[c])
            acc_ref[c] = (_dot(vT_ref[0, g, :, pl.ds(q0[r], K_BLOCK)], p_d[c].astype(BF16))
                          + _dot(vT_ref[0, g, :, pl.ds(p0[r], K_BLOCK)], p_p[c].astype(BF16)))

        def any_head(test):
            hit = test(0)
            for g in heads[1:]:
                hit = jnp.logical_or(hit, test(g))
            return hit

        def walk(r):
            i = blk[r]

            def cond(j):
                return jnp.logical_and(j >= 0, any_head(lambda g: block_live(g, i, j)))

            def body(j):
                k0 = pl.multiple_of(j * K_BLOCK, K_BLOCK)
                for g in heads:
                    c = r * HEAD_GROUP + g
                    l_j, acc_j = bounded_tile(k_ref[0, g, pl.ds(k0, K_BLOCK), :],
                                              vT_ref[0, g, :, pl.ds(k0, K_BLOCK)],
                                              qT_ref[0, g, :, pl.ds(q0[r], Q_BLOCK)])
                    l_ref[c] += l_j
                    acc_ref[c] += acc_j
                return j - 1

            lax.while_loop(cond, body, i - 2)

            @pl.when(any_head(lambda g: meta_live(g, i)))
            def _():
                for g in heads:
                    c = r * HEAD_GROUP + g
                    l_m, acc_m = bounded_tile(km_ref[g], vmT_ref[g], qT_ref[0, g, :, pl.ds(q0[r], Q_BLOCK)])
                    l_ref[c] += l_m
                    acc_ref[c] += acc_m

        walks = [jnp.logical_or(jnp.logical_and(i >= 2, any_head(lambda g, i=i: block_live(g, i, i - 2))),
                                any_head(lambda g, i=i: meta_live(g, i))) for i in blk]
        any_walk = walks[0]
        for hit in walks[1:]:
            any_walk = jnp.logical_or(any_walk, hit)

        @pl.when(any_walk)
        def _():
            for r in range(Q_PER_STEP):
                walk(r)

        for c, (r, g) in enumerate(chains):
            l_tot = jnp.sum(l_ref[c], axis=0, keepdims=True)
            o_ref[0, g, :, pl.ds(q0[r], Q_BLOCK)] = (acc_ref[c] / l_tot).astype(o_ref.dtype)

    @pl.when(c_bound <= BOUNDED_LOGIT_MAX)
    def _():
        def step(n, carry):
            bounded_q_blocks(n * Q_PER_STEP)
            return carry

        lax.fori_loop(0, nq // Q_PER_STEP, step, 0)

    def online_q_block(g, i):
        q0 = pl.multiple_of(i * Q_BLOCK, Q_BLOCK)
        q_t = qT_ref[0, g, :, pl.ds(q0, Q_BLOCK)]
        s_m = _dot(km_ref[g], q_t)
        m = jnp.max(s_m, axis=0, keepdims=True)
        p_m = jnp.exp2(s_m - m)
        l = jnp.sum(p_m, axis=0, keepdims=True)
        acc = _dot(vmT_ref[g], p_m.astype(BF16))

        def online_step(state, k0, masked):
            m, l, acc = state
            s = _dot(k_ref[0, g, pl.ds(k0, K_BLOCK), :], q_t)
            if masked:
                s = jnp.where(causal, s, MASKED)
            m_new = jnp.maximum(m, jnp.max(s, axis=0, keepdims=True))
            alpha = jnp.exp2(m - m_new)
            p = jnp.exp2(s - m_new)
            l = alpha * l + jnp.sum(p, axis=0, keepdims=True)
            acc = alpha * acc + _dot(vT_ref[0, g, :, pl.ds(k0, K_BLOCK)], p.astype(BF16))
            return m_new, l, acc

        state = online_step((m, l, acc), q0, True)

        def cond(carry):
            return jnp.logical_and(carry[0] >= 0, block_live(g, i, carry[0]))

        def body(carry):
            j = carry[0]
            return (j - 1,) + online_step(carry[1:], pl.multiple_of(j * K_BLOCK, K_BLOCK), False)

        _, m, l, acc = lax.while_loop(cond, body, (i - 1,) + state)
        o_ref[0, g, :, pl.ds(q0, Q_BLOCK)] = (acc / l).astype(o_ref.dtype)

    @pl.when(c_bound > BOUNDED_LOGIT_MAX)
    def _():
        def head_loop(g, carry):
            def step(i, inner):
                online_q_block(g, i)
                return inner

            lax.fori_loop(0, nq, step, 0)
            return carry

        lax.fori_loop(0, HEAD_GROUP, head_loop, 0)


def _fox_attention(qT, k, vT, k_meta, vT_meta, fq0, flast, fmeta, c_bound):
    nb, _, _, seq = qT.shape
    assert seq % (Q_PER_STEP * Q_BLOCK) == 0 and Q_BLOCK == K_BLOCK
    smem = pl.BlockSpec(memory_space=pltpu.SMEM)
    hg = HEAD_GROUP
    return pl.pallas_call(
        _fox_attention_kernel,
        out_shape=jax.ShapeDtypeStruct((nb, HEADS, HEAD_DIM, seq), BF16),
        grid=(nb, HEADS // hg),
        in_specs=[
            smem, smem, smem, smem,
            pl.BlockSpec((1, hg, AUG_DIM, seq), lambda b, h: (b, h, 0, 0)),
            pl.BlockSpec((1, hg, seq, AUG_DIM), lambda b, h: (b, h, 0, 0)),
            pl.BlockSpec((1, hg, HEAD_DIM, seq), lambda b, h: (b, h, 0, 0)),
            pl.BlockSpec((hg, N_META, AUG_DIM), lambda b, h: (h, 0, 0)),
            pl.BlockSpec((hg, HEAD_DIM, N_META), lambda b, h: (h, 0, 0)),
        ],
        out_specs=pl.BlockSpec((1, hg, HEAD_DIM, seq), lambda b, h: (b, h, 0, 0)),
        scratch_shapes=[pltpu.VMEM((Q_PER_STEP * hg, HEAD_DIM, Q_BLOCK), F32),
                        pltpu.VMEM((Q_PER_STEP * hg, SUBLANES, Q_BLOCK), F32)],
        name="fox_attention",
        compiler_params=pltpu.CompilerParams(
            dimension_semantics=("arbitrary", "arbitrary"), vmem_limit_bytes=VMEM_LIMIT_BYTES),
    )(fq0, flast, fmeta, c_bound, qT, k, vT, k_meta, vT_meta)


def _mix_out_ffn2_kernel(h1_ref, oT_ref, oconv_ref,
                         gattn_ref, woa_ref, woc_ref,
                         g2_ref, win_hbm, wout_hbm, gfin_ref,
                         out_ref,
                         xn_ref, acc_ref, win_ref, wout_ref, win_stage, wout_stage, win_sem, wout_sem):
    @pl.when(jnp.logical_and(pl.program_id(0) == 0, pl.program_id(1) == 0))
    def _():
        _load_weight_as_bf16(win_hbm, win_ref, win_stage, win_sem)
        _load_weight_as_bf16(wout_hbm, wout_ref, wout_stage, wout_sem, MACARON_SCALE)

    o_t = oT_ref[0].astype(F32)
    ms = jnp.mean(o_t * o_t, axis=0, keepdims=True)
    o_n = (o_t * lax.rsqrt(ms + EPS) * gattn_ref[...]).T.astype(BF16)
    mix = _dot(o_n, woa_ref[...]) + _dot(oconv_ref[0], woc_ref[...])
    h2 = h1_ref[0] + mix
    xn_ref[...] = _rmsnorm_rows(h2, g2_ref[...]).astype(BF16)
    _swiglu(xn_ref, win_ref, wout_ref, acc_ref, h2)
    out_ref[0] = _rmsnorm_rows(acc_ref[...], gfin_ref[...])


def _mix_out_ffn2(h1, o_t, oconv, p, tm):
    nb, seq, _ = h1.shape
    gattn_t = jnp.broadcast_to(p["gattn"][:, None], (D_ATTN, tm))
    in_specs = [
        pl.BlockSpec((1, tm, D_MODEL), lambda b, t: (b, t, 0)),
        pl.BlockSpec((1, D_ATTN, tm), lambda b, t: (b, 0, t)),
        pl.BlockSpec((1, tm, D_CONV), lambda b, t: (b, t, 0)),
        _const_spec((D_ATTN, tm)), _const_spec((D_ATTN, D_MODEL)), _const_spec((D_CONV, D_MODEL)),
        _const_spec((1, D_MODEL)),
        pl.BlockSpec(memory_space=pl.ANY), pl.BlockSpec(memory_space=pl.ANY),
        _const_spec((1, D_MODEL)),
    ]
    return pl.pallas_call(
        _mix_out_ffn2_kernel,
        out_shape=jax.ShapeDtypeStruct((nb, seq, D_MODEL), F32),
        grid=(nb, seq // tm),
        in_specs=in_specs,
        out_specs=pl.BlockSpec((1, tm, D_MODEL), lambda b, t: (b, t, 0)),
        scratch_shapes=[pltpu.VMEM((tm, D_MODEL), BF16), pltpu.VMEM((tm, D_MODEL), F32)] + _ffn_weight_scratch(),
        name="mix_out_ffn2",
        compiler_params=pltpu.CompilerParams(
            dimension_semantics=("arbitrary", "arbitrary"), vmem_limit_bytes=VMEM_LIMIT_BYTES),
    )(h1, o_t, oconv, gattn_t, p["woa"], p["woc"], p["g2"], p["win2"], p["wout2"], p["gfin"])


def kernel(x, meta_tokens, ffn1_norm, ffn1_w_in, ffn1_w_out, mix_norm, w_mix_in, b_forget, q_norm, k_norm, conv_w, attn_out_norm, conv_out_norm, w_mix_out, ffn2_norm, ffn2_w_in, ffn2_w_out, final_norm):
    nb, seq, _ = x.shape
    wmix = w_mix_in[0]
    n_qkv = 3 * D_ATTN
    p = {
        "g1": ffn1_norm, "win1": ffn1_w_in[0], "wout1": ffn1_w_out[0],
        "gmix": mix_norm,
        "wfkT": jnp.concatenate([jnp.pad(wmix[:, n_qkv:n_qkv + HEADS].T, ((0, F_ROWS - HEADS), (0, 0))),
                                 wmix[:, D_ATTN:2 * D_ATTN].T], axis=0).astype(BF16),
        "wqT": wmix[:, :D_ATTN].T.astype(BF16),
        "wvT": wmix[:, 2 * D_ATTN:n_qkv].T.astype(BF16),
        "wcu": wmix[:, n_qkv + HEADS + D_CONV:].astype(BF16),
        "wb": wmix[:, n_qkv + HEADS:n_qkv + HEADS + D_CONV].astype(BF16),
        "bf": jnp.pad(b_forget[0], (0, F_ROWS - HEADS))[:, None],
        "gq": q_norm[0], "gk": k_norm[0],
        "cw": jnp.pad(conv_w[0], ((0, SUBLANES - CONV_WIDTH), (0, 0))),
        "gconv": conv_out_norm,
        "gattn": attn_out_norm[0],
        "woa": w_mix_out[0, :D_ATTN].astype(BF16), "woc": w_mix_out[0, D_ATTN:].astype(BF16),
        "g2": ffn2_norm, "win2": ffn2_w_in[0], "wout2": ffn2_w_out[0],
        "gfin": final_norm,
    }

    meta = jnp.pad(meta_tokens.astype(x.dtype), ((0, META_TILE - N_META), (0, 0)))
    h1, qT, k, vT, f_cum, oconv, k_meta, vT_meta, f_meta = _ffn1_mix_in(x, meta, p, TOKEN_TILE)

    fq0 = f_cum[:, :, 0::Q_BLOCK].reshape(nb * HEADS, seq // Q_BLOCK)
    flast = f_cum[:, :, K_BLOCK - 1::K_BLOCK].reshape(nb * HEADS, seq // K_BLOCK)
    c_bound = (16.0 * 1.02 * jnp.max(jnp.abs(q_norm)) * jnp.max(jnp.abs(k_norm)) + 1.0).reshape(1).astype(F32)

    o_t = _fox_attention(qT, k, vT, k_meta, vT_meta, fq0, flast, f_meta[:HEADS, 0], c_bound)
    o_t = o_t.reshape(nb, D_ATTN, seq)
    return _mix_out_ffn2(h1, o_t, oconv, p, TOKEN_TILE)
```

```python
from typing import Any, NamedTuple

import jax
import jax.numpy as jnp
from jax import lax
from jax.experimental import pallas as pl
from jax.experimental.pallas import tpu as pltpu

D_MODEL = 1024
N_META = 16
D_ATTN = 512
D_CONV = 512
HEADS = 8
HEAD_DIM = 64
CONV_WIDTH = 3
D_FF = 2816
EPS = 1e-6

F32 = jnp.float32
BF16 = jnp.bfloat16

LANES = 128
SUBLANES = 8
MXU_DIM = 256
VMEM_LIMIT_BYTES = 60000 * 1024

TOKEN_TILE = 512
META_TILE = LANES
FF_CHUNK = MXU_DIM
N_FF_CHUNKS = D_FF // FF_CHUNK
Q_BLOCK = MXU_DIM
K_BLOCK = MXU_DIM
AUG_DIM = LANES
HEAD_GROUP = 4
Q_PER_STEP = 4
MASKED = -1e30
LOG2E = 1.4426950408889634
F_ROWS = 2 * SUBLANES
BIAS_PIECES = 3
SCORE_BOUND_SLACK = 1.02
SCORE_BOUND_MARGIN = 1.0
MACARON_SCALE = 0.5
WIN_STAGE_ROWS = 64
WOUT_STAGE_ROWS = 256
STAGE_SLOTS = 4

EXP_UNDERFLOW = -104.0
BOUNDED_LOGIT_MAX = 120.0

assert D_FF % FF_CHUNK == 0 and FF_CHUNK % LANES == 0 and HEADS % HEAD_GROUP == 0


def _dot(a, b):
    return jnp.dot(a, b, preferred_element_type=F32)


def _dot_nt(a, b):
    return lax.dot_general(a, b, (((1,), (1,)), ((), ())), preferred_element_type=F32)


def _rmsnorm_rows(x, gain):
    ms = jnp.mean(x * x, axis=-1, keepdims=True)
    return x * lax.rsqrt(ms + EPS) * gain


def _split3(x):
    hi = x.astype(BF16)
    r1 = x - hi.astype(F32)
    mid = r1.astype(BF16)
    lo = (r1 - mid.astype(F32)).astype(BF16)
    return hi, mid, lo


def _load_weight_as_bf16(w_hbm, w_bf, stage, sem, scale=None):
    n_slots, rows = stage.shape[0], stage.shape[1]
    n_chunks = w_hbm.shape[0] // rows
    assert n_chunks * rows == w_hbm.shape[0]

    def chunk_copy(i):
        slot = i % n_slots
        return pltpu.make_async_copy(w_hbm.at[pl.ds(i * rows, rows)], stage.at[slot], sem.at[slot])

    for i in range(min(n_slots - 1, n_chunks)):
        chunk_copy(i).start()
    for i in range(n_chunks):
        if i + n_slots - 1 < n_chunks:
            chunk_copy(i + n_slots - 1).start()
        chunk_copy(i).wait()
        w = stage[i % n_slots]
        w_bf[i * rows:(i + 1) * rows, :] = (w if scale is None else w * scale).astype(BF16)


def _swiglu(xn_ref, win_ref, wout_ref, acc_ref, act_ref, residual):
    for c in range(N_FF_CHUNKS):
        lo, hi = c * FF_CHUNK, (c + 1) * FF_CHUNK
        xn = xn_ref[...]
        g = _dot(xn, win_ref[:, lo:hi])
        u = _dot(xn, win_ref[:, D_FF + lo:D_FF + hi])
        act_ref[:, lo:hi] = (g * jax.nn.sigmoid(g) * u).astype(BF16)
    acc_ref[...] = residual + _dot(act_ref[...], wout_ref[...])


class _MixWeights(NamedTuple):
    g1: Any
    win: Any
    wout: Any
    gmix: Any
    wfkT: Any
    wqT: Any
    wvT: Any
    wcu: Any
    wb: Any
    bf: Any
    tri: Any
    gq: Any
    gk: Any
    cw: Any
    gconv: Any


class _TileOut(NamedTuple):
    h1: Any
    q: Any
    k: Any
    v: Any
    f: Any
    oconv: Any


def _token_tile(x, n_valid, w, xn_ref, acc_ref, act_ref, zc_ref, fc_ref, out):
    tm = x.shape[0]

    xn_ref[...] = _rmsnorm_rows(x, w.g1[...]).astype(BF16)
    _swiglu(xn_ref, w.win, w.wout, acc_ref, act_ref, x)
    h1 = acc_ref[...]
    if out.h1 is not None:
        out.h1[0] = h1

    xn2 = _rmsnorm_rows(h1, w.gmix[...]).astype(BF16)
    fkT = _dot_nt(w.wfkT[...], xn2)
    cu = _dot(xn2, w.wcu[...])

    fl = fkT[0:F_ROWS] + w.bf[...]
    logf = jnp.minimum(fl, 0.0) - jnp.log(1.0 + jnp.exp(-jnp.abs(fl)))
    if n_valid < tm:
        logf = jnp.where(lax.broadcasted_iota(jnp.int32, logf.shape, 1) < n_valid, logf, 0.0)
    pieces = _dot(jnp.concatenate(_split3(logf), axis=0), w.tri[0:tm, 0:tm])
    csum = pieces[0:F_ROWS] + pieces[F_ROWS:2 * F_ROWS] + pieces[2 * F_ROWS:3 * F_ROWS]
    f_all = csum + jnp.concatenate([fc_ref[...]] * (tm // LANES), axis=1)
    fc_ref[...] = jnp.broadcast_to(f_all[:, tm - 1:tm], fc_ref.shape)
    if out.f is not None:
        out.f[0] = f_all[:HEADS]

    f_hi, f_mid, f_lo = (p.astype(F32) for p in _split3(f_all * LOG2E))
    row = lax.broadcasted_iota(jnp.int32, (SUBLANES, tm), 0)
    ones_mid = jnp.where(row < 2 * BIAS_PIECES, 1.0, 0.0)

    def bias_rows(pieces, first_row, sign):
        rows = ones_mid
        for i, piece in enumerate(pieces):
            rows = jnp.where(row == first_row + i, sign * piece, rows)
        return rows
    pad_rows = jnp.zeros((AUG_DIM - HEAD_DIM - SUBLANES, tm), F32)

    def head_pieces(h):
        return (jnp.broadcast_to(f_hi[h:h + 1], (SUBLANES, tm)),
                jnp.broadcast_to(f_mid[h:h + 1], (SUBLANES, tm)),
                jnp.broadcast_to(f_lo[h:h + 1], (SUBLANES, tm)))

    def head_rmsnorm(x_t, gain_t):
        x3 = x_t.reshape(HEADS, HEAD_DIM, tm)
        return x3 * lax.rsqrt(jnp.mean(x3 * x3, axis=1, keepdims=True) + EPS) * gain_t[None]

    kn = head_rmsnorm(fkT[F_ROWS:F_ROWS + D_ATTN], w.gk[:, 0:tm])
    if out.q is not None:
        qT = _dot_nt(w.wqT[...], xn2)
    if out.oconv is not None:
        gate_b = _dot(xn2, w.wb[...])
    for h in range(HEADS):
        aug_k = bias_rows(head_pieces(h), BIAS_PIECES, -1.0)
        k_aug_t = jnp.concatenate([kn[h], aug_k, pad_rows], axis=0)
        out.k(h, k_aug_t.T.astype(BF16))

    vT = _dot_nt(w.wvT[...], xn2)
    z = cu[:, 0:D_CONV] * cu[:, D_CONV:2 * D_CONV]
    if out.oconv is not None:
        zc = zc_ref[...]
        rowz = lax.broadcasted_iota(jnp.int32, z.shape, 0)
        prev1 = jnp.broadcast_to(zc[SUBLANES - 1:SUBLANES], z.shape)
        prev2 = jnp.broadcast_to(zc[SUBLANES - 2:SUBLANES - 1], z.shape)
        z1 = jnp.where(rowz == 0, prev1, pltpu.roll(z, 1, axis=0))
        z2 = jnp.where(rowz == 0, prev2, jnp.where(rowz == 1, prev1, pltpu.roll(z, 2, axis=0)))
        cw = w.cw[...]
        y = cw[0:1] * z2 + cw[1:2] * z1 + cw[2:3] * z
        out.oconv[0] = _rmsnorm_rows(gate_b * y, w.gconv[...]).astype(BF16)
    zc_ref[...] = z[n_valid - SUBLANES:n_valid]

    if out.q is not None:
        qn = head_rmsnorm(qT, w.gq[:, 0:tm] * (HEAD_DIM ** -0.5 * LOG2E))
    v3 = vT.reshape(HEADS, HEAD_DIM, tm)
    for h in range(HEADS):
        if out.q is not None:
            aug_q = bias_rows(head_pieces(h), 0, 1.0)
            out.q[0, h] = jnp.concatenate([qn[h], aug_q, pad_rows], axis=0).astype(BF16)
        out.v(h, v3[h].astype(BF16))


def _ffn1_mix_in_kernel(
        x_ref, meta_ref,
        g1_ref, win_hbm, wout_hbm,
        gmix_ref, wfkT_ref, wqT_ref, wvT_ref, wcu_ref, wb_ref, bf_ref, tri_ref,
        gq_ref, gk_ref, cw_ref, gconv_ref,
        h1_ref, qT_ref, k_ref, vT_ref, f_ref, oconv_ref, km_ref, vmT_ref, fmeta_ref,
        xn_ref, acc_ref, act_ref, zc_ref, fc_ref, zc0_ref, fc0_ref,
        win_ref, wout_ref, win_stage, wout_stage, win_sem, wout_sem):
    t = pl.program_id(1)
    w = _MixWeights(g1_ref, win_ref, wout_ref, gmix_ref, wfkT_ref, wqT_ref, wvT_ref, wcu_ref, wb_ref,
                    bf_ref, tri_ref, gq_ref, gk_ref, cw_ref, gconv_ref)

    def put_k(h, k_aug):
        k_ref[0, h] = k_aug

    def put_v(h, v_t):
        vT_ref[0, h] = v_t

    def put_meta_k(h, k_aug):
        km_ref[h] = k_aug[0:N_META]

    def put_meta_v(h, v_t):
        vmT_ref[h] = v_t[:, 0:N_META]

    @pl.when(jnp.logical_and(pl.program_id(0) == 0, t == 0))
    def _():
        _load_weight_as_bf16(win_hbm, win_ref, win_stage, win_sem)
        _load_weight_as_bf16(wout_hbm, wout_ref, wout_stage, wout_sem, MACARON_SCALE)
        rows = meta_ref.shape[0]
        zc_ref[...] = jnp.zeros_like(zc_ref)
        fc_ref[...] = jnp.zeros_like(fc_ref)
        _token_tile(meta_ref[...], N_META, w, xn_ref.at[pl.ds(0, rows)], acc_ref.at[pl.ds(0, rows)],
                    act_ref.at[pl.ds(0, rows)], zc_ref, fc_ref, _TileOut(None, None, put_meta_k, put_meta_v, None, None))
        zc0_ref[...] = zc_ref[...]
        fc0_ref[...] = fc_ref[...]
        fmeta_ref[...] = fc_ref[...]

    @pl.when(t == 0)
    def _():
        zc_ref[...] = zc0_ref[...]
        fc_ref[...] = fc0_ref[...]

    _token_tile(x_ref[0], x_ref.shape[1], w, xn_ref, acc_ref, act_ref, zc_ref, fc_ref,
                _TileOut(h1_ref, qT_ref, put_k, put_v, f_ref, oconv_ref))


def _ffn_weight_scratch():
    return [pltpu.VMEM((D_MODEL, 2 * D_FF), BF16), pltpu.VMEM((D_FF, D_MODEL), BF16),
            pltpu.VMEM((STAGE_SLOTS, WIN_STAGE_ROWS, 2 * D_FF), F32),
            pltpu.VMEM((STAGE_SLOTS, WOUT_STAGE_ROWS, D_MODEL), F32),
            pltpu.SemaphoreType.DMA((STAGE_SLOTS,)), pltpu.SemaphoreType.DMA((STAGE_SLOTS,))]


def _const_spec(shape):
    nd = len(shape)
    return pl.BlockSpec(shape, lambda *_: (0,) * nd, pipeline_mode=pl.Buffered(1))


def _ffn1_mix_in(x, meta, p, tm):
    nb, seq, _ = x.shape
    nt = seq // tm
    assert tm % LANES == 0 and meta.shape[0] <= tm
    tri = jnp.triu(jnp.ones((tm, tm), BF16))
    gq_t = jnp.broadcast_to(p["gq"][:, None], (HEAD_DIM, tm))
    gk_t = jnp.broadcast_to(p["gk"][:, None], (HEAD_DIM, tm))

    def tile3(last):
        return pl.BlockSpec((1, tm, last), lambda b, t: (b, t, 0))

    in_specs = [
        tile3(D_MODEL), _const_spec(meta.shape),
        _const_spec((1, D_MODEL)),
        pl.BlockSpec(memory_space=pl.ANY), pl.BlockSpec(memory_space=pl.ANY),
        _const_spec((1, D_MODEL)), _const_spec((F_ROWS + D_ATTN, D_MODEL)),
        _const_spec((D_ATTN, D_MODEL)), _const_spec((D_ATTN, D_MODEL)),
        _const_spec((D_MODEL, 2 * D_CONV)), _const_spec((D_MODEL, D_CONV)),
        _const_spec((F_ROWS, 1)), _const_spec((tm, tm)),
        _const_spec((HEAD_DIM, tm)), _const_spec((HEAD_DIM, tm)),
        _const_spec((SUBLANES, D_CONV)), _const_spec((1, D_CONV)),
    ]
    out_shape = [
        jax.ShapeDtypeStruct((nb, seq, D_MODEL), F32),
        jax.ShapeDtypeStruct((nb, HEADS, AUG_DIM, seq), BF16),
        jax.ShapeDtypeStruct((nb, HEADS, seq, AUG_DIM), BF16),
        jax.ShapeDtypeStruct((nb, HEADS, HEAD_DIM, seq), BF16),
        jax.ShapeDtypeStruct((nb, HEADS, seq), F32),
        jax.ShapeDtypeStruct((nb, seq, D_CONV), BF16),
        jax.ShapeDtypeStruct((HEADS, N_META, AUG_DIM), BF16),
        jax.ShapeDtypeStruct((HEADS, HEAD_DIM, N_META), BF16),
        jax.ShapeDtypeStruct((F_ROWS, LANES), F32),
    ]
    out_specs = [
        tile3(D_MODEL),
        pl.BlockSpec((1, HEADS, AUG_DIM, tm), lambda b, t: (b, 0, 0, t)),
        pl.BlockSpec((1, HEADS, tm, AUG_DIM), lambda b, t: (b, 0, t, 0)),
        pl.BlockSpec((1, HEADS, HEAD_DIM, tm), lambda b, t: (b, 0, 0, t)),
        pl.BlockSpec((1, HEADS, tm), lambda b, t: (b, 0, t)),
        tile3(D_CONV),
        pl.BlockSpec((HEADS, N_META, AUG_DIM), lambda b, t: (0, 0, 0)),
        pl.BlockSpec((HEADS, HEAD_DIM, N_META), lambda b, t: (0, 0, 0)),
        pl.BlockSpec((F_ROWS, LANES), lambda b, t: (0, 0)),
    ]
    scratch_shapes = [
        pltpu.VMEM((tm, D_MODEL), BF16),
        pltpu.VMEM((tm, D_MODEL), F32),
        pltpu.VMEM((tm, D_FF), BF16),
        pltpu.VMEM((SUBLANES, D_CONV), F32),
        pltpu.VMEM((F_ROWS, LANES), F32),
        pltpu.VMEM((SUBLANES, D_CONV), F32),
        pltpu.VMEM((F_ROWS, LANES), F32),
    ] + _ffn_weight_scratch()
    return pl.pallas_call(
        _ffn1_mix_in_kernel, out_shape=out_shape, grid=(nb, nt), in_specs=in_specs, out_specs=out_specs,
        scratch_shapes=scratch_shapes, name="ffn1_mix_in",
        compiler_params=pltpu.CompilerParams(
            dimension_semantics=("arbitrary", "arbitrary"), vmem_limit_bytes=VMEM_LIMIT_BYTES),
    )(x, meta,
      p["g1"], p["win1"], p["wout1"],
      p["gmix"], p["wfkT"], p["wqT"], p["wvT"], p["wcu"], p["wb"], p["bf"], tri,
      gq_t, gk_t, p["cw"], p["gconv"])


def _fox_attention_kernel(fq0_ref, flast_ref, fmeta_ref, cb_ref,
                          qT_ref, k_ref, vT_ref, km_ref, vmT_ref,
                          o_ref,
                          acc_ref, l_ref):
    bh0 = pl.program_id(0) * HEADS + pl.program_id(1) * HEAD_GROUP
    heads = range(HEAD_GROUP)
    nq = qT_ref.shape[3] // Q_BLOCK
    c_bound = cb_ref[0]
    key_idx = lax.broadcasted_iota(jnp.int32, (K_BLOCK, Q_BLOCK), 0)
    qry_idx = lax.broadcasted_iota(jnp.int32, (K_BLOCK, Q_BLOCK), 1)
    causal = key_idx <= qry_idx

    def block_live(g, i, j):
        return fq0_ref[bh0 + g, i] - flast_ref[bh0 + g, jnp.maximum(j, 0)] + c_bound >= EXP_UNDERFLOW

    def meta_live(g, i):
        head = pl.program_id(1) * HEAD_GROUP + g
        return fq0_ref[bh0 + g, i] - fmeta_ref[head] + c_bound >= EXP_UNDERFLOW

    def sublane_partial_sum(p):
        return jnp.sum(p.reshape(p.shape[0] // SUBLANES, SUBLANES, p.shape[1]), axis=0)

    def bounded_tile(k_blk, v_t, q_t):
        p = jnp.exp2(_dot(k_blk, q_t))
        return sublane_partial_sum(p), _dot(v_t, p.astype(BF16))

    def bounded_q_blocks(i_first):
        chains = [(r, g) for r in range(Q_PER_STEP) for g in heads]
        blk = [i_first + r for r in range(Q_PER_STEP)]
        q0 = [pl.multiple_of(i * Q_BLOCK, Q_BLOCK) for i in blk]
        p0 = [pl.multiple_of(jnp.maximum(i - 1, 0) * K_BLOCK, K_BLOCK) for i in blk]
        first_mask = jnp.where(i_first >= 1, 0.0, MASKED)
        q_t = [qT_ref[0, g, :, pl.ds(q0[r], Q_BLOCK)] for r, g in chains]
        s_d = [_dot(k_ref[0, g, pl.ds(q0[r], K_BLOCK), :], q_t[c]) for c, (r, g) in enumerate(chains)]
        s_p = [_dot(k_ref[0, g, pl.ds(p0[r], K_BLOCK), :], q_t[c]) for c, (r, g) in enumerate(chains)]
        p_d = [jnp.exp2(jnp.where(causal, s, MASKED)) for s in s_d]
        p_p = [jnp.exp2(s + first_mask if r == 0 else s) for s, (r, g) in zip(s_p, chains)]
        for c, (r, g) in enumerate(chains):
            l_ref[c] = sublane_partial_sum(p_d[c]) + sublane_partial_sum(p_p[c])
            acc_ref[c] = (_dot(vT_ref[0, g, :, pl.ds(q0[r], K_BLOCK)], p_d[c].astype(BF16))
                          + _dot(vT_ref[0, g, :, pl.ds(p0[r], K_BLOCK)], p_p[c].astype(BF16)))

        def any_head(test):
            hit = test(0)
            for g in heads[1:]:
                hit = jnp.logical_or(hit, test(g))
            return hit

        def walk(r):
            i = blk[r]

            def cond(j):
                return jnp.logical_and(j >= 0, any_head(lambda g: block_live(g, i, j)))

            def body(j):
                k0 = pl.multiple_of(j * K_BLOCK, K_BLOCK)
                for g in heads:
                    c = r * HEAD_GROUP + g
                    l_j, acc_j = bounded_tile(k_ref[0, g, pl.ds(k0, K_BLOCK), :],
                                              vT_ref[0, g, :, pl.ds(k0, K_BLOCK)],
                                              qT_ref[0, g, :, pl.ds(q0[r], Q_BLOCK)])
                    l_ref[c] += l_j
                    acc_ref[c] += acc_j
                return j - 1

            lax.while_loop(cond, body, i - 2)

            @pl.when(any_head(lambda g: meta_live(g, i)))
            def _():
                for g in heads:
                    c = r * HEAD_GROUP + g
                    l_m, acc_m = bounded_tile(km_ref[g], vmT_ref[g], qT_ref[0, g, :, pl.ds(q0[r], Q_BLOCK)])
                    l_ref[c] += l_m
                    acc_ref[c] += acc_m

        walks = [jnp.logical_or(jnp.logical_and(i >= 2, any_head(lambda g, i=i: block_live(g, i, i - 2))),
                                any_head(lambda g, i=i: meta_live(g, i))) for i in blk]
        any_walk = walks[0]
        for hit in walks[1:]:
            any_walk = jnp.logical_or(any_walk, hit)

        @pl.when(any_walk)
        def _():
            for r in range(Q_PER_STEP):
                walk(r)

        for c, (r, g) in enumerate(chains):
            l_tot = jnp.sum(l_ref[c], axis=0, keepdims=True)
            o_ref[0, g, :, pl.ds(q0[r], Q_BLOCK)] = (acc_ref[c] / l_tot).astype(o_ref.dtype)

    @pl.when(c_bound <= BOUNDED_LOGIT_MAX)
    def _():
        def step(n, carry):
            bounded_q_blocks(n * Q_PER_STEP)
            return carry

        lax.fori_loop(0, nq // Q_PER_STEP, step, 0)

    def online_q_block(g, i):
        q0 = pl.multiple_of(i * Q_BLOCK, Q_BLOCK)
        q_t = qT_ref[0, g, :, pl.ds(q0, Q_BLOCK)]
        s_m = _dot(km_ref[g], q_t)
        m = jnp.max(s_m, axis=0, keepdims=True)
        p_m = jnp.exp2(s_m - m)
        l = jnp.sum(p_m, axis=0, keepdims=True)
        acc = _dot(vmT_ref[g], p_m.astype(BF16))

        def online_step(state, k0, masked):
            m, l, acc = state
            s = _dot(k_ref[0, g, pl.ds(k0, K_BLOCK), :], q_t)
            if masked:
                s = jnp.where(causal, s, MASKED)
            m_new = jnp.maximum(m, jnp.max(s, axis=0, keepdims=True))
            alpha = jnp.exp2(m - m_new)
            p = jnp.exp2(s - m_new)
            l = alpha * l + jnp.sum(p, axis=0, keepdims=True)
            acc = alpha * acc + _dot(vT_ref[0, g, :, pl.ds(k0, K_BLOCK)], p.astype(BF16))
            return m_new, l, acc

        state = online_step((m, l, acc), q0, True)

        def cond(carry):
            return jnp.logical_and(carry[0] >= 0, block_live(g, i, carry[0]))

        def body(carry):
            j = carry[0]
            return (j - 1,) + online_step(carry[1:], pl.multiple_of(j * K_BLOCK, K_BLOCK), False)

        _, m, l, acc = lax.while_loop(cond, body, (i - 1,) + state)
        o_ref[0, g, :, pl.ds(q0, Q_BLOCK)] = (acc / l).astype(o_ref.dtype)

    @pl.when(c_bound > BOUNDED_LOGIT_MAX)
    def _():
        def head_loop(g, carry):
            def step(i, inner):
                online_q_block(g, i)
                return inner

            lax.fori_loop(0, nq, step, 0)
            return carry

        lax.fori_loop(0, HEAD_GROUP, head_loop, 0)


def _fox_attention(qT, k, vT, k_meta, vT_meta, fq0, flast, fmeta, c_bound):
    nb, _, _, seq = qT.shape
    assert seq % (Q_PER_STEP * Q_BLOCK) == 0 and Q_BLOCK == K_BLOCK
    smem = pl.BlockSpec(memory_space=pltpu.SMEM)
    hg = HEAD_GROUP
    return pl.pallas_call(
        _fox_attention_kernel,
        out_shape=jax.ShapeDtypeStruct((nb, HEADS, HEAD_DIM, seq), BF16),
        grid=(nb, HEADS // hg),
        in_specs=[
            smem, smem, smem, smem,
            pl.BlockSpec((1, hg, AUG_DIM, seq), lambda b, h: (b, h, 0, 0)),
            pl.BlockSpec((1, hg, seq, AUG_DIM), lambda b, h: (b, h, 0, 0)),
            pl.BlockSpec((1, hg, HEAD_DIM, seq), lambda b, h: (b, h, 0, 0)),
            pl.BlockSpec((hg, N_META, AUG_DIM), lambda b, h: (h, 0, 0)),
            pl.BlockSpec((hg, HEAD_DIM, N_META), lambda b, h: (h, 0, 0)),
        ],
        out_specs=pl.BlockSpec((1, hg, HEAD_DIM, seq), lambda b, h: (b, h, 0, 0)),
        scratch_shapes=[pltpu.VMEM((Q_PER_STEP * hg, HEAD_DIM, Q_BLOCK), F32),
                        pltpu.VMEM((Q_PER_STEP * hg, SUBLANES, Q_BLOCK), F32)],
        name="fox_attention",
        compiler_params=pltpu.CompilerParams(
            dimension_semantics=("arbitrary", "arbitrary"), vmem_limit_bytes=VMEM_LIMIT_BYTES),
    )(fq0, flast, fmeta, c_bound, qT, k, vT, k_meta, vT_meta)


def _mix_out_ffn2_kernel(h1_ref, oT_ref, oconv_ref,
                         gattn_ref, woa_ref, woc_ref,
                         g2_ref, win_hbm, wout_hbm, gfin_ref,
                         out_ref,
                         xn_ref, acc_ref, act_ref, win_ref, wout_ref, win_stage, wout_stage, win_sem, wout_sem):
    @pl.when(jnp.logical_and(pl.program_id(0) == 0, pl.program_id(1) == 0))
    def _():
        _load_weight_as_bf16(win_hbm, win_ref, win_stage, win_sem)
        _load_weight_as_bf16(wout_hbm, wout_ref, wout_stage, wout_sem, MACARON_SCALE)

    o_t = oT_ref[0].astype(F32)
    ms = jnp.mean(o_t * o_t, axis=0, keepdims=True)
    o_n = (o_t * lax.rsqrt(ms + EPS) * gattn_ref[...]).T.astype(BF16)
    mix = _dot(o_n, woa_ref[...]) + _dot(oconv_ref[0], woc_ref[...])
    h2 = h1_ref[0] + mix
    xn_ref[...] = _rmsnorm_rows(h2, g2_ref[...]).astype(BF16)
    _swiglu(xn_ref, win_ref, wout_ref, acc_ref, act_ref, h2)
    out_ref[0] = _rmsnorm_rows(acc_ref[...], gfin_ref[...])


def _mix_out_ffn2(h1, o_t, oconv, p, tm):
    nb, seq, _ = h1.shape
    gattn_t = jnp.broadcast_to(p["gattn"][:, None], (D_ATTN, tm))
    in_specs = [
        pl.BlockSpec((1, tm, D_MODEL), lambda b, t: (b, t, 0)),
        pl.BlockSpec((1, D_ATTN, tm), lambda b, t: (b, 0, t)),
        pl.BlockSpec((1, tm, D_CONV), lambda b, t: (b, t, 0)),
        _const_spec((D_ATTN, tm)), _const_spec((D_ATTN, D_MODEL)), _const_spec((D_CONV, D_MODEL)),
        _const_spec((1, D_MODEL)),
        pl.BlockSpec(memory_space=pl.ANY), pl.BlockSpec(memory_space=pl.ANY),
        _const_spec((1, D_MODEL)),
    ]
    return pl.pallas_call(
        _mix_out_ffn2_kernel,
        out_shape=jax.ShapeDtypeStruct((nb, seq, D_MODEL), F32),
        grid=(nb, seq // tm),
        in_specs=in_specs,
        out_specs=pl.BlockSpec((1, tm, D_MODEL), lambda b, t: (b, t, 0)),
        scratch_shapes=[pltpu.VMEM((tm, D_MODEL), BF16), pltpu.VMEM((tm, D_MODEL), F32),
                        pltpu.VMEM((tm, D_FF), BF16)] + _ffn_weight_scratch(),
        name="mix_out_ffn2",
        compiler_params=pltpu.CompilerParams(
            dimension_semantics=("arbitrary", "arbitrary"), vmem_limit_bytes=VMEM_LIMIT_BYTES),
    )(h1, o_t, oconv, gattn_t, p["woa"], p["woc"], p["g2"], p["win2"], p["wout2"], p["gfin"])


def kernel(x, meta_tokens, ffn1_norm, ffn1_w_in, ffn1_w_out, mix_norm, w_mix_in, b_forget, q_norm, k_norm, conv_w, attn_out_norm, conv_out_norm, w_mix_out, ffn2_norm, ffn2_w_in, ffn2_w_out, final_norm):
    nb, seq, _ = x.shape
    wmix = w_mix_in[0]
    n_qkv = 3 * D_ATTN
    p = {
        "g1": ffn1_norm, "win1": ffn1_w_in[0], "wout1": ffn1_w_out[0],
        "gmix": mix_norm,
        "wfkT": jnp.concatenate([jnp.pad(wmix[:, n_qkv:n_qkv + HEADS].T, ((0, F_ROWS - HEADS), (0, 0))),
                                 wmix[:, D_ATTN:2 * D_ATTN].T], axis=0).astype(BF16),
        "wqT": wmix[:, :D_ATTN].T.astype(BF16),
        "wvT": wmix[:, 2 * D_ATTN:n_qkv].T.astype(BF16),
        "wcu": wmix[:, n_qkv + HEADS + D_CONV:].astype(BF16),
        "wb": wmix[:, n_qkv + HEADS:n_qkv + HEADS + D_CONV].astype(BF16),
        "bf": jnp.pad(b_forget[0], (0, F_ROWS - HEADS))[:, None],
        "gq": q_norm[0], "gk": k_norm[0],
        "cw": jnp.pad(conv_w[0], ((0, SUBLANES - CONV_WIDTH), (0, 0))),
        "gconv": conv_out_norm,
        "gattn": attn_out_norm[0],
        "woa": w_mix_out[0, :D_ATTN].astype(BF16), "woc": w_mix_out[0, D_ATTN:].astype(BF16),
        "g2": ffn2_norm, "win2": ffn2_w_in[0], "wout2": ffn2_w_out[0],
        "gfin": final_norm,
    }

    meta = jnp.pad(meta_tokens.astype(x.dtype), ((0, META_TILE - N_META), (0, 0)))
    h1, qT, k, vT, f_cum, oconv, k_meta, vT_meta, f_meta = _ffn1_mix_in(x, meta, p, TOKEN_TILE)

    fq0 = f_cum[:, :, 0::Q_BLOCK].reshape(nb * HEADS, seq // Q_BLOCK)
    flast = f_cum[:, :, K_BLOCK - 1::K_BLOCK].reshape(nb * HEADS, seq // K_BLOCK)
    c_bound = (2.0 * HEAD_DIM ** 0.5 * SCORE_BOUND_SLACK * jnp.max(jnp.abs(q_norm)) * jnp.max(jnp.abs(k_norm))
               + SCORE_BOUND_MARGIN).reshape(1).astype(F32)

    o_t = _fox_attention(qT, k, vT, k_meta, vT_meta, fq0, flast, f_meta[:HEADS, 0], c_bound)
    o_t = o_t.reshape(nb, D_ATTN, seq)
    return _mix_out_ffn2(h1, o_t, oconv, p, TOKEN_TILE)
```

```python
from typing import Any, NamedTuple

import jax
import jax.numpy as jnp
from jax import lax
from jax.experimental import pallas as pl
from jax.experimental.pallas import tpu as pltpu

D_MODEL = 1024
N_META = 16
D_ATTN = 512
D_CONV = 512
HEADS = 8
HEAD_DIM = 64
CONV_WIDTH = 3
D_FF = 2816
EPS = 1e-6

F32 = jnp.float32
BF16 = jnp.bfloat16

LANES = 128
SUBLANES = 8
MXU_DIM = 256
VMEM_LIMIT_BYTES = 60000 * 1024

TOKEN_TILE = 512
META_TILE = LANES
FF_CHUNK = MXU_DIM
N_FF_CHUNKS = D_FF // FF_CHUNK
Q_BLOCK = MXU_DIM
K_BLOCK = MXU_DIM
AUG_DIM = LANES
HEAD_GROUP = 4
Q_PER_STEP = 4
CHAIN_GROUP = 2
VALUE_DOT_LAG = 2
MASKED = -1e30
LOG2E = 1.4426950408889634
F_ROWS = 2 * SUBLANES
BIAS_PIECES = 3
SCORE_BOUND_SLACK = 1.02
SCORE_BOUND_MARGIN = 1.0
MACARON_SCALE = 0.5
WIN_STAGE_ROWS = 64
WOUT_STAGE_ROWS = 256
STAGE_SLOTS = 4

EXP_UNDERFLOW = -104.0
BOUNDED_LOGIT_MAX = 120.0

assert D_FF % FF_CHUNK == 0 and FF_CHUNK % LANES == 0 and HEADS % HEAD_GROUP == 0


def _dot(a, b):
    return jnp.dot(a, b, preferred_element_type=F32)


def _dot_nt(a, b):
    return lax.dot_general(a, b, (((1,), (1,)), ((), ())), preferred_element_type=F32)


def _rmsnorm_rows(x, gain):
    ms = jnp.mean(x * x, axis=-1, keepdims=True)
    return x * lax.rsqrt(ms + EPS) * gain


def _split3(x):
    hi = x.astype(BF16)
    r1 = x - hi.astype(F32)
    mid = r1.astype(BF16)
    lo = (r1 - mid.astype(F32)).astype(BF16)
    return hi, mid, lo


def _load_weight_as_bf16(w_hbm, w_bf, stage, sem, scale=None):
    n_slots, rows = stage.shape[0], stage.shape[1]
    n_chunks = w_hbm.shape[0] // rows
    assert n_chunks * rows == w_hbm.shape[0]

    def chunk_copy(i):
        slot = i % n_slots
        return pltpu.make_async_copy(w_hbm.at[pl.ds(i * rows, rows)], stage.at[slot], sem.at[slot])

    for i in range(min(n_slots - 1, n_chunks)):
        chunk_copy(i).start()
    for i in range(n_chunks):
        if i + n_slots - 1 < n_chunks:
            chunk_copy(i + n_slots - 1).start()
        chunk_copy(i).wait()
        w = stage[i % n_slots]
        w_bf[i * rows:(i + 1) * rows, :] = (w if scale is None else w * scale).astype(BF16)


def _swiglu(xn_ref, win_ref, wout_ref, acc_ref, act_ref, residual):
    for c in range(N_FF_CHUNKS):
        lo, hi = c * FF_CHUNK, (c + 1) * FF_CHUNK
        xn = xn_ref[...]
        g = _dot(xn, win_ref[:, lo:hi])
        u = _dot(xn, win_ref[:, D_FF + lo:D_FF + hi])
        act_ref[:, lo:hi] = (g * jax.nn.sigmoid(g) * u).astype(BF16)
    acc_ref[...] = residual + _dot(act_ref[...], wout_ref[...])


class _MixWeights(NamedTuple):
    g1: Any
    win: Any
    wout: Any
    gmix: Any
    wfkT: Any
    wqT: Any
    wvT: Any
    wcu: Any
    wb: Any
    bf: Any
    tri: Any
    gq: Any
    gk: Any
    cw: Any
    gconv: Any


class _TileOut(NamedTuple):
    h1: Any
    q: Any
    k: Any
    v: Any
    f: Any
    oconv: Any


def _token_tile(x, n_valid, w, xn_ref, acc_ref, act_ref, zc_ref, fc_ref, out):
    tm = x.shape[0]

    xn_ref[...] = _rmsnorm_rows(x, w.g1[...]).astype(BF16)
    _swiglu(xn_ref, w.win, w.wout, acc_ref, act_ref, x)
    h1 = acc_ref[...]
    if out.h1 is not None:
        out.h1[0] = h1

    xn2 = _rmsnorm_rows(h1, w.gmix[...]).astype(BF16)
    fkT = _dot_nt(w.wfkT[...], xn2)
    cu = _dot(xn2, w.wcu[...])

    fl = fkT[0:F_ROWS] + w.bf[...]
    logf = jnp.minimum(fl, 0.0) - jnp.log(1.0 + jnp.exp(-jnp.abs(fl)))
    if n_valid < tm:
        logf = jnp.where(lax.broadcasted_iota(jnp.int32, logf.shape, 1) < n_valid, logf, 0.0)
    pieces = _dot(jnp.concatenate(_split3(logf), axis=0), w.tri[0:tm, 0:tm])
    csum = pieces[0:F_ROWS] + pieces[F_ROWS:2 * F_ROWS] + pieces[2 * F_ROWS:3 * F_ROWS]
    f_all = csum + jnp.concatenate([fc_ref[...]] * (tm // LANES), axis=1)
    fc_ref[...] = jnp.broadcast_to(f_all[:, tm - 1:tm], fc_ref.shape)
    if out.f is not None:
        out.f[0] = f_all[:HEADS]

    f_hi, f_mid, f_lo = (p.astype(F32) for p in _split3(f_all * LOG2E))
    row = lax.broadcasted_iota(jnp.int32, (SUBLANES, tm), 0)
    ones_mid = jnp.where(row < 2 * BIAS_PIECES, 1.0, 0.0)

    def bias_rows(pieces, first_row, sign):
        rows = ones_mid
        for i, piece in enumerate(pieces):
            rows = jnp.where(row == first_row + i, sign * piece, rows)
        return rows
    pad_rows = jnp.zeros((AUG_DIM - HEAD_DIM - SUBLANES, tm), F32)

    def head_pieces(h):
        return (jnp.broadcast_to(f_hi[h:h + 1], (SUBLANES, tm)),
                jnp.broadcast_to(f_mid[h:h + 1], (SUBLANES, tm)),
                jnp.broadcast_to(f_lo[h:h + 1], (SUBLANES, tm)))

    def head_rmsnorm(x_t, gain_t):
        x3 = x_t.reshape(HEADS, HEAD_DIM, tm)
        return x3 * lax.rsqrt(jnp.mean(x3 * x3, axis=1, keepdims=True) + EPS) * gain_t[None]

    kn = head_rmsnorm(fkT[F_ROWS:F_ROWS + D_ATTN], w.gk[:, 0:tm])
    if out.q is not None:
        qT = _dot_nt(w.wqT[...], xn2)
    if out.oconv is not None:
        gate_b = _dot(xn2, w.wb[...])
    for h in range(HEADS):
        aug_k = bias_rows(head_pieces(h), BIAS_PIECES, -1.0)
        k_aug_t = jnp.concatenate([kn[h], aug_k, pad_rows], axis=0)
        out.k(h, k_aug_t.T.astype(BF16))

    vT = _dot_nt(w.wvT[...], xn2)
    z = cu[:, 0:D_CONV] * cu[:, D_CONV:2 * D_CONV]
    if out.oconv is not None:
        zc = zc_ref[...]
        rowz = lax.broadcasted_iota(jnp.int32, z.shape, 0)
        prev1 = jnp.broadcast_to(zc[SUBLANES - 1:SUBLANES], z.shape)
        prev2 = jnp.broadcast_to(zc[SUBLANES - 2:SUBLANES - 1], z.shape)
        z1 = jnp.where(rowz == 0, prev1, pltpu.roll(z, 1, axis=0))
        z2 = jnp.where(rowz == 0, prev2, jnp.where(rowz == 1, prev1, pltpu.roll(z, 2, axis=0)))
        cw = w.cw[...]
        y = cw[0:1] * z2 + cw[1:2] * z1 + cw[2:3] * z
        out.oconv[0] = _rmsnorm_rows(gate_b * y, w.gconv[...]).astype(BF16)
    zc_ref[...] = z[n_valid - SUBLANES:n_valid]

    if out.q is not None:
        qn = head_rmsnorm(qT, w.gq[:, 0:tm] * (HEAD_DIM ** -0.5 * LOG2E))
    v3 = vT.reshape(HEADS, HEAD_DIM, tm)
    for h in range(HEADS):
        if out.q is not None:
            aug_q = bias_rows(head_pieces(h), 0, 1.0)
            out.q[0, h] = jnp.concatenate([qn[h], aug_q, pad_rows], axis=0).astype(BF16)
        out.v(h, v3[h].astype(BF16))


def _ffn1_mix_in_kernel(
        x_ref, meta_ref,
        g1_ref, win_hbm, wout_hbm,
        gmix_ref, wfkT_ref, wqT_ref, wvT_ref, wcu_ref, wb_ref, bf_ref, tri_ref,
        gq_ref, gk_ref, cw_ref, gconv_ref,
        h1_ref, qT_ref, k_ref, vT_ref, f_ref, oconv_ref, km_ref, vmT_ref, fmeta_ref,
        xn_ref, acc_ref, act_ref, zc_ref, fc_ref, zc0_ref, fc0_ref,
        win_ref, wout_ref, win_stage, wout_stage, win_sem, wout_sem):
    t = pl.program_id(1)
    w = _MixWeights(g1_ref, win_ref, wout_ref, gmix_ref, wfkT_ref, wqT_ref, wvT_ref, wcu_ref, wb_ref,
                    bf_ref, tri_ref, gq_ref, gk_ref, cw_ref, gconv_ref)

    def put_k(h, k_aug):
        k_ref[0, h] = k_aug

    def put_v(h, v_t):
        vT_ref[0, h] = v_t

    def put_meta_k(h, k_aug):
        km_ref[h] = k_aug[0:N_META]

    def put_meta_v(h, v_t):
        vmT_ref[h] = v_t[:, 0:N_META]

    @pl.when(jnp.logical_and(pl.program_id(0) == 0, t == 0))
    def _():
        _load_weight_as_bf16(win_hbm, win_ref, win_stage, win_sem)
        _load_weight_as_bf16(wout_hbm, wout_ref, wout_stage, wout_sem, MACARON_SCALE)
        rows = meta_ref.shape[0]
        zc_ref[...] = jnp.zeros_like(zc_ref)
        fc_ref[...] = jnp.zeros_like(fc_ref)
        _token_tile(meta_ref[...], N_META, w, xn_ref.at[pl.ds(0, rows)], acc_ref.at[pl.ds(0, rows)],
                    act_ref.at[pl.ds(0, rows)], zc_ref, fc_ref, _TileOut(None, None, put_meta_k, put_meta_v, None, None))
        zc0_ref[...] = zc_ref[...]
        fc0_ref[...] = fc_ref[...]
        fmeta_ref[...] = fc_ref[...]

    @pl.when(t == 0)
    def _():
        zc_ref[...] = zc0_ref[...]
        fc_ref[...] = fc0_ref[...]

    _token_tile(x_ref[0], x_ref.shape[1], w, xn_ref, acc_ref, act_ref, zc_ref, fc_ref,
                _TileOut(h1_ref, qT_ref, put_k, put_v, f_ref, oconv_ref))


def _ffn_weight_scratch():
    return [pltpu.VMEM((D_MODEL, 2 * D_FF), BF16), pltpu.VMEM((D_FF, D_MODEL), BF16),
            pltpu.VMEM((STAGE_SLOTS, WIN_STAGE_ROWS, 2 * D_FF), F32),
            pltpu.VMEM((STAGE_SLOTS, WOUT_STAGE_ROWS, D_MODEL), F32),
            pltpu.SemaphoreType.DMA((STAGE_SLOTS,)), pltpu.SemaphoreType.DMA((STAGE_SLOTS,))]


def _const_spec(shape):
    nd = len(shape)
    return pl.BlockSpec(shape, lambda *_: (0,) * nd, pipeline_mode=pl.Buffered(1))


def _ffn1_mix_in(x, meta, p, tm):
    nb, seq, _ = x.shape
    nt = seq // tm
    assert tm % LANES == 0 and meta.shape[0] <= tm
    tri = jnp.triu(jnp.ones((tm, tm), BF16))
    gq_t = jnp.broadcast_to(p["gq"][:, None], (HEAD_DIM, tm))
    gk_t = jnp.broadcast_to(p["gk"][:, None], (HEAD_DIM, tm))

    def tile3(last):
        return pl.BlockSpec((1, tm, last), lambda b, t: (b, t, 0))

    in_specs = [
        tile3(D_MODEL), _const_spec(meta.shape),
        _const_spec((1, D_MODEL)),
        pl.BlockSpec(memory_space=pl.ANY), pl.BlockSpec(memory_space=pl.ANY),
        _const_spec((1, D_MODEL)), _const_spec((F_ROWS + D_ATTN, D_MODEL)),
        _const_spec((D_ATTN, D_MODEL)), _const_spec((D_ATTN, D_MODEL)),
        _const_spec((D_MODEL, 2 * D_CONV)), _const_spec((D_MODEL, D_CONV)),
        _const_spec((F_ROWS, 1)), _const_spec((tm, tm)),
        _const_spec((HEAD_DIM, tm)), _const_spec((HEAD_DIM, tm)),
        _const_spec((SUBLANES, D_CONV)), _const_spec((1, D_CONV)),
    ]
    out_shape = [
        jax.ShapeDtypeStruct((nb, seq, D_MODEL), F32),
        jax.ShapeDtypeStruct((nb, HEADS, AUG_DIM, seq), BF16),
        jax.ShapeDtypeStruct((nb, HEADS, seq, AUG_DIM), BF16),
        jax.ShapeDtypeStruct((nb, HEADS, HEAD_DIM, seq), BF16),
        jax.ShapeDtypeStruct((nb, HEADS, seq), F32),
        jax.ShapeDtypeStruct((nb, seq, D_CONV), BF16),
        jax.ShapeDtypeStruct((HEADS, N_META, AUG_DIM), BF16),
        jax.ShapeDtypeStruct((HEADS, HEAD_DIM, N_META), BF16),
        jax.ShapeDtypeStruct((F_ROWS, LANES), F32),
    ]
    out_specs = [
        tile3(D_MODEL),
        pl.BlockSpec((1, HEADS, AUG_DIM, tm), lambda b, t: (b, 0, 0, t)),
        pl.BlockSpec((1, HEADS, tm, AUG_DIM), lambda b, t: (b, 0, t, 0)),
        pl.BlockSpec((1, HEADS, HEAD_DIM, tm), lambda b, t: (b, 0, 0, t)),
        pl.BlockSpec((1, HEADS, tm), lambda b, t: (b, 0, t)),
        tile3(D_CONV),
        pl.BlockSpec((HEADS, N_META, AUG_DIM), lambda b, t: (0, 0, 0)),
        pl.BlockSpec((HEADS, HEAD_DIM, N_META), lambda b, t: (0, 0, 0)),
        pl.BlockSpec((F_ROWS, LANES), lambda b, t: (0, 0)),
    ]
    scratch_shapes = [
        pltpu.VMEM((tm, D_MODEL), BF16),
        pltpu.VMEM((tm, D_MODEL), F32),
        pltpu.VMEM((tm, D_FF), BF16),
        pltpu.VMEM((SUBLANES, D_CONV), F32),
        pltpu.VMEM((F_ROWS, LANES), F32),
        pltpu.VMEM((SUBLANES, D_CONV), F32),
        pltpu.VMEM((F_ROWS, LANES), F32),
    ] + _ffn_weight_scratch()
    return pl.pallas_call(
        _ffn1_mix_in_kernel, out_shape=out_shape, grid=(nb, nt), in_specs=in_specs, out_specs=out_specs,
        scratch_shapes=scratch_shapes, name="ffn1_mix_in",
        compiler_params=pltpu.CompilerParams(
            dimension_semantics=("arbitrary", "arbitrary"), vmem_limit_bytes=VMEM_LIMIT_BYTES),
    )(x, meta,
      p["g1"], p["win1"], p["wout1"],
      p["gmix"], p["wfkT"], p["wqT"], p["wvT"], p["wcu"], p["wb"], p["bf"], tri,
      gq_t, gk_t, p["cw"], p["gconv"])


def _fox_attention_kernel(fq0_ref, flast_ref, fmeta_ref, cb_ref,
                          qT_ref, k_ref, vT_ref, km_ref, vmT_ref,
                          o_ref,
                          acc_ref, l_ref):
    bh0 = pl.program_id(0) * HEADS + pl.program_id(1) * HEAD_GROUP
    heads = range(HEAD_GROUP)
    nq = qT_ref.shape[3] // Q_BLOCK
    c_bound = cb_ref[0]
    key_idx = lax.broadcasted_iota(jnp.int32, (K_BLOCK, Q_BLOCK), 0)
    qry_idx = lax.broadcasted_iota(jnp.int32, (K_BLOCK, Q_BLOCK), 1)
    causal = key_idx <= qry_idx

    def block_live(g, i, j):
        return fq0_ref[bh0 + g, i] - flast_ref[bh0 + g, jnp.maximum(j, 0)] + c_bound >= EXP_UNDERFLOW

    def meta_live(g, i):
        head = pl.program_id(1) * HEAD_GROUP + g
        return fq0_ref[bh0 + g, i] - fmeta_ref[head] + c_bound >= EXP_UNDERFLOW

    def sublane_partial_sum(p):
        return jnp.sum(p.reshape(p.shape[0] // SUBLANES, SUBLANES, p.shape[1]), axis=0)

    def bounded_tile(k_blk, v_t, q_t):
        p = jnp.exp2(_dot(k_blk, q_t))
        return sublane_partial_sum(p), _dot(v_t, p.astype(BF16))

    def bounded_q_blocks(i_first):
        chains = [(r, g) for r in range(Q_PER_STEP) for g in heads]
        blk = [i_first + r for r in range(Q_PER_STEP)]
        q0 = [pl.multiple_of(i * Q_BLOCK, Q_BLOCK) for i in blk]
        p0 = [pl.multiple_of(jnp.maximum(i - 1, 0) * K_BLOCK, K_BLOCK) for i in blk]
        first_mask = jnp.where(i_first >= 1, 0.0, MASKED)

        def weights(group):
            q_t = [qT_ref[0, g, :, pl.ds(q0[r], Q_BLOCK)] for r, g in group]
            s_d = [_dot(k_ref[0, g, pl.ds(q0[r], K_BLOCK), :], q) for (r, g), q in zip(group, q_t)]
            s_p = [_dot(k_ref[0, g, pl.ds(p0[r], K_BLOCK), :], q) for (r, g), q in zip(group, q_t)]
            return ([jnp.exp2(jnp.where(causal, s, MASKED)) for s in s_d],
                    [jnp.exp2(s + first_mask if r == 0 else s) for s, (r, g) in zip(s_p, group)])

        def values(group, p):
            for (r, g), p_d, p_p in zip(group, *p):
                c = r * HEAD_GROUP + g
                l_ref[c] = sublane_partial_sum(p_d) + sublane_partial_sum(p_p)
                acc_ref[c] = (_dot(vT_ref[0, g, :, pl.ds(q0[r], K_BLOCK)], p_d.astype(BF16))
                              + _dot(vT_ref[0, g, :, pl.ds(p0[r], K_BLOCK)], p_p.astype(BF16)))

        groups = [chains[n:n + CHAIN_GROUP] for n in range(0, len(chains), CHAIN_GROUP)]
        pending = []
        for n, group in enumerate(groups):
            pending.append((group, weights(group)))
            if n >= VALUE_DOT_LAG:
                values(*pending.pop(0))
        for item in pending:
            values(*item)

        def any_head(test):
            hit = test(0)
            for g in heads[1:]:
                hit = jnp.logical_or(hit, test(g))
            return hit

        def walk(r):
            i = blk[r]

            def cond(j):
                return jnp.logical_and(j >= 0, any_head(lambda g: block_live(g, i, j)))

            def body(j):
                k0 = pl.multiple_of(j * K_BLOCK, K_BLOCK)
                for g in heads:
                    c = r * HEAD_GROUP + g
                    l_j, acc_j = bounded_tile(k_ref[0, g, pl.ds(k0, K_BLOCK), :],
                                              vT_ref[0, g, :, pl.ds(k0, K_BLOCK)],
                                              qT_ref[0, g, :, pl.ds(q0[r], Q_BLOCK)])
                    l_ref[c] += l_j
                    acc_ref[c] += acc_j
                return j - 1

            lax.while_loop(cond, body, i - 2)

            @pl.when(any_head(lambda g: meta_live(g, i)))
            def _():
                for g in heads:
                    c = r * HEAD_GROUP + g
                    l_m, acc_m = bounded_tile(km_ref[g], vmT_ref[g], qT_ref[0, g, :, pl.ds(q0[r], Q_BLOCK)])
                    l_ref[c] += l_m
                    acc_ref[c] += acc_m

        walks = [jnp.logical_or(jnp.logical_and(i >= 2, any_head(lambda g, i=i: block_live(g, i, i - 2))),
                                any_head(lambda g, i=i: meta_live(g, i))) for i in blk]
        any_walk = walks[0]
        for hit in walks[1:]:
            any_walk = jnp.logical_or(any_walk, hit)

        @pl.when(any_walk)
        def _():
            for r in range(Q_PER_STEP):
                walk(r)

        for c, (r, g) in enumerate(chains):
            l_tot = jnp.sum(l_ref[c], axis=0, keepdims=True)
            o_ref[0, g, :, pl.ds(q0[r], Q_BLOCK)] = (acc_ref[c] / l_tot).astype(o_ref.dtype)

    @pl.when(c_bound <= BOUNDED_LOGIT_MAX)
    def _():
        def step(n, carry):
            bounded_q_blocks(n * Q_PER_STEP)
            return carry

        lax.fori_loop(0, nq // Q_PER_STEP, step, 0)

    def online_q_block(g, i):
        q0 = pl.multiple_of(i * Q_BLOCK, Q_BLOCK)
        q_t = qT_ref[0, g, :, pl.ds(q0, Q_BLOCK)]
        s_m = _dot(km_ref[g], q_t)
        m = jnp.max(s_m, axis=0, keepdims=True)
        p_m = jnp.exp2(s_m - m)
        l = jnp.sum(p_m, axis=0, keepdims=True)
        acc = _dot(vmT_ref[g], p_m.astype(BF16))

        def online_step(state, k0, masked):
            m, l, acc = state
            s = _dot(k_ref[0, g, pl.ds(k0, K_BLOCK), :], q_t)
            if masked:
                s = jnp.where(causal, s, MASKED)
            m_new = jnp.maximum(m, jnp.max(s, axis=0, keepdims=True))
            alpha = jnp.exp2(m - m_new)
            p = jnp.exp2(s - m_new)
            l = alpha * l + jnp.sum(p, axis=0, keepdims=True)
            acc = alpha * acc + _dot(vT_ref[0, g, :, pl.ds(k0, K_BLOCK)], p.astype(BF16))
            return m_new, l, acc

        state = online_step((m, l, acc), q0, True)

        def cond(carry):
            return jnp.logical_and(carry[0] >= 0, block_live(g, i, carry[0]))

        def body(carry):
            j = carry[0]
            return (j - 1,) + online_step(carry[1:], pl.multiple_of(j * K_BLOCK, K_BLOCK), False)

        _, m, l, acc = lax.while_loop(cond, body, (i - 1,) + state)
        o_ref[0, g, :, pl.ds(q0, Q_BLOCK)] = (acc / l).astype(o_ref.dtype)

    @pl.when(c_bound > BOUNDED_LOGIT_MAX)
    def _():
        def head_loop(g, carry):
            def step(i, inner):
                online_q_block(g, i)
                return inner

            lax.fori_loop(0, nq, step, 0)
            return carry

        lax.fori_loop(0, HEAD_GROUP, head_loop, 0)


def _fox_attention(qT, k, vT, k_meta, vT_meta, fq0, flast, fmeta, c_bound):
    nb, _, _, seq = qT.shape
    assert seq % (Q_PER_STEP * Q_BLOCK) == 0 and Q_BLOCK == K_BLOCK
    smem = pl.BlockSpec(memory_space=pltpu.SMEM)
    hg = HEAD_GROUP
    return pl.pallas_call(
        _fox_attention_kernel,
        out_shape=jax.ShapeDtypeStruct((nb, HEADS, HEAD_DIM, seq), BF16),
        grid=(nb, HEADS // hg),
        in_specs=[
            smem, smem, smem, smem,
            pl.BlockSpec((1, hg, AUG_DIM, seq), lambda b, h: (b, h, 0, 0)),
            pl.BlockSpec((1, hg, seq, AUG_DIM), lambda b, h: (b, h, 0, 0)),
            pl.BlockSpec((1, hg, HEAD_DIM, seq), lambda b, h: (b, h, 0, 0)),
            pl.BlockSpec((hg, N_META, AUG_DIM), lambda b, h: (h, 0, 0)),
            pl.BlockSpec((hg, HEAD_DIM, N_META), lambda b, h: (h, 0, 0)),
        ],
        out_specs=pl.BlockSpec((1, hg, HEAD_DIM, seq), lambda b, h: (b, h, 0, 0)),
        scratch_shapes=[pltpu.VMEM((Q_PER_STEP * hg, HEAD_DIM, Q_BLOCK), F32),
                        pltpu.VMEM((Q_PER_STEP * hg, SUBLANES, Q_BLOCK), F32)],
        name="fox_attention",
        compiler_params=pltpu.CompilerParams(
            dimension_semantics=("arbitrary", "arbitrary"), vmem_limit_bytes=VMEM_LIMIT_BYTES),
    )(fq0, flast, fmeta, c_bound, qT, k, vT, k_meta, vT_meta)


def _mix_out_ffn2_kernel(h1_ref, oT_ref, oconv_ref,
                         gattn_ref, woa_ref, woc_ref,
                         g2_ref, win_hbm, wout_hbm, gfin_ref,
                         out_ref,
                         xn_ref, acc_ref, act_ref, win_ref, wout_ref, win_stage, wout_stage, win_sem, wout_sem):
    @pl.when(jnp.logical_and(pl.program_id(0) == 0, pl.program_id(1) == 0))
    def _():
        _load_weight_as_bf16(win_hbm, win_ref, win_stage, win_sem)
        _load_weight_as_bf16(wout_hbm, wout_ref, wout_stage, wout_sem, MACARON_SCALE)

    o_t = oT_ref[0].astype(F32)
    ms = jnp.mean(o_t * o_t, axis=0, keepdims=True)
    o_n = (o_t * lax.rsqrt(ms + EPS) * gattn_ref[...]).T.astype(BF16)
    mix = _dot(o_n, woa_ref[...]) + _dot(oconv_ref[0], woc_ref[...])
    h2 = h1_ref[0] + mix
    xn_ref[...] = _rmsnorm_rows(h2, g2_ref[...]).astype(BF16)
    _swiglu(xn_ref, win_ref, wout_ref, acc_ref, act_ref, h2)
    out_ref[0] = _rmsnorm_rows(acc_ref[...], gfin_ref[...])


def _mix_out_ffn2(h1, o_t, oconv, p, tm):
    nb, seq, _ = h1.shape
    gattn_t = jnp.broadcast_to(p["gattn"][:, None], (D_ATTN, tm))
    in_specs = [
        pl.BlockSpec((1, tm, D_MODEL), lambda b, t: (b, t, 0)),
        pl.BlockSpec((1, D_ATTN, tm), lambda b, t: (b, 0, t)),
        pl.BlockSpec((1, tm, D_CONV), lambda b, t: (b, t, 0)),
        _const_spec((D_ATTN, tm)), _const_spec((D_ATTN, D_MODEL)), _const_spec((D_CONV, D_MODEL)),
        _const_spec((1, D_MODEL)),
        pl.BlockSpec(memory_space=pl.ANY), pl.BlockSpec(memory_space=pl.ANY),
        _const_spec((1, D_MODEL)),
    ]
    return pl.pallas_call(
        _mix_out_ffn2_kernel,
        out_shape=jax.ShapeDtypeStruct((nb, seq, D_MODEL), F32),
        grid=(nb, seq // tm),
        in_specs=in_specs,
        out_specs=pl.BlockSpec((1, tm, D_MODEL), lambda b, t: (b, t, 0)),
        scratch_shapes=[pltpu.VMEM((tm, D_MODEL), BF16), pltpu.VMEM((tm, D_MODEL), F32),
                        pltpu.VMEM((tm, D_FF), BF16)] + _ffn_weight_scratch(),
        name="mix_out_ffn2",
        compiler_params=pltpu.CompilerParams(
            dimension_semantics=("arbitrary", "arbitrary"), vmem_limit_bytes=VMEM_LIMIT_BYTES),
    )(h1, o_t, oconv, gattn_t, p["woa"], p["woc"], p["g2"], p["win2"], p["wout2"], p["gfin"])


def kernel(x, meta_tokens, ffn1_norm, ffn1_w_in, ffn1_w_out, mix_norm, w_mix_in, b_forget, q_norm, k_norm, conv_w, attn_out_norm, conv_out_norm, w_mix_out, ffn2_norm, ffn2_w_in, ffn2_w_out, final_norm):
    nb, seq, _ = x.shape
    wmix = w_mix_in[0]
    n_qkv = 3 * D_ATTN
    p = {
        "g1": ffn1_norm, "win1": ffn1_w_in[0], "wout1": ffn1_w_out[0],
        "gmix": mix_norm,
        "wfkT": jnp.concatenate([jnp.pad(wmix[:, n_qkv:n_qkv + HEADS].T, ((0, F_ROWS - HEADS), (0, 0))),
                                 wmix[:, D_ATTN:2 * D_ATTN].T], axis=0).astype(BF16),
        "wqT": wmix[:, :D_ATTN].T.astype(BF16),
        "wvT": wmix[:, 2 * D_ATTN:n_qkv].T.astype(BF16),
        "wcu": wmix[:, n_qkv + HEADS + D_CONV:].astype(BF16),
        "wb": wmix[:, n_qkv + HEADS:n_qkv + HEADS + D_CONV].astype(BF16),
        "bf": jnp.pad(b_forget[0], (0, F_ROWS - HEADS))[:, None],
        "gq": q_norm[0], "gk": k_norm[0],
        "cw": jnp.pad(conv_w[0], ((0, SUBLANES - CONV_WIDTH), (0, 0))),
        "gconv": conv_out_norm,
        "gattn": attn_out_norm[0],
        "woa": w_mix_out[0, :D_ATTN].astype(BF16), "woc": w_mix_out[0, D_ATTN:].astype(BF16),
        "g2": ffn2_norm, "win2": ffn2_w_in[0], "wout2": ffn2_w_out[0],
        "gfin": final_norm,
    }

    meta = jnp.pad(meta_tokens.astype(x.dtype), ((0, META_TILE - N_META), (0, 0)))
    h1, qT, k, vT, f_cum, oconv, k_meta, vT_meta, f_meta = _ffn1_mix_in(x, meta, p, TOKEN_TILE)

    fq0 = f_cum[:, :, 0::Q_BLOCK].reshape(nb * HEADS, seq // Q_BLOCK)
    flast = f_cum[:, :, K_BLOCK - 1::K_BLOCK].reshape(nb * HEADS, seq // K_BLOCK)
    c_bound = (2.0 * HEAD_DIM ** 0.5 * SCORE_BOUND_SLACK * jnp.max(jnp.abs(q_norm)) * jnp.max(jnp.abs(k_norm))
               + SCORE_BOUND_MARGIN).reshape(1).astype(F32)

    o_t = _fox_attention(qT, k, vT, k_meta, vT_meta, fq0, flast, f_meta[:HEADS, 0], c_bound)
    o_t = o_t.reshape(nb, D_ATTN, seq)
    return _mix_out_ffn2(h1, o_t, oconv, p, TOKEN_TILE)
```

```python
from typing import Any, NamedTuple

import jax
import jax.numpy as jnp
from jax import lax
from jax.experimental import pallas as pl
from jax.experimental.pallas import tpu as pltpu

D_MODEL = 1024
N_META = 16
D_ATTN = 512
D_CONV = 512
HEADS = 8
HEAD_DIM = 64
CONV_WIDTH = 3
D_FF = 2816
EPS = 1e-6

F32 = jnp.float32
BF16 = jnp.bfloat16

LANES = 128
SUBLANES = 8
MXU_DIM = 256
VMEM_LIMIT_BYTES = 60000 * 1024

TOKEN_TILE = 512
META_TILE = LANES
FF_CHUNK = MXU_DIM
N_FF_CHUNKS = D_FF // FF_CHUNK
Q_BLOCK = MXU_DIM
K_BLOCK = MXU_DIM
AUG_DIM = LANES
HEAD_GROUP = 4
Q_PER_STEP = 8
CHAIN_GROUP = 2
VALUE_DOT_LAG = 2
MASKED = -1e30
LOG2E = 1.4426950408889634
F_ROWS = 2 * SUBLANES
BIAS_PIECES = 3
SCORE_BOUND_SLACK = 1.02
SCORE_BOUND_MARGIN = 1.0
MACARON_SCALE = 0.5
WIN_STAGE_ROWS = 64
WOUT_STAGE_ROWS = 256
STAGE_SLOTS = 4

EXP_UNDERFLOW = -104.0
BOUNDED_LOGIT_MAX = 120.0

assert D_FF % FF_CHUNK == 0 and FF_CHUNK % LANES == 0 and HEADS % HEAD_GROUP == 0


def _dot(a, b):
    return jnp.dot(a, b, preferred_element_type=F32)


def _dot_nt(a, b):
    return lax.dot_general(a, b, (((1,), (1,)), ((), ())), preferred_element_type=F32)


def _rmsnorm_rows(x, gain):
    ms = jnp.mean(x * x, axis=-1, keepdims=True)
    return x * lax.rsqrt(ms + EPS) * gain


def _split3(x):
    hi = x.astype(BF16)
    r1 = x - hi.astype(F32)
    mid = r1.astype(BF16)
    lo = (r1 - mid.astype(F32)).astype(BF16)
    return hi, mid, lo


def _load_weight_as_bf16(w_hbm, w_bf, stage, sem, scale=None):
    n_slots, rows = stage.shape[0], stage.shape[1]
    n_chunks = w_hbm.shape[0] // rows
    assert n_chunks * rows == w_hbm.shape[0]

    def chunk_copy(i):
        slot = i % n_slots
        return pltpu.make_async_copy(w_hbm.at[pl.ds(i * rows, rows)], stage.at[slot], sem.at[slot])

    for i in range(min(n_slots - 1, n_chunks)):
        chunk_copy(i).start()
    for i in range(n_chunks):
        if i + n_slots - 1 < n_chunks:
            chunk_copy(i + n_slots - 1).start()
        chunk_copy(i).wait()
        w = stage[i % n_slots]
        w_bf[i * rows:(i + 1) * rows, :] = (w if scale is None else w * scale).astype(BF16)


def _swiglu(xn_ref, win_ref, wout_ref, acc_ref, act_ref, residual):
    for c in range(N_FF_CHUNKS):
        lo, hi = c * FF_CHUNK, (c + 1) * FF_CHUNK
        xn = xn_ref[...]
        g = _dot(xn, win_ref[:, lo:hi])
        u = _dot(xn, win_ref[:, D_FF + lo:D_FF + hi])
        act_ref[:, lo:hi] = (g * jax.nn.sigmoid(g) * u).astype(BF16)
    acc_ref[...] = residual + _dot(act_ref[...], wout_ref[...])


class _MixWeights(NamedTuple):
    g1: Any
    win: Any
    wout: Any
    gmix: Any
    wfkT: Any
    wqT: Any
    wvT: Any
    wcu: Any
    wb: Any
    bf: Any
    tri: Any
    gq: Any
    gk: Any
    cw: Any
    gconv: Any


class _TileOut(NamedTuple):
    h1: Any
    q: Any
    k: Any
    v: Any
    f: Any
    oconv: Any


def _token_tile(x, n_valid, w, xn_ref, acc_ref, act_ref, zc_ref, fc_ref, out):
    tm = x.shape[0]

    xn_ref[...] = _rmsnorm_rows(x, w.g1[...]).astype(BF16)
    _swiglu(xn_ref, w.win, w.wout, acc_ref, act_ref, x)
    h1 = acc_ref[...]
    if out.h1 is not None:
        out.h1[0] = h1

    xn2 = _rmsnorm_rows(h1, w.gmix[...]).astype(BF16)
    fkT = _dot_nt(w.wfkT[...], xn2)
    cu = _dot(xn2, w.wcu[...])

    fl = fkT[0:F_ROWS] + w.bf[...]
    logf = jnp.minimum(fl, 0.0) - jnp.log(1.0 + jnp.exp(-jnp.abs(fl)))
    if n_valid < tm:
        logf = jnp.where(lax.broadcasted_iota(jnp.int32, logf.shape, 1) < n_valid, logf, 0.0)
    pieces = _dot(jnp.concatenate(_split3(logf), axis=0), w.tri[0:tm, 0:tm])
    csum = pieces[0:F_ROWS] + pieces[F_ROWS:2 * F_ROWS] + pieces[2 * F_ROWS:3 * F_ROWS]
    f_all = csum + jnp.concatenate([fc_ref[...]] * (tm // LANES), axis=1)
    fc_ref[...] = jnp.broadcast_to(f_all[:, tm - 1:tm], fc_ref.shape)
    if out.f is not None:
        out.f[0] = f_all[:HEADS]

    f_hi, f_mid, f_lo = (p.astype(F32) for p in _split3(f_all * LOG2E))
    row = lax.broadcasted_iota(jnp.int32, (SUBLANES, tm), 0)
    ones_mid = jnp.where(row < 2 * BIAS_PIECES, 1.0, 0.0)

    def bias_rows(pieces, first_row, sign):
        rows = ones_mid
        for i, piece in enumerate(pieces):
            rows = jnp.where(row == first_row + i, sign * piece, rows)
        return rows
    pad_rows = jnp.zeros((AUG_DIM - HEAD_DIM - SUBLANES, tm), F32)

    def head_pieces(h):
        return (jnp.broadcast_to(f_hi[h:h + 1], (SUBLANES, tm)),
                jnp.broadcast_to(f_mid[h:h + 1], (SUBLANES, tm)),
                jnp.broadcast_to(f_lo[h:h + 1], (SUBLANES, tm)))

    def head_rmsnorm(x_t, gain_t):
        x3 = x_t.reshape(HEADS, HEAD_DIM, tm)
        return x3 * lax.rsqrt(jnp.mean(x3 * x3, axis=1, keepdims=True) + EPS) * gain_t[None]

    kn = head_rmsnorm(fkT[F_ROWS:F_ROWS + D_ATTN], w.gk[:, 0:tm])
    if out.q is not None:
        qT = _dot_nt(w.wqT[...], xn2)
    if out.oconv is not None:
        gate_b = _dot(xn2, w.wb[...])
    for h in range(HEADS):
        aug_k = bias_rows(head_pieces(h), BIAS_PIECES, -1.0)
        k_aug_t = jnp.concatenate([kn[h], aug_k, pad_rows], axis=0)
        out.k(h, k_aug_t.T.astype(BF16))

    vT = _dot_nt(w.wvT[...], xn2)
    z = cu[:, 0:D_CONV] * cu[:, D_CONV:2 * D_CONV]
    if out.oconv is not None:
        zc = zc_ref[...]
        rowz = lax.broadcasted_iota(jnp.int32, z.shape, 0)
        prev1 = jnp.broadcast_to(zc[SUBLANES - 1:SUBLANES], z.shape)
        prev2 = jnp.broadcast_to(zc[SUBLANES - 2:SUBLANES - 1], z.shape)
        z1 = jnp.where(rowz == 0, prev1, pltpu.roll(z, 1, axis=0))
        z2 = jnp.where(rowz == 0, prev2, jnp.where(rowz == 1, prev1, pltpu.roll(z, 2, axis=0)))
        cw = w.cw[...]
        y = cw[0:1] * z2 + cw[1:2] * z1 + cw[2:3] * z
        out.oconv[0] = _rmsnorm_rows(gate_b * y, w.gconv[...]).astype(BF16)
    zc_ref[...] = z[n_valid - SUBLANES:n_valid]

    if out.q is not None:
        qn = head_rmsnorm(qT, w.gq[:, 0:tm] * (HEAD_DIM ** -0.5 * LOG2E))
    v3 = vT.reshape(HEADS, HEAD_DIM, tm)
    for h in range(HEADS):
        if out.q is not None:
            aug_q = bias_rows(head_pieces(h), 0, 1.0)
            out.q[0, h] = jnp.concatenate([qn[h], aug_q, pad_rows], axis=0).astype(BF16)
        out.v(h, v3[h].astype(BF16))


def _ffn1_mix_in_kernel(
        x_ref, meta_ref,
        g1_ref, win_hbm, wout_hbm,
        gmix_ref, wfkT_ref, wqT_ref, wvT_ref, wcu_ref, wb_ref, bf_ref, tri_ref,
        gq_ref, gk_ref, cw_ref, gconv_ref,
        h1_ref, qT_ref, k_ref, vT_ref, f_ref, oconv_ref, km_ref, vmT_ref, fmeta_ref,
        xn_ref, acc_ref, act_ref, zc_ref, fc_ref, zc0_ref, fc0_ref,
        win_ref, wout_ref, win_stage, wout_stage, win_sem, wout_sem):
    t = pl.program_id(1)
    w = _MixWeights(g1_ref, win_ref, wout_ref, gmix_ref, wfkT_ref, wqT_ref, wvT_ref, wcu_ref, wb_ref,
                    bf_ref, tri_ref, gq_ref, gk_ref, cw_ref, gconv_ref)

    def put_k(h, k_aug):
        k_ref[0, h] = k_aug

    def put_v(h, v_t):
        vT_ref[0, h] = v_t

    def put_meta_k(h, k_aug):
        km_ref[h] = k_aug[0:N_META]

    def put_meta_v(h, v_t):
        vmT_ref[h] = v_t[:, 0:N_META]

    @pl.when(jnp.logical_and(pl.program_id(0) == 0, t == 0))
    def _():
        _load_weight_as_bf16(win_hbm, win_ref, win_stage, win_sem)
        _load_weight_as_bf16(wout_hbm, wout_ref, wout_stage, wout_sem, MACARON_SCALE)
        rows = meta_ref.shape[0]
        zc_ref[...] = jnp.zeros_like(zc_ref)
        fc_ref[...] = jnp.zeros_like(fc_ref)
        _token_tile(meta_ref[...], N_META, w, xn_ref.at[pl.ds(0, rows)], acc_ref.at[pl.ds(0, rows)],
                    act_ref.at[pl.ds(0, rows)], zc_ref, fc_ref, _TileOut(None, None, put_meta_k, put_meta_v, None, None))
        zc0_ref[...] = zc_ref[...]
        fc0_ref[...] = fc_ref[...]
        fmeta_ref[...] = fc_ref[...]

    @pl.when(t == 0)
    def _():
        zc_ref[...] = zc0_ref[...]
        fc_ref[...] = fc0_ref[...]

    _token_tile(x_ref[0], x_ref.shape[1], w, xn_ref, acc_ref, act_ref, zc_ref, fc_ref,
                _TileOut(h1_ref, qT_ref, put_k, put_v, f_ref, oconv_ref))


def _ffn_weight_scratch():
    return [pltpu.VMEM((D_MODEL, 2 * D_FF), BF16), pltpu.VMEM((D_FF, D_MODEL), BF16),
            pltpu.VMEM((STAGE_SLOTS, WIN_STAGE_ROWS, 2 * D_FF), F32),
            pltpu.VMEM((STAGE_SLOTS, WOUT_STAGE_ROWS, D_MODEL), F32),
            pltpu.SemaphoreType.DMA((STAGE_SLOTS,)), pltpu.SemaphoreType.DMA((STAGE_SLOTS,))]


def _const_spec(shape):
    nd = len(shape)
    return pl.BlockSpec(shape, lambda *_: (0,) * nd, pipeline_mode=pl.Buffered(1))


def _ffn1_mix_in(x, meta, p, tm):
    nb, seq, _ = x.shape
    nt = seq // tm
    assert tm % LANES == 0 and meta.shape[0] <= tm
    tri = jnp.triu(jnp.ones((tm, tm), BF16))
    gq_t = jnp.broadcast_to(p["gq"][:, None], (HEAD_DIM, tm))
    gk_t = jnp.broadcast_to(p["gk"][:, None], (HEAD_DIM, tm))

    def tile3(last):
        return pl.BlockSpec((1, tm, last), lambda b, t: (b, t, 0))

    in_specs = [
        tile3(D_MODEL), _const_spec(meta.shape),
        _const_spec((1, D_MODEL)),
        pl.BlockSpec(memory_space=pl.ANY), pl.BlockSpec(memory_space=pl.ANY),
        _const_spec((1, D_MODEL)), _const_spec((F_ROWS + D_ATTN, D_MODEL)),
        _const_spec((D_ATTN, D_MODEL)), _const_spec((D_ATTN, D_MODEL)),
        _const_spec((D_MODEL, 2 * D_CONV)), _const_spec((D_MODEL, D_CONV)),
        _const_spec((F_ROWS, 1)), _const_spec((tm, tm)),
        _const_spec((HEAD_DIM, tm)), _const_spec((HEAD_DIM, tm)),
        _const_spec((SUBLANES, D_CONV)), _const_spec((1, D_CONV)),
    ]
    out_shape = [
        jax.ShapeDtypeStruct((nb, seq, D_MODEL), F32),
        jax.ShapeDtypeStruct((nb, HEADS, AUG_DIM, seq), BF16),
        jax.ShapeDtypeStruct((nb, HEADS, seq, AUG_DIM), BF16),
        jax.ShapeDtypeStruct((nb, HEADS, HEAD_DIM, seq), BF16),
        jax.ShapeDtypeStruct((nb, HEADS, seq), F32),
        jax.ShapeDtypeStruct((nb, seq, D_CONV), BF16),
        jax.ShapeDtypeStruct((HEADS, N_META, AUG_DIM), BF16),
        jax.ShapeDtypeStruct((HEADS, HEAD_DIM, N_META), BF16),
        jax.ShapeDtypeStruct((F_ROWS, LANES), F32),
    ]
    out_specs = [
        tile3(D_MODEL),
        pl.BlockSpec((1, HEADS, AUG_DIM, tm), lambda b, t: (b, 0, 0, t)),
        pl.BlockSpec((1, HEADS, tm, AUG_DIM), lambda b, t: (b, 0, t, 0)),
        pl.BlockSpec((1, HEADS, HEAD_DIM, tm), lambda b, t: (b, 0, 0, t)),
        pl.BlockSpec((1, HEADS, tm), lambda b, t: (b, 0, t)),
        tile3(D_CONV),
        pl.BlockSpec((HEADS, N_META, AUG_DIM), lambda b, t: (0, 0, 0)),
        pl.BlockSpec((HEADS, HEAD_DIM, N_META), lambda b, t: (0, 0, 0)),
        pl.BlockSpec((F_ROWS, LANES), lambda b, t: (0, 0)),
    ]
    scratch_shapes = [
        pltpu.VMEM((tm, D_MODEL), BF16),
        pltpu.VMEM((tm, D_MODEL), F32),
        pltpu.VMEM((tm, D_FF), BF16),
        pltpu.VMEM((SUBLANES, D_CONV), F32),
        pltpu.VMEM((F_ROWS, LANES), F32),
        pltpu.VMEM((SUBLANES, D_CONV), F32),
        pltpu.VMEM((F_ROWS, LANES), F32),
    ] + _ffn_weight_scratch()
    return pl.pallas_call(
        _ffn1_mix_in_kernel, out_shape=out_shape, grid=(nb, nt), in_specs=in_specs, out_specs=out_specs,
        scratch_shapes=scratch_shapes, name="ffn1_mix_in",
        compiler_params=pltpu.CompilerParams(
            dimension_semantics=("arbitrary", "arbitrary"), vmem_limit_bytes=VMEM_LIMIT_BYTES),
    )(x, meta,
      p["g1"], p["win1"], p["wout1"],
      p["gmix"], p["wfkT"], p["wqT"], p["wvT"], p["wcu"], p["wb"], p["bf"], tri,
      gq_t, gk_t, p["cw"], p["gconv"])


def _fox_attention_kernel(fq0_ref, flast_ref, fmeta_ref, cb_ref,
                          qT_ref, k_ref, vT_ref, km_ref, vmT_ref,
                          o_ref,
                          acc_ref, l_ref):
    bh0 = pl.program_id(0) * HEADS + pl.program_id(1) * HEAD_GROUP
    heads = range(HEAD_GROUP)
    nq = qT_ref.shape[3] // Q_BLOCK
    c_bound = cb_ref[0]
    key_idx = lax.broadcasted_iota(jnp.int32, (K_BLOCK, Q_BLOCK), 0)
    qry_idx = lax.broadcasted_iota(jnp.int32, (K_BLOCK, Q_BLOCK), 1)
    causal = key_idx <= qry_idx

    def block_live(g, i, j):
        return fq0_ref[bh0 + g, i] - flast_ref[bh0 + g, jnp.maximum(j, 0)] + c_bound >= EXP_UNDERFLOW

    def meta_live(g, i):
        head = pl.program_id(1) * HEAD_GROUP + g
        return fq0_ref[bh0 + g, i] - fmeta_ref[head] + c_bound >= EXP_UNDERFLOW

    def sublane_partial_sum(p):
        return jnp.sum(p.reshape(p.shape[0] // SUBLANES, SUBLANES, p.shape[1]), axis=0)

    def bounded_tile(k_blk, v_t, q_t):
        p = jnp.exp2(_dot(k_blk, q_t))
        return sublane_partial_sum(p), _dot(v_t, p.astype(BF16))

    def bounded_q_blocks(i_first):
        chains = [(r, g) for r in range(Q_PER_STEP) for g in heads]
        blk = [i_first + r for r in range(Q_PER_STEP)]
        q0 = [pl.multiple_of(i * Q_BLOCK, Q_BLOCK) for i in blk]
        p0 = [pl.multiple_of(jnp.maximum(i - 1, 0) * K_BLOCK, K_BLOCK) for i in blk]
        first_mask = jnp.where(i_first >= 1, 0.0, MASKED)

        def weights(group):
            q_t = [qT_ref[0, g, :, pl.ds(q0[r], Q_BLOCK)] for r, g in group]
            s_d = [_dot(k_ref[0, g, pl.ds(q0[r], K_BLOCK), :], q) for (r, g), q in zip(group, q_t)]
            s_p = [_dot(k_ref[0, g, pl.ds(p0[r], K_BLOCK), :], q) for (r, g), q in zip(group, q_t)]
            return ([jnp.exp2(jnp.where(causal, s, MASKED)) for s in s_d],
                    [jnp.exp2(s + first_mask if r == 0 else s) for s, (r, g) in zip(s_p, group)])

        def values(group, p):
            for (r, g), p_d, p_p in zip(group, *p):
                c = r * HEAD_GROUP + g
                l_ref[c] = sublane_partial_sum(p_d) + sublane_partial_sum(p_p)
                acc_ref[c] = (_dot(vT_ref[0, g, :, pl.ds(q0[r], K_BLOCK)], p_d.astype(BF16))
                              + _dot(vT_ref[0, g, :, pl.ds(p0[r], K_BLOCK)], p_p.astype(BF16)))

        groups = [chains[n:n + CHAIN_GROUP] for n in range(0, len(chains), CHAIN_GROUP)]
        pending = []
        for n, group in enumerate(groups):
            pending.append((group, weights(group)))
            if n >= VALUE_DOT_LAG:
                values(*pending.pop(0))
        for item in pending:
            values(*item)

        def any_head(test):
            hit = test(0)
            for g in heads[1:]:
                hit = jnp.logical_or(hit, test(g))
            return hit

        def walk(r):
            i = blk[r]

            def cond(j):
                return jnp.logical_and(j >= 0, any_head(lambda g: block_live(g, i, j)))

            def body(j):
                k0 = pl.multiple_of(j * K_BLOCK, K_BLOCK)
                for g in heads:
                    c = r * HEAD_GROUP + g
                    l_j, acc_j = bounded_tile(k_ref[0, g, pl.ds(k0, K_BLOCK), :],
                                              vT_ref[0, g, :, pl.ds(k0, K_BLOCK)],
                                              qT_ref[0, g, :, pl.ds(q0[r], Q_BLOCK)])
                    l_ref[c] += l_j
                    acc_ref[c] += acc_j
                return j - 1

            lax.while_loop(cond, body, i - 2)

            @pl.when(any_head(lambda g: meta_live(g, i)))
            def _():
                for g in heads:
                    c = r * HEAD_GROUP + g
                    l_m, acc_m = bounded_tile(km_ref[g], vmT_ref[g], qT_ref[0, g, :, pl.ds(q0[r], Q_BLOCK)])
                    l_ref[c] += l_m
                    acc_ref[c] += acc_m

        walks = [jnp.logical_or(jnp.logical_and(i >= 2, any_head(lambda g, i=i: block_live(g, i, i - 2))),
                                any_head(lambda g, i=i: meta_live(g, i))) for i in blk]
        any_walk = walks[0]
        for hit in walks[1:]:
            any_walk = jnp.logical_or(any_walk, hit)

        @pl.when(any_walk)
        def _():
            for r in range(Q_PER_STEP):
                walk(r)

        for c, (r, g) in enumerate(chains):
            l_tot = jnp.sum(l_ref[c], axis=0, keepdims=True)
            o_ref[0, g, :, pl.ds(q0[r], Q_BLOCK)] = (acc_ref[c] / l_tot).astype(o_ref.dtype)

    @pl.when(c_bound <= BOUNDED_LOGIT_MAX)
    def _():
        def step(n, carry):
            bounded_q_blocks(n * Q_PER_STEP)
            return carry

        lax.fori_loop(0, nq // Q_PER_STEP, step, 0)

    def online_q_block(g, i):
        q0 = pl.multiple_of(i * Q_BLOCK, Q_BLOCK)
        q_t = qT_ref[0, g, :, pl.ds(q0, Q_BLOCK)]
        s_m = _dot(km_ref[g], q_t)
        m = jnp.max(s_m, axis=0, keepdims=True)
        p_m = jnp.exp2(s_m - m)
        l = jnp.sum(p_m, axis=0, keepdims=True)
        acc = _dot(vmT_ref[g], p_m.astype(BF16))

        def online_step(state, k0, masked):
            m, l, acc = state
            s = _dot(k_ref[0, g, pl.ds(k0, K_BLOCK), :], q_t)
            if masked:
                s = jnp.where(causal, s, MASKED)
            m_new = jnp.maximum(m, jnp.max(s, axis=0, keepdims=True))
            alpha = jnp.exp2(m - m_new)
            p = jnp.exp2(s - m_new)
            l = alpha * l + jnp.sum(p, axis=0, keepdims=True)
            acc = alpha * acc + _dot(vT_ref[0, g, :, pl.ds(k0, K_BLOCK)], p.astype(BF16))
            return m_new, l, acc

        state = online_step((m, l, acc), q0, True)

        def cond(carry):
            return jnp.logical_and(carry[0] >= 0, block_live(g, i, carry[0]))

        def body(carry):
            j = carry[0]
            return (j - 1,) + online_step(carry[1:], pl.multiple_of(j * K_BLOCK, K_BLOCK), False)

        _, m, l, acc = lax.while_loop(cond, body, (i - 1,) + state)
        o_ref[0, g, :, pl.ds(q0, Q_BLOCK)] = (acc / l).astype(o_ref.dtype)

    @pl.when(c_bound > BOUNDED_LOGIT_MAX)
    def _():
        def head_loop(g, carry):
            def step(i, inner):
                online_q_block(g, i)
                return inner

            lax.fori_loop(0, nq, step, 0)
            return carry

        lax.fori_loop(0, HEAD_GROUP, head_loop, 0)


def _fox_attention(qT, k, vT, k_meta, vT_meta, fq0, flast, fmeta, c_bound):
    nb, _, _, seq = qT.shape
    assert seq % (Q_PER_STEP * Q_BLOCK) == 0 and Q_BLOCK == K_BLOCK
    smem = pl.BlockSpec(memory_space=pltpu.SMEM)
    hg = HEAD_GROUP
    return pl.pallas_call(
        _fox_attention_kernel,
        out_shape=jax.ShapeDtypeStruct((nb, HEADS, HEAD_DIM, seq), BF16),
        grid=(nb, HEADS // hg),
        in_specs=[
            smem, smem, smem, smem,
            pl.BlockSpec((1, hg, AUG_DIM, seq), lambda b, h: (b, h, 0, 0)),
            pl.BlockSpec((1, hg, seq, AUG_DIM), lambda b, h: (b, h, 0, 0)),
            pl.BlockSpec((1, hg, HEAD_DIM, seq), lambda b, h: (b, h, 0, 0)),
            pl.BlockSpec((hg, N_META, AUG_DIM), lambda b, h: (h, 0, 0)),
            pl.BlockSpec((hg, HEAD_DIM, N_META), lambda b, h: (h, 0, 0)),
        ],
        out_specs=pl.BlockSpec((1, hg, HEAD_DIM, seq), lambda b, h: (b, h, 0, 0)),
        scratch_shapes=[pltpu.VMEM((Q_PER_STEP * hg, HEAD_DIM, Q_BLOCK), F32),
                        pltpu.VMEM((Q_PER_STEP * hg, SUBLANES, Q_BLOCK), F32)],
        name="fox_attention",
        compiler_params=pltpu.CompilerParams(
            dimension_semantics=("arbitrary", "arbitrary"), vmem_limit_bytes=VMEM_LIMIT_BYTES),
    )(fq0, flast, fmeta, c_bound, qT, k, vT, k_meta, vT_meta)


def _mix_out_ffn2_kernel(h1_ref, oT_ref, oconv_ref,
                         gattn_ref, woa_ref, woc_ref,
                         g2_ref, win_hbm, wout_hbm, gfin_ref,
                         out_ref,
                         xn_ref, acc_ref, act_ref, win_ref, wout_ref, win_stage, wout_stage, win_sem, wout_sem):
    @pl.when(jnp.logical_and(pl.program_id(0) == 0, pl.program_id(1) == 0))
    def _():
        _load_weight_as_bf16(win_hbm, win_ref, win_stage, win_sem)
        _load_weight_as_bf16(wout_hbm, wout_ref, wout_stage, wout_sem, MACARON_SCALE)

    o_t = oT_ref[0].astype(F32)
    ms = jnp.mean(o_t * o_t, axis=0, keepdims=True)
    o_n = (o_t * lax.rsqrt(ms + EPS) * gattn_ref[...]).T.astype(BF16)
    mix = _dot(o_n, woa_ref[...]) + _dot(oconv_ref[0], woc_ref[...])
    h2 = h1_ref[0] + mix
    xn_ref[...] = _rmsnorm_rows(h2, g2_ref[...]).astype(BF16)
    _swiglu(xn_ref, win_ref, wout_ref, acc_ref, act_ref, h2)
    out_ref[0] = _rmsnorm_rows(acc_ref[...], gfin_ref[...])


def _mix_out_ffn2(h1, o_t, oconv, p, tm):
    nb, seq, _ = h1.shape
    gattn_t = jnp.broadcast_to(p["gattn"][:, None], (D_ATTN, tm))
    in_specs = [
        pl.BlockSpec((1, tm, D_MODEL), lambda b, t: (b, t, 0)),
        pl.BlockSpec((1, D_ATTN, tm), lambda b, t: (b, 0, t)),
        pl.BlockSpec((1, tm, D_CONV), lambda b, t: (b, t, 0)),
        _const_spec((D_ATTN, tm)), _const_spec((D_ATTN, D_MODEL)), _const_spec((D_CONV, D_MODEL)),
        _const_spec((1, D_MODEL)),
        pl.BlockSpec(memory_space=pl.ANY), pl.BlockSpec(memory_space=pl.ANY),
        _const_spec((1, D_MODEL)),
    ]
    return pl.pallas_call(
        _mix_out_ffn2_kernel,
        out_shape=jax.ShapeDtypeStruct((nb, seq, D_MODEL), F32),
        grid=(nb, seq // tm),
        in_specs=in_specs,
        out_specs=pl.BlockSpec((1, tm, D_MODEL), lambda b, t: (b, t, 0)),
        scratch_shapes=[pltpu.VMEM((tm, D_MODEL), BF16), pltpu.VMEM((tm, D_MODEL), F32),
                        pltpu.VMEM((tm, D_FF), BF16)] + _ffn_weight_scratch(),
        name="mix_out_ffn2",
        compiler_params=pltpu.CompilerParams(
            dimension_semantics=("arbitrary", "arbitrary"), vmem_limit_bytes=VMEM_LIMIT_BYTES),
    )(h1, o_t, oconv, gattn_t, p["woa"], p["woc"], p["g2"], p["win2"], p["wout2"], p["gfin"])


def kernel(x, meta_tokens, ffn1_norm, ffn1_w_in, ffn1_w_out, mix_norm, w_mix_in, b_forget, q_norm, k_norm, conv_w, attn_out_norm, conv_out_norm, w_mix_out, ffn2_norm, ffn2_w_in, ffn2_w_out, final_norm):
    nb, seq, _ = x.shape
    wmix = w_mix_in[0]
    n_qkv = 3 * D_ATTN
    p = {
        "g1": ffn1_norm, "win1": ffn1_w_in[0], "wout1": ffn1_w_out[0],
        "gmix": mix_norm,
        "wfkT": jnp.concatenate([jnp.pad(wmix[:, n_qkv:n_qkv + HEADS].T, ((0, F_ROWS - HEADS), (0, 0))),
                                 wmix[:, D_ATTN:2 * D_ATTN].T], axis=0).astype(BF16),
        "wqT": wmix[:, :D_ATTN].T.astype(BF16),
        "wvT": wmix[:, 2 * D_ATTN:n_qkv].T.astype(BF16),
        "wcu": wmix[:, n_qkv + HEADS + D_CONV:].astype(BF16),
        "wb": wmix[:, n_qkv + HEADS:n_qkv + HEADS + D_CONV].astype(BF16),
        "bf": jnp.pad(b_forget[0], (0, F_ROWS - HEADS))[:, None],
        "gq": q_norm[0], "gk": k_norm[0],
        "cw": jnp.pad(conv_w[0], ((0, SUBLANES - CONV_WIDTH), (0, 0))),
        "gconv": conv_out_norm,
        "gattn": attn_out_norm[0],
        "woa": w_mix_out[0, :D_ATTN].astype(BF16), "woc": w_mix_out[0, D_ATTN:].astype(BF16),
        "g2": ffn2_norm, "win2": ffn2_w_in[0], "wout2": ffn2_w_out[0],
        "gfin": final_norm,
    }

    meta = jnp.pad(meta_tokens.astype(x.dtype), ((0, META_TILE - N_META), (0, 0)))
    h1, qT, k, vT, f_cum, oconv, k_meta, vT_meta, f_meta = _ffn1_mix_in(x, meta, p, TOKEN_TILE)

    fq0 = f_cum[:, :, 0::Q_BLOCK].reshape(nb * HEADS, seq // Q_BLOCK)
    flast = f_cum[:, :, K_BLOCK - 1::K_BLOCK].reshape(nb * HEADS, seq // K_BLOCK)
    c_bound = (2.0 * HEAD_DIM ** 0.5 * SCORE_BOUND_SLACK * jnp.max(jnp.abs(q_norm)) * jnp.max(jnp.abs(k_norm))
               + SCORE_BOUND_MARGIN).reshape(1).astype(F32)

    o_t = _fox_attention(qT, k, vT, k_meta, vT_meta, fq0, flast, f_meta[:HEADS, 0], c_bound)
    o_t = o_t.reshape(nb, D_ATTN, seq)
    return _mix_out_ffn2(h1, o_t, oconv, p, TOKEN_TILE)
```

```python
from typing import Any, NamedTuple

import jax
import jax.numpy as jnp
from jax import lax
from jax.experimental import pallas as pl
from jax.experimental.pallas import tpu as pltpu

D_MODEL = 1024
N_META = 16
D_ATTN = 512
D_CONV = 512
HEADS = 8
HEAD_DIM = 64
CONV_WIDTH = 3
D_FF = 2816
EPS = 1e-6

F32 = jnp.float32
BF16 = jnp.bfloat16

LANES = 128
SUBLANES = 8
MXU_DIM = 256
VMEM_LIMIT_BYTES = 60000 * 1024

TOKEN_TILE = 512
META_TILE = LANES
FF_CHUNK = MXU_DIM
N_FF_CHUNKS = D_FF // FF_CHUNK
Q_BLOCK = MXU_DIM
K_BLOCK = MXU_DIM
AUG_DIM = LANES
HEAD_GROUP = 4
Q_PER_STEP = 8
CHAIN_GROUP = 2
VALUE_DOT_LAG = 2
MASKED = -1e30
LOG2E = 1.4426950408889634
F_ROWS = 2 * SUBLANES
BIAS_PIECES = 3
SCORE_BOUND_SLACK = 1.02
SCORE_BOUND_MARGIN = 1.0
MACARON_SCALE = 0.5
WIN_STAGE_ROWS = 64
WOUT_STAGE_ROWS = 256
STAGE_SLOTS = 4

EXP_UNDERFLOW = -104.0
BOUNDED_LOGIT_MAX = 120.0

assert D_FF % FF_CHUNK == 0 and FF_CHUNK % LANES == 0 and HEADS % HEAD_GROUP == 0


def _dot(a, b):
    return jnp.dot(a, b, preferred_element_type=F32)


def _dot_nt(a, b):
    return lax.dot_general(a, b, (((1,), (1,)), ((), ())), preferred_element_type=F32)


def _rmsnorm_rows(x, gain):
    ms = jnp.mean(x * x, axis=-1, keepdims=True)
    return x * lax.rsqrt(ms + EPS) * gain


def _split3(x):
    hi = x.astype(BF16)
    r1 = x - hi.astype(F32)
    mid = r1.astype(BF16)
    lo = (r1 - mid.astype(F32)).astype(BF16)
    return hi, mid, lo


def _load_weight_as_bf16(w_hbm, w_bf, stage, sem, scale=None):
    n_slots, rows = stage.shape[0], stage.shape[1]
    n_chunks = w_hbm.shape[0] // rows
    assert n_chunks * rows == w_hbm.shape[0]

    def chunk_copy(i):
        slot = i % n_slots
        return pltpu.make_async_copy(w_hbm.at[pl.ds(i * rows, rows)], stage.at[slot], sem.at[slot])

    for i in range(min(n_slots - 1, n_chunks)):
        chunk_copy(i).start()
    for i in range(n_chunks):
        if i + n_slots - 1 < n_chunks:
            chunk_copy(i + n_slots - 1).start()
        chunk_copy(i).wait()
        w = stage[i % n_slots]
        w_bf[i * rows:(i + 1) * rows, :] = (w if scale is None else w * scale).astype(BF16)


def _swiglu(xn_ref, win_ref, wout_ref, acc_ref, act_ref, residual):
    for c in range(N_FF_CHUNKS):
        lo, hi = c * FF_CHUNK, (c + 1) * FF_CHUNK
        xn = xn_ref[...]
        g = _dot(xn, win_ref[:, lo:hi])
        u = _dot(xn, win_ref[:, D_FF + lo:D_FF + hi])
        act_ref[:, lo:hi] = (g * jax.nn.sigmoid(g) * u).astype(BF16)
    acc_ref[...] = residual + _dot(act_ref[...], wout_ref[...])


class _MixWeights(NamedTuple):
    g1: Any
    win: Any
    wout: Any
    gmix: Any
    wfkT: Any
    wqT: Any
    wvT: Any
    wcu: Any
    wb: Any
    bf: Any
    gq: Any
    gk: Any
    cw: Any
    gconv: Any


def _lane_cumsum(x):
    lane = lax.broadcasted_iota(jnp.int32, x.shape, 1)
    shift = 1
    while shift < x.shape[1]:
        x = x + jnp.where(lane >= shift, pltpu.roll(x, shift, axis=1), 0.0)
        shift *= 2
    return x


class _TileOut(NamedTuple):
    h1: Any
    q: Any
    k: Any
    v: Any
    f: Any
    oconv: Any


def _token_tile(x, n_valid, w, xn_ref, acc_ref, act_ref, zc_ref, fc_ref, out):
    tm = x.shape[0]

    xn_ref[...] = _rmsnorm_rows(x, w.g1[...]).astype(BF16)
    _swiglu(xn_ref, w.win, w.wout, acc_ref, act_ref, x)
    h1 = acc_ref[...]
    if out.h1 is not None:
        out.h1[0] = h1

    xn2 = _rmsnorm_rows(h1, w.gmix[...]).astype(BF16)
    fkT = _dot_nt(w.wfkT[...], xn2)
    cu = _dot(xn2, w.wcu[...])

    fl = fkT[0:F_ROWS] + w.bf[...]
    logf = jnp.minimum(fl, 0.0) - jnp.log(1.0 + jnp.exp(-jnp.abs(fl)))
    if n_valid < tm:
        logf = jnp.where(lax.broadcasted_iota(jnp.int32, logf.shape, 1) < n_valid, logf, 0.0)
    csum = _lane_cumsum(logf)
    f_all = csum + jnp.concatenate([fc_ref[...]] * (tm // LANES), axis=1)
    fc_ref[...] = jnp.broadcast_to(f_all[:, tm - 1:tm], fc_ref.shape)
    if out.f is not None:
        out.f[0] = f_all[:HEADS]

    f_hi, f_mid, f_lo = (p.astype(F32) for p in _split3(f_all * LOG2E))
    row = lax.broadcasted_iota(jnp.int32, (SUBLANES, tm), 0)
    ones_mid = jnp.where(row < 2 * BIAS_PIECES, 1.0, 0.0)

    def bias_rows(pieces, first_row, sign):
        rows = ones_mid
        for i, piece in enumerate(pieces):
            rows = jnp.where(row == first_row + i, sign * piece, rows)
        return rows
    pad_rows = jnp.zeros((AUG_DIM - HEAD_DIM - SUBLANES, tm), F32)

    def head_pieces(h):
        return (jnp.broadcast_to(f_hi[h:h + 1], (SUBLANES, tm)),
                jnp.broadcast_to(f_mid[h:h + 1], (SUBLANES, tm)),
                jnp.broadcast_to(f_lo[h:h + 1], (SUBLANES, tm)))

    def head_rmsnorm(x_t, gain_t):
        x3 = x_t.reshape(HEADS, HEAD_DIM, tm)
        return x3 * lax.rsqrt(jnp.mean(x3 * x3, axis=1, keepdims=True) + EPS) * gain_t[None]

    kn = head_rmsnorm(fkT[F_ROWS:F_ROWS + D_ATTN], w.gk[:, 0:tm])
    if out.q is not None:
        qT = _dot_nt(w.wqT[...], xn2)
    if out.oconv is not None:
        gate_b = _dot(xn2, w.wb[...])
    for h in range(HEADS):
        aug_k = bias_rows(head_pieces(h), BIAS_PIECES, -1.0)
        k_aug_t = jnp.concatenate([kn[h], aug_k, pad_rows], axis=0)
        out.k(h, k_aug_t.T.astype(BF16))

    vT = _dot_nt(w.wvT[...], xn2)
    z = cu[:, 0:D_CONV] * cu[:, D_CONV:2 * D_CONV]
    if out.oconv is not None:
        zc = zc_ref[...]
        rowz = lax.broadcasted_iota(jnp.int32, z.shape, 0)
        prev1 = jnp.broadcast_to(zc[SUBLANES - 1:SUBLANES], z.shape)
        prev2 = jnp.broadcast_to(zc[SUBLANES - 2:SUBLANES - 1], z.shape)
        z1 = jnp.where(rowz == 0, prev1, pltpu.roll(z, 1, axis=0))
        z2 = jnp.where(rowz == 0, prev2, jnp.where(rowz == 1, prev1, pltpu.roll(z, 2, axis=0)))
        cw = w.cw[...]
        y = cw[0:1] * z2 + cw[1:2] * z1 + cw[2:3] * z
        out.oconv[0] = _rmsnorm_rows(gate_b * y, w.gconv[...]).astype(BF16)
    zc_ref[...] = z[n_valid - SUBLANES:n_valid]

    if out.q is not None:
        qn = head_rmsnorm(qT, w.gq[:, 0:tm] * (HEAD_DIM ** -0.5 * LOG2E))
    v3 = vT.reshape(HEADS, HEAD_DIM, tm)
    for h in range(HEADS):
        if out.q is not None:
            aug_q = bias_rows(head_pieces(h), 0, 1.0)
            out.q[0, h] = jnp.concatenate([qn[h], aug_q, pad_rows], axis=0).astype(BF16)
        out.v(h, v3[h].astype(BF16))


def _ffn1_mix_in_kernel(
        x_ref, meta_ref,
        g1_ref, win_hbm, wout_hbm,
        gmix_ref, wfkT_ref, wqT_ref, wvT_ref, wcu_ref, wb_ref, bf_ref,
        gq_ref, gk_ref, cw_ref, gconv_ref,
        h1_ref, qT_ref, k_ref, vT_ref, f_ref, oconv_ref, km_ref, vmT_ref, fmeta_ref,
        xn_ref, acc_ref, act_ref, zc_ref, fc_ref, zc0_ref, fc0_ref,
        win_ref, wout_ref, win_stage, wout_stage, win_sem, wout_sem):
    t = pl.program_id(1)
    w = _MixWeights(g1_ref, win_ref, wout_ref, gmix_ref, wfkT_ref, wqT_ref, wvT_ref, wcu_ref, wb_ref,
                    bf_ref, gq_ref, gk_ref, cw_ref, gconv_ref)

    def put_k(h, k_aug):
        k_ref[0, h] = k_aug

    def put_v(h, v_t):
        vT_ref[0, h] = v_t

    def put_meta_k(h, k_aug):
        km_ref[h] = k_aug[0:N_META]

    def put_meta_v(h, v_t):
        vmT_ref[h] = v_t[:, 0:N_META]

    @pl.when(jnp.logical_and(pl.program_id(0) == 0, t == 0))
    def _():
        _load_weight_as_bf16(win_hbm, win_ref, win_stage, win_sem)
        _load_weight_as_bf16(wout_hbm, wout_ref, wout_stage, wout_sem, MACARON_SCALE)
        rows = meta_ref.shape[0]
        zc_ref[...] = jnp.zeros_like(zc_ref)
        fc_ref[...] = jnp.zeros_like(fc_ref)
        _token_tile(meta_ref[...], N_META, w, xn_ref.at[pl.ds(0, rows)], acc_ref.at[pl.ds(0, rows)],
                    act_ref.at[pl.ds(0, rows)], zc_ref, fc_ref, _TileOut(None, None, put_meta_k, put_meta_v, None, None))
        zc0_ref[...] = zc_ref[...]
        fc0_ref[...] = fc_ref[...]
        fmeta_ref[...] = fc_ref[...]

    @pl.when(t == 0)
    def _():
        zc_ref[...] = zc0_ref[...]
        fc_ref[...] = fc0_ref[...]

    _token_tile(x_ref[0], x_ref.shape[1], w, xn_ref, acc_ref, act_ref, zc_ref, fc_ref,
                _TileOut(h1_ref, qT_ref, put_k, put_v, f_ref, oconv_ref))


def _ffn_weight_scratch():
    return [pltpu.VMEM((D_MODEL, 2 * D_FF), BF16), pltpu.VMEM((D_FF, D_MODEL), BF16),
            pltpu.VMEM((STAGE_SLOTS, WIN_STAGE_ROWS, 2 * D_FF), F32),
            pltpu.VMEM((STAGE_SLOTS, WOUT_STAGE_ROWS, D_MODEL), F32),
            pltpu.SemaphoreType.DMA((STAGE_SLOTS,)), pltpu.SemaphoreType.DMA((STAGE_SLOTS,))]


def _const_spec(shape):
    nd = len(shape)
    return pl.BlockSpec(shape, lambda *_: (0,) * nd, pipeline_mode=pl.Buffered(1))


def _ffn1_mix_in(x, meta, p, tm):
    nb, seq, _ = x.shape
    nt = seq // tm
    assert tm % LANES == 0 and meta.shape[0] <= tm
    gq_t = jnp.broadcast_to(p["gq"][:, None], (HEAD_DIM, tm))
    gk_t = jnp.broadcast_to(p["gk"][:, None], (HEAD_DIM, tm))

    def tile3(last):
        return pl.BlockSpec((1, tm, last), lambda b, t: (b, t, 0))

    in_specs = [
        tile3(D_MODEL), _const_spec(meta.shape),
        _const_spec((1, D_MODEL)),
        pl.BlockSpec(memory_space=pl.ANY), pl.BlockSpec(memory_space=pl.ANY),
        _const_spec((1, D_MODEL)), _const_spec((F_ROWS + D_ATTN, D_MODEL)),
        _const_spec((D_ATTN, D_MODEL)), _const_spec((D_ATTN, D_MODEL)),
        _const_spec((D_MODEL, 2 * D_CONV)), _const_spec((D_MODEL, D_CONV)),
        _const_spec((F_ROWS, 1)),
        _const_spec((HEAD_DIM, tm)), _const_spec((HEAD_DIM, tm)),
        _const_spec((SUBLANES, D_CONV)), _const_spec((1, D_CONV)),
    ]
    out_shape = [
        jax.ShapeDtypeStruct((nb, seq, D_MODEL), F32),
        jax.ShapeDtypeStruct((nb, HEADS, AUG_DIM, seq), BF16),
        jax.ShapeDtypeStruct((nb, HEADS, seq, AUG_DIM), BF16),
        jax.ShapeDtypeStruct((nb, HEADS, HEAD_DIM, seq), BF16),
        jax.ShapeDtypeStruct((nb, HEADS, seq), F32),
        jax.ShapeDtypeStruct((nb, seq, D_CONV), BF16),
        jax.ShapeDtypeStruct((HEADS, N_META, AUG_DIM), BF16),
        jax.ShapeDtypeStruct((HEADS, HEAD_DIM, N_META), BF16),
        jax.ShapeDtypeStruct((F_ROWS, LANES), F32),
    ]
    out_specs = [
        tile3(D_MODEL),
        pl.BlockSpec((1, HEADS, AUG_DIM, tm), lambda b, t: (b, 0, 0, t)),
        pl.BlockSpec((1, HEADS, tm, AUG_DIM), lambda b, t: (b, 0, t, 0)),
        pl.BlockSpec((1, HEADS, HEAD_DIM, tm), lambda b, t: (b, 0, 0, t)),
        pl.BlockSpec((1, HEADS, tm), lambda b, t: (b, 0, t)),
        tile3(D_CONV),
        pl.BlockSpec((HEADS, N_META, AUG_DIM), lambda b, t: (0, 0, 0)),
        pl.BlockSpec((HEADS, HEAD_DIM, N_META), lambda b, t: (0, 0, 0)),
        pl.BlockSpec((F_ROWS, LANES), lambda b, t: (0, 0)),
    ]
    scratch_shapes = [
        pltpu.VMEM((tm, D_MODEL), BF16),
        pltpu.VMEM((tm, D_MODEL), F32),
        pltpu.VMEM((tm, D_FF), BF16),
        pltpu.VMEM((SUBLANES, D_CONV), F32),
        pltpu.VMEM((F_ROWS, LANES), F32),
        pltpu.VMEM((SUBLANES, D_CONV), F32),
        pltpu.VMEM((F_ROWS, LANES), F32),
    ] + _ffn_weight_scratch()
    return pl.pallas_call(
        _ffn1_mix_in_kernel, out_shape=out_shape, grid=(nb, nt), in_specs=in_specs, out_specs=out_specs,
        scratch_shapes=scratch_shapes, name="ffn1_mix_in",
        compiler_params=pltpu.CompilerParams(
            dimension_semantics=("arbitrary", "arbitrary"), vmem_limit_bytes=VMEM_LIMIT_BYTES),
    )(x, meta,
      p["g1"], p["win1"], p["wout1"],
      p["gmix"], p["wfkT"], p["wqT"], p["wvT"], p["wcu"], p["wb"], p["bf"],
      gq_t, gk_t, p["cw"], p["gconv"])


def _fox_attention_kernel(fq0_ref, flast_ref, fmeta_ref, cb_ref,
                          qT_ref, k_ref, vT_ref, km_ref, vmT_ref,
                          o_ref,
                          acc_ref, l_ref):
    bh0 = pl.program_id(0) * HEADS + pl.program_id(1) * HEAD_GROUP
    heads = range(HEAD_GROUP)
    nq = qT_ref.shape[3] // Q_BLOCK
    c_bound = cb_ref[0]
    key_idx = lax.broadcasted_iota(jnp.int32, (K_BLOCK, Q_BLOCK), 0)
    qry_idx = lax.broadcasted_iota(jnp.int32, (K_BLOCK, Q_BLOCK), 1)
    causal = key_idx <= qry_idx

    def block_live(g, i, j):
        return fq0_ref[bh0 + g, i] - flast_ref[bh0 + g, jnp.maximum(j, 0)] + c_bound >= EXP_UNDERFLOW

    def meta_live(g, i):
        head = pl.program_id(1) * HEAD_GROUP + g
        return fq0_ref[bh0 + g, i] - fmeta_ref[head] + c_bound >= EXP_UNDERFLOW

    def sublane_partial_sum(p):
        return jnp.sum(p.reshape(p.shape[0] // SUBLANES, SUBLANES, p.shape[1]), axis=0)

    def bounded_tile(k_blk, v_t, q_t):
        p = jnp.exp2(_dot(k_blk, q_t))
        return sublane_partial_sum(p), _dot(v_t, p.astype(BF16))

    def bounded_q_blocks(i_first):
        chains = [(r, g) for r in range(Q_PER_STEP) for g in heads]
        blk = [i_first + r for r in range(Q_PER_STEP)]
        q0 = [pl.multiple_of(i * Q_BLOCK, Q_BLOCK) for i in blk]
        p0 = [pl.multiple_of(jnp.maximum(i - 1, 0) * K_BLOCK, K_BLOCK) for i in blk]
        first_mask = jnp.where(i_first >= 1, 0.0, MASKED)

        def weights(group):
            q_t = [qT_ref[0, g, :, pl.ds(q0[r], Q_BLOCK)] for r, g in group]
            s_d = [_dot(k_ref[0, g, pl.ds(q0[r], K_BLOCK), :], q) for (r, g), q in zip(group, q_t)]
            s_p = [_dot(k_ref[0, g, pl.ds(p0[r], K_BLOCK), :], q) for (r, g), q in zip(group, q_t)]
            return ([jnp.exp2(jnp.where(causal, s, MASKED)) for s in s_d],
                    [jnp.exp2(s + first_mask if r == 0 else s) for s, (r, g) in zip(s_p, group)])

        def values(group, p):
            for (r, g), p_d, p_p in zip(group, *p):
                c = r * HEAD_GROUP + g
                l_ref[c] = sublane_partial_sum(p_d) + sublane_partial_sum(p_p)
                acc_ref[c] = (_dot(vT_ref[0, g, :, pl.ds(q0[r], K_BLOCK)], p_d.astype(BF16))
                              + _dot(vT_ref[0, g, :, pl.ds(p0[r], K_BLOCK)], p_p.astype(BF16)))

        groups = [chains[n:n + CHAIN_GROUP] for n in range(0, len(chains), CHAIN_GROUP)]
        pending = []
        for n, group in enumerate(groups):
            pending.append((group, weights(group)))
            if n >= VALUE_DOT_LAG:
                values(*pending.pop(0))
        for item in pending:
            values(*item)

        def any_head(test):
            hit = test(0)
            for g in heads[1:]:
                hit = jnp.logical_or(hit, test(g))
            return hit

        def walk(r):
            i = blk[r]

            def cond(j):
                return jnp.logical_and(j >= 0, any_head(lambda g: block_live(g, i, j)))

            def body(j):
                k0 = pl.multiple_of(j * K_BLOCK, K_BLOCK)
                for g in heads:
                    c = r * HEAD_GROUP + g
                    l_j, acc_j = bounded_tile(k_ref[0, g, pl.ds(k0, K_BLOCK), :],
                                              vT_ref[0, g, :, pl.ds(k0, K_BLOCK)],
                                              qT_ref[0, g, :, pl.ds(q0[r], Q_BLOCK)])
                    l_ref[c] += l_j
                    acc_ref[c] += acc_j
                return j - 1

            lax.while_loop(cond, body, i - 2)

            @pl.when(any_head(lambda g: meta_live(g, i)))
            def _():
                for g in heads:
                    c = r * HEAD_GROUP + g
                    l_m, acc_m = bounded_tile(km_ref[g], vmT_ref[g], qT_ref[0, g, :, pl.ds(q0[r], Q_BLOCK)])
                    l_ref[c] += l_m
                    acc_ref[c] += acc_m

        walks = [jnp.logical_or(jnp.logical_and(i >= 2, any_head(lambda g, i=i: block_live(g, i, i - 2))),
                                any_head(lambda g, i=i: meta_live(g, i))) for i in blk]
        any_walk = walks[0]
        for hit in walks[1:]:
            any_walk = jnp.logical_or(any_walk, hit)

        @pl.when(any_walk)
        def _():
            for r in range(Q_PER_STEP):
                walk(r)

        for c, (r, g) in enumerate(chains):
            l_tot = jnp.sum(l_ref[c], axis=0, keepdims=True)
            o_ref[0, g, :, pl.ds(q0[r], Q_BLOCK)] = (acc_ref[c] / l_tot).astype(o_ref.dtype)

    @pl.when(c_bound <= BOUNDED_LOGIT_MAX)
    def _():
        def step(n, carry):
            bounded_q_blocks(n * Q_PER_STEP)
            return carry

        lax.fori_loop(0, nq // Q_PER_STEP, step, 0)

    def online_q_block(g, i):
        q0 = pl.multiple_of(i * Q_BLOCK, Q_BLOCK)
        q_t = qT_ref[0, g, :, pl.ds(q0, Q_BLOCK)]
        s_m = _dot(km_ref[g], q_t)
        m = jnp.max(s_m, axis=0, keepdims=True)
        p_m = jnp.exp2(s_m - m)
        l = jnp.sum(p_m, axis=0, keepdims=True)
        acc = _dot(vmT_ref[g], p_m.astype(BF16))

        def online_step(state, k0, masked):
            m, l, acc = state
            s = _dot(k_ref[0, g, pl.ds(k0, K_BLOCK), :], q_t)
            if masked:
                s = jnp.where(causal, s, MASKED)
            m_new = jnp.maximum(m, jnp.max(s, axis=0, keepdims=True))
            alpha = jnp.exp2(m - m_new)
            p = jnp.exp2(s - m_new)
            l = alpha * l + jnp.sum(p, axis=0, keepdims=True)
            acc = alpha * acc + _dot(vT_ref[0, g, :, pl.ds(k0, K_BLOCK)], p.astype(BF16))
            return m_new, l, acc

        state = online_step((m, l, acc), q0, True)

        def cond(carry):
            return jnp.logical_and(carry[0] >= 0, block_live(g, i, carry[0]))

        def body(carry):
            j = carry[0]
            return (j - 1,) + online_step(carry[1:], pl.multiple_of(j * K_BLOCK, K_BLOCK), False)

        _, m, l, acc = lax.while_loop(cond, body, (i - 1,) + state)
        o_ref[0, g, :, pl.ds(q0, Q_BLOCK)] = (acc / l).astype(o_ref.dtype)

    @pl.when(c_bound > BOUNDED_LOGIT_MAX)
    def _():
        def head_loop(g, carry):
            def step(i, inner):
                online_q_block(g, i)
                return inner

            lax.fori_loop(0, nq, step, 0)
            return carry

        lax.fori_loop(0, HEAD_GROUP, head_loop, 0)


def _fox_attention(qT, k, vT, k_meta, vT_meta, fq0, flast, fmeta, c_bound):
    nb, _, _, seq = qT.shape
    assert seq % (Q_PER_STEP * Q_BLOCK) == 0 and Q_BLOCK == K_BLOCK
    smem = pl.BlockSpec(memory_space=pltpu.SMEM)
    hg = HEAD_GROUP
    return pl.pallas_call(
        _fox_attention_kernel,
        out_shape=jax.ShapeDtypeStruct((nb, HEADS, HEAD_DIM, seq), BF16),
        grid=(nb, HEADS // hg),
        in_specs=[
            smem, smem, smem, smem,
            pl.BlockSpec((1, hg, AUG_DIM, seq), lambda b, h: (b, h, 0, 0)),
            pl.BlockSpec((1, hg, seq, AUG_DIM), lambda b, h: (b, h, 0, 0)),
            pl.BlockSpec((1, hg, HEAD_DIM, seq), lambda b, h: (b, h, 0, 0)),
            pl.BlockSpec((hg, N_META, AUG_DIM), lambda b, h: (h, 0, 0)),
            pl.BlockSpec((hg, HEAD_DIM, N_META), lambda b, h: (h, 0, 0)),
        ],
        out_specs=pl.BlockSpec((1, hg, HEAD_DIM, seq), lambda b, h: (b, h, 0, 0)),
        scratch_shapes=[pltpu.VMEM((Q_PER_STEP * hg, HEAD_DIM, Q_BLOCK), F32),
                        pltpu.VMEM((Q_PER_STEP * hg, SUBLANES, Q_BLOCK), F32)],
        name="fox_attention",
        compiler_params=pltpu.CompilerParams(
            dimension_semantics=("arbitrary", "arbitrary"), vmem_limit_bytes=VMEM_LIMIT_BYTES),
    )(fq0, flast, fmeta, c_bound, qT, k, vT, k_meta, vT_meta)


def _mix_out_ffn2_kernel(h1_ref, oT_ref, oconv_ref,
                         gattn_ref, woa_ref, woc_ref,
                         g2_ref, win_hbm, wout_hbm, gfin_ref,
                         out_ref,
                         xn_ref, acc_ref, act_ref, win_ref, wout_ref, win_stage, wout_stage, win_sem, wout_sem):
    @pl.when(jnp.logical_and(pl.program_id(0) == 0, pl.program_id(1) == 0))
    def _():
        _load_weight_as_bf16(win_hbm, win_ref, win_stage, win_sem)
        _load_weight_as_bf16(wout_hbm, wout_ref, wout_stage, wout_sem, MACARON_SCALE)

    o_t = oT_ref[0].astype(F32)
    ms = jnp.mean(o_t * o_t, axis=0, keepdims=True)
    o_n = (o_t * lax.rsqrt(ms + EPS) * gattn_ref[...]).T.astype(BF16)
    mix = _dot(o_n, woa_ref[...]) + _dot(oconv_ref[0], woc_ref[...])
    h2 = h1_ref[0] + mix
    xn_ref[...] = _rmsnorm_rows(h2, g2_ref[...]).astype(BF16)
    _swiglu(xn_ref, win_ref, wout_ref, acc_ref, act_ref, h2)
    out_ref[0] = _rmsnorm_rows(acc_ref[...], gfin_ref[...])


def _mix_out_ffn2(h1, o_t, oconv, p, tm):
    nb, seq, _ = h1.shape
    gattn_t = jnp.broadcast_to(p["gattn"][:, None], (D_ATTN, tm))
    in_specs = [
        pl.BlockSpec((1, tm, D_MODEL), lambda b, t: (b, t, 0)),
        pl.BlockSpec((1, D_ATTN, tm), lambda b, t: (b, 0, t)),
        pl.BlockSpec((1, tm, D_CONV), lambda b, t: (b, t, 0)),
        _const_spec((D_ATTN, tm)), _const_spec((D_ATTN, D_MODEL)), _const_spec((D_CONV, D_MODEL)),
        _const_spec((1, D_MODEL)),
        pl.BlockSpec(memory_space=pl.ANY), pl.BlockSpec(memory_space=pl.ANY),
        _const_spec((1, D_MODEL)),
    ]
    return pl.pallas_call(
        _mix_out_ffn2_kernel,
        out_shape=jax.ShapeDtypeStruct((nb, seq, D_MODEL), F32),
        grid=(nb, seq // tm),
        in_specs=in_specs,
        out_specs=pl.BlockSpec((1, tm, D_MODEL), lambda b, t: (b, t, 0)),
        scratch_shapes=[pltpu.VMEM((tm, D_MODEL), BF16), pltpu.VMEM((tm, D_MODEL), F32),
                        pltpu.VMEM((tm, D_FF), BF16)] + _ffn_weight_scratch(),
        name="mix_out_ffn2",
        compiler_params=pltpu.CompilerParams(
            dimension_semantics=("arbitrary", "arbitrary"), vmem_limit_bytes=VMEM_LIMIT_BYTES),
    )(h1, o_t, oconv, gattn_t, p["woa"], p["woc"], p["g2"], p["win2"], p["wout2"], p["gfin"])


def kernel(x, meta_tokens, ffn1_norm, ffn1_w_in, ffn1_w_out, mix_norm, w_mix_in, b_forget, q_norm, k_norm, conv_w, attn_out_norm, conv_out_norm, w_mix_out, ffn2_norm, ffn2_w_in, ffn2_w_out, final_norm):
    nb, seq, _ = x.shape
    wmix = w_mix_in[0]
    n_qkv = 3 * D_ATTN
    p = {
        "g1": ffn1_norm, "win1": ffn1_w_in[0], "wout1": ffn1_w_out[0],
        "gmix": mix_norm,
        "wfkT": jnp.concatenate([jnp.pad(wmix[:, n_qkv:n_qkv + HEADS].T, ((0, F_ROWS - HEADS), (0, 0))),
                                 wmix[:, D_ATTN:2 * D_ATTN].T], axis=0).astype(BF16),
        "wqT": wmix[:, :D_ATTN].T.astype(BF16),
        "wvT": wmix[:, 2 * D_ATTN:n_qkv].T.astype(BF16),
        "wcu": wmix[:, n_qkv + HEADS + D_CONV:].astype(BF16),
        "wb": wmix[:, n_qkv + HEADS:n_qkv + HEADS + D_CONV].astype(BF16),
        "bf": jnp.pad(b_forget[0], (0, F_ROWS - HEADS))[:, None],
        "gq": q_norm[0], "gk": k_norm[0],
        "cw": jnp.pad(conv_w[0], ((0, SUBLANES - CONV_WIDTH), (0, 0))),
        "gconv": conv_out_norm,
        "gattn": attn_out_norm[0],
        "woa": w_mix_out[0, :D_ATTN].astype(BF16), "woc": w_mix_out[0, D_ATTN:].astype(BF16),
        "g2": ffn2_norm, "win2": ffn2_w_in[0], "wout2": ffn2_w_out[0],
        "gfin": final_norm,
    }

    meta = jnp.pad(meta_tokens.astype(x.dtype), ((0, META_TILE - N_META), (0, 0)))
    h1, qT, k, vT, f_cum, oconv, k_meta, vT_meta, f_meta = _ffn1_mix_in(x, meta, p, TOKEN_TILE)

    fq0 = f_cum[:, :, 0::Q_BLOCK].reshape(nb * HEADS, seq // Q_BLOCK)
    flast = f_cum[:, :, K_BLOCK - 1::K_BLOCK].reshape(nb * HEADS, seq // K_BLOCK)
    c_bound = (2.0 * HEAD_DIM ** 0.5 * SCORE_BOUND_SLACK * jnp.max(jnp.abs(q_norm)) * jnp.max(jnp.abs(k_norm))
               + SCORE_BOUND_MARGIN).reshape(1).astype(F32)

    o_t = _fox_attention(qT, k, vT, k_meta, vT_meta, fq0, flast, f_meta[:HEADS, 0], c_bound)
    o_t = o_t.reshape(nb, D_ATTN, seq)
    return _mix_out_ffn2(h1, o_t, oconv, p, TOKEN_TILE)
```

```python
from typing import Any, NamedTuple

import jax
import jax.numpy as jnp
from jax import lax
from jax.experimental import pallas as pl
from jax.experimental.pallas import tpu as pltpu

D_MODEL = 1024
N_META = 16
D_ATTN = 512
D_CONV = 512
HEADS = 8
HEAD_DIM = 64
CONV_WIDTH = 3
D_FF = 2816
EPS = 1e-6

F32 = jnp.float32
BF16 = jnp.bfloat16

LANES = 128
SUBLANES = 8
MXU_DIM = 256
VMEM_LIMIT_BYTES = 60000 * 1024

TOKEN_TILE = 512
META_TILE = LANES
FF_CHUNK = MXU_DIM
N_FF_CHUNKS = D_FF // FF_CHUNK
Q_BLOCK = MXU_DIM
K_BLOCK = MXU_DIM
AUG_DIM = LANES
HEAD_GROUP = 4
Q_PER_STEP = 8
CHAIN_GROUP = 2
VALUE_DOT_LAG = 2
MASKED = -1e30
LOG2E = 1.4426950408889634
F_ROWS = 2 * SUBLANES
BIAS_PIECES = 3
SCORE_BOUND_SLACK = 1.02
SCORE_BOUND_MARGIN = 1.0
MACARON_SCALE = 0.5
WIN_STAGE_ROWS = 64
WOUT_STAGE_ROWS = 256
STAGE_SLOTS = 4

EXP_UNDERFLOW = -104.0
BOUNDED_LOGIT_MAX = 120.0

assert D_FF % FF_CHUNK == 0 and FF_CHUNK % LANES == 0 and HEADS % HEAD_GROUP == 0


def _dot(a, b):
    return jnp.dot(a, b, preferred_element_type=F32)


def _dot_nt(a, b):
    return lax.dot_general(a, b, (((1,), (1,)), ((), ())), preferred_element_type=F32)


def _rmsnorm_rows(x, gain):
    ms = jnp.mean(x * x, axis=-1, keepdims=True)
    return x * lax.rsqrt(ms + EPS) * gain


def _split3(x):
    hi = x.astype(BF16)
    r1 = x - hi.astype(F32)
    mid = r1.astype(BF16)
    lo = (r1 - mid.astype(F32)).astype(BF16)
    return hi, mid, lo


def _load_weight_as_bf16(w_hbm, w_bf, stage, sem, scale=None):
    n_slots, rows = stage.shape[0], stage.shape[1]
    n_chunks = w_hbm.shape[0] // rows
    assert n_chunks * rows == w_hbm.shape[0]

    def chunk_copy(i):
        slot = i % n_slots
        return pltpu.make_async_copy(w_hbm.at[pl.ds(i * rows, rows)], stage.at[slot], sem.at[slot])

    for i in range(min(n_slots - 1, n_chunks)):
        chunk_copy(i).start()
    for i in range(n_chunks):
        if i + n_slots - 1 < n_chunks:
            chunk_copy(i + n_slots - 1).start()
        chunk_copy(i).wait()
        w = stage[i % n_slots]
        w_bf[i * rows:(i + 1) * rows, :] = (w if scale is None else w * scale).astype(BF16)


def _stream_ffn_weights(win_hbm, wout_hbm, win_ref, wout_ref, stage_in, stage_out, sem):
    n_slots = stage_out.shape[0]

    def copies(c):
        slot, lo = c % n_slots, c * FF_CHUNK
        return (pltpu.make_async_copy(win_hbm.at[:, pl.ds(lo, FF_CHUNK)], stage_in.at[slot, 0], sem.at[slot, 0]),
                pltpu.make_async_copy(win_hbm.at[:, pl.ds(D_FF + lo, FF_CHUNK)], stage_in.at[slot, 1],
                                      sem.at[slot, 1]),
                pltpu.make_async_copy(wout_hbm.at[pl.ds(lo, FF_CHUNK)], stage_out.at[slot], sem.at[slot, 2]))

    def start(c):
        for copy in copies(c):
            copy.start()

    for c in range(min(n_slots, N_FF_CHUNKS)):
        start(c)

    def prepare(c):
        slot, lo, hi = c % n_slots, c * FF_CHUNK, (c + 1) * FF_CHUNK
        for copy in copies(c):
            copy.wait()
        win_ref[:, lo:hi] = stage_in[slot, 0].astype(BF16)
        win_ref[:, D_FF + lo:D_FF + hi] = stage_in[slot, 1].astype(BF16)
        wout_ref[lo:hi, :] = (stage_out[slot] * MACARON_SCALE).astype(BF16)
        if c + n_slots < N_FF_CHUNKS:
            start(c + n_slots)

    return prepare


def _swiglu(xn_ref, win_ref, wout_ref, acc_ref, act_ref, residual, prepare=None):
    for c in range(N_FF_CHUNKS):
        lo, hi = c * FF_CHUNK, (c + 1) * FF_CHUNK
        if prepare is not None:
            prepare(c)
        xn = xn_ref[...]
        g = _dot(xn, win_ref[:, lo:hi])
        u = _dot(xn, win_ref[:, D_FF + lo:D_FF + hi])
        act_ref[:, lo:hi] = (g * jax.nn.sigmoid(g) * u).astype(BF16)
    acc_ref[...] = residual + _dot(act_ref[...], wout_ref[...])


class _MixWeights(NamedTuple):
    g1: Any
    win: Any
    wout: Any
    gmix: Any
    wfkT: Any
    wqT: Any
    wvT: Any
    wcu: Any
    wb: Any
    bf: Any
    gq: Any
    gk: Any
    cw: Any
    gconv: Any


def _lane_cumsum(x):
    lane = lax.broadcasted_iota(jnp.int32, x.shape, 1)
    shift = 1
    while shift < x.shape[1]:
        x = x + jnp.where(lane >= shift, pltpu.roll(x, shift, axis=1), 0.0)
        shift *= 2
    return x


class _TileOut(NamedTuple):
    h1: Any
    q: Any
    k: Any
    v: Any
    f: Any
    oconv: Any


def _token_tile(x, n_valid, w, xn_ref, acc_ref, act_ref, zc_ref, fc_ref, out):
    tm = x.shape[0]

    xn_ref[...] = _rmsnorm_rows(x, w.g1[...]).astype(BF16)
    _swiglu(xn_ref, w.win, w.wout, acc_ref, act_ref, x)
    h1 = acc_ref[...]
    if out.h1 is not None:
        out.h1[0] = h1

    xn2 = _rmsnorm_rows(h1, w.gmix[...]).astype(BF16)
    fkT = _dot_nt(w.wfkT[...], xn2)
    cu = _dot(xn2, w.wcu[...])

    fl = fkT[0:F_ROWS] + w.bf[...]
    logf = jnp.minimum(fl, 0.0) - jnp.log(1.0 + jnp.exp(-jnp.abs(fl)))
    if n_valid < tm:
        logf = jnp.where(lax.broadcasted_iota(jnp.int32, logf.shape, 1) < n_valid, logf, 0.0)
    csum = _lane_cumsum(logf)
    f_all = csum + jnp.concatenate([fc_ref[...]] * (tm // LANES), axis=1)
    fc_ref[...] = jnp.broadcast_to(f_all[:, tm - 1:tm], fc_ref.shape)
    if out.f is not None:
        out.f[0] = f_all[:HEADS]

    f_hi, f_mid, f_lo = (p.astype(F32) for p in _split3(f_all * LOG2E))
    row = lax.broadcasted_iota(jnp.int32, (SUBLANES, tm), 0)
    ones_mid = jnp.where(row < 2 * BIAS_PIECES, 1.0, 0.0)

    def bias_rows(pieces, first_row, sign):
        rows = ones_mid
        for i, piece in enumerate(pieces):
            rows = jnp.where(row == first_row + i, sign * piece, rows)
        return rows
    pad_rows = jnp.zeros((AUG_DIM - HEAD_DIM - SUBLANES, tm), F32)

    def head_pieces(h):
        return (jnp.broadcast_to(f_hi[h:h + 1], (SUBLANES, tm)),
                jnp.broadcast_to(f_mid[h:h + 1], (SUBLANES, tm)),
                jnp.broadcast_to(f_lo[h:h + 1], (SUBLANES, tm)))

    def head_rmsnorm(x_t, gain_t):
        x3 = x_t.reshape(HEADS, HEAD_DIM, tm)
        return x3 * lax.rsqrt(jnp.mean(x3 * x3, axis=1, keepdims=True) + EPS) * gain_t[None]

    kn = head_rmsnorm(fkT[F_ROWS:F_ROWS + D_ATTN], w.gk[:, 0:tm])
    if out.q is not None:
        qT = _dot_nt(w.wqT[...], xn2)
    if out.oconv is not None:
        gate_b = _dot(xn2, w.wb[...])
    for h in range(HEADS):
        aug_k = bias_rows(head_pieces(h), BIAS_PIECES, -1.0)
        k_aug_t = jnp.concatenate([kn[h], aug_k, pad_rows], axis=0)
        out.k(h, k_aug_t.T.astype(BF16))

    vT = _dot_nt(w.wvT[...], xn2)
    z = cu[:, 0:D_CONV] * cu[:, D_CONV:2 * D_CONV]
    if out.oconv is not None:
        zc = zc_ref[...]
        rowz = lax.broadcasted_iota(jnp.int32, z.shape, 0)
        prev1 = jnp.broadcast_to(zc[SUBLANES - 1:SUBLANES], z.shape)
        prev2 = jnp.broadcast_to(zc[SUBLANES - 2:SUBLANES - 1], z.shape)
        z1 = jnp.where(rowz == 0, prev1, pltpu.roll(z, 1, axis=0))
        z2 = jnp.where(rowz == 0, prev2, jnp.where(rowz == 1, prev1, pltpu.roll(z, 2, axis=0)))
        cw = w.cw[...]
        y = cw[0:1] * z2 + cw[1:2] * z1 + cw[2:3] * z
        out.oconv[0] = _rmsnorm_rows(gate_b * y, w.gconv[...]).astype(BF16)
    zc_ref[...] = z[n_valid - SUBLANES:n_valid]

    if out.q is not None:
        qn = head_rmsnorm(qT, w.gq[:, 0:tm] * (HEAD_DIM ** -0.5 * LOG2E))
    v3 = vT.reshape(HEADS, HEAD_DIM, tm)
    for h in range(HEADS):
        if out.q is not None:
            aug_q = bias_rows(head_pieces(h), 0, 1.0)
            out.q[0, h] = jnp.concatenate([qn[h], aug_q, pad_rows], axis=0).astype(BF16)
        out.v(h, v3[h].astype(BF16))


def _ffn1_mix_in_kernel(
        x_ref, meta_ref,
        g1_ref, win_hbm, wout_hbm,
        gmix_ref, wfkT_ref, wqT_ref, wvT_ref, wcu_ref, wb_ref, bf_ref,
        gq_ref, gk_ref, cw_ref, gconv_ref,
        h1_ref, qT_ref, k_ref, vT_ref, f_ref, oconv_ref, km_ref, vmT_ref, fmeta_ref,
        xn_ref, acc_ref, act_ref, zc_ref, fc_ref, zc0_ref, fc0_ref,
        win_ref, wout_ref, win_stage, wout_stage, win_sem, wout_sem):
    t = pl.program_id(1)
    w = _MixWeights(g1_ref, win_ref, wout_ref, gmix_ref, wfkT_ref, wqT_ref, wvT_ref, wcu_ref, wb_ref,
                    bf_ref, gq_ref, gk_ref, cw_ref, gconv_ref)

    def put_k(h, k_aug):
        k_ref[0, h] = k_aug

    def put_v(h, v_t):
        vT_ref[0, h] = v_t

    def put_meta_k(h, k_aug):
        km_ref[h] = k_aug[0:N_META]

    def put_meta_v(h, v_t):
        vmT_ref[h] = v_t[:, 0:N_META]

    @pl.when(jnp.logical_and(pl.program_id(0) == 0, t == 0))
    def _():
        _load_weight_as_bf16(win_hbm, win_ref, win_stage, win_sem)
        _load_weight_as_bf16(wout_hbm, wout_ref, wout_stage, wout_sem, MACARON_SCALE)
        rows = meta_ref.shape[0]
        zc_ref[...] = jnp.zeros_like(zc_ref)
        fc_ref[...] = jnp.zeros_like(fc_ref)
        _token_tile(meta_ref[...], N_META, w, xn_ref.at[pl.ds(0, rows)], acc_ref.at[pl.ds(0, rows)],
                    act_ref.at[pl.ds(0, rows)], zc_ref, fc_ref, _TileOut(None, None, put_meta_k, put_meta_v, None, None))
        zc0_ref[...] = zc_ref[...]
        fc0_ref[...] = fc_ref[...]
        fmeta_ref[...] = fc_ref[...]

    @pl.when(t == 0)
    def _():
        zc_ref[...] = zc0_ref[...]
        fc_ref[...] = fc0_ref[...]

    _token_tile(x_ref[0], x_ref.shape[1], w, xn_ref, acc_ref, act_ref, zc_ref, fc_ref,
                _TileOut(h1_ref, qT_ref, put_k, put_v, f_ref, oconv_ref))


def _ffn_weight_scratch():
    return [pltpu.VMEM((D_MODEL, 2 * D_FF), BF16), pltpu.VMEM((D_FF, D_MODEL), BF16),
            pltpu.VMEM((STAGE_SLOTS, WIN_STAGE_ROWS, 2 * D_FF), F32),
            pltpu.VMEM((STAGE_SLOTS, WOUT_STAGE_ROWS, D_MODEL), F32),
            pltpu.SemaphoreType.DMA((STAGE_SLOTS,)), pltpu.SemaphoreType.DMA((STAGE_SLOTS,))]


def _const_spec(shape):
    nd = len(shape)
    return pl.BlockSpec(shape, lambda *_: (0,) * nd, pipeline_mode=pl.Buffered(1))


def _ffn1_mix_in(x, meta, p, tm):
    nb, seq, _ = x.shape
    nt = seq // tm
    assert tm % LANES == 0 and meta.shape[0] <= tm
    gq_t = jnp.broadcast_to(p["gq"][:, None], (HEAD_DIM, tm))
    gk_t = jnp.broadcast_to(p["gk"][:, None], (HEAD_DIM, tm))

    def tile3(last):
        return pl.BlockSpec((1, tm, last), lambda b, t: (b, t, 0))

    in_specs = [
        tile3(D_MODEL), _const_spec(meta.shape),
        _const_spec((1, D_MODEL)),
        pl.BlockSpec(memory_space=pl.ANY), pl.BlockSpec(memory_space=pl.ANY),
        _const_spec((1, D_MODEL)), _const_spec((F_ROWS + D_ATTN, D_MODEL)),
        _const_spec((D_ATTN, D_MODEL)), _const_spec((D_ATTN, D_MODEL)),
        _const_spec((D_MODEL, 2 * D_CONV)), _const_spec((D_MODEL, D_CONV)),
        _const_spec((F_ROWS, 1)),
        _const_spec((HEAD_DIM, tm)), _const_spec((HEAD_DIM, tm)),
        _const_spec((SUBLANES, D_CONV)), _const_spec((1, D_CONV)),
    ]
    out_shape = [
        jax.ShapeDtypeStruct((nb, seq, D_MODEL), F32),
        jax.ShapeDtypeStruct((nb, HEADS, AUG_DIM, seq), BF16),
        jax.ShapeDtypeStruct((nb, HEADS, seq, AUG_DIM), BF16),
        jax.ShapeDtypeStruct((nb, HEADS, HEAD_DIM, seq), BF16),
        jax.ShapeDtypeStruct((nb, HEADS, seq), F32),
        jax.ShapeDtypeStruct((nb, seq, D_CONV), BF16),
        jax.ShapeDtypeStruct((HEADS, N_META, AUG_DIM), BF16),
        jax.ShapeDtypeStruct((HEADS, HEAD_DIM, N_META), BF16),
        jax.ShapeDtypeStruct((F_ROWS, LANES), F32),
    ]
    out_specs = [
        tile3(D_MODEL),
        pl.BlockSpec((1, HEADS, AUG_DIM, tm), lambda b, t: (b, 0, 0, t)),
        pl.BlockSpec((1, HEADS, tm, AUG_DIM), lambda b, t: (b, 0, t, 0)),
        pl.BlockSpec((1, HEADS, HEAD_DIM, tm), lambda b, t: (b, 0, 0, t)),
        pl.BlockSpec((1, HEADS, tm), lambda b, t: (b, 0, t)),
        tile3(D_CONV),
        pl.BlockSpec((HEADS, N_META, AUG_DIM), lambda b, t: (0, 0, 0)),
        pl.BlockSpec((HEADS, HEAD_DIM, N_META), lambda b, t: (0, 0, 0)),
        pl.BlockSpec((F_ROWS, LANES), lambda b, t: (0, 0)),
    ]
    scratch_shapes = [
        pltpu.VMEM((tm, D_MODEL), BF16),
        pltpu.VMEM((tm, D_MODEL), F32),
        pltpu.VMEM((tm, D_FF), BF16),
        pltpu.VMEM((SUBLANES, D_CONV), F32),
        pltpu.VMEM((F_ROWS, LANES), F32),
        pltpu.VMEM((SUBLANES, D_CONV), F32),
        pltpu.VMEM((F_ROWS, LANES), F32),
    ] + _ffn_weight_scratch()
    return pl.pallas_call(
        _ffn1_mix_in_kernel, out_shape=out_shape, grid=(nb, nt), in_specs=in_specs, out_specs=out_specs,
        scratch_shapes=scratch_shapes, name="ffn1_mix_in",
        compiler_params=pltpu.CompilerParams(
            dimension_semantics=("arbitrary", "arbitrary"), vmem_limit_bytes=VMEM_LIMIT_BYTES),
    )(x, meta,
      p["g1"], p["win1"], p["wout1"],
      p["gmix"], p["wfkT"], p["wqT"], p["wvT"], p["wcu"], p["wb"], p["bf"],
      gq_t, gk_t, p["cw"], p["gconv"])


def _fox_attention_kernel(fq0_ref, flast_ref, fmeta_ref, cb_ref,
                          qT_ref, k_ref, vT_ref, km_ref, vmT_ref,
                          o_ref,
                          acc_ref, l_ref):
    bh0 = pl.program_id(0) * HEADS + pl.program_id(1) * HEAD_GROUP
    heads = range(HEAD_GROUP)
    nq = qT_ref.shape[3] // Q_BLOCK
    c_bound = cb_ref[0]
    key_idx = lax.broadcasted_iota(jnp.int32, (K_BLOCK, Q_BLOCK), 0)
    qry_idx = lax.broadcasted_iota(jnp.int32, (K_BLOCK, Q_BLOCK), 1)
    causal = key_idx <= qry_idx

    def block_live(g, i, j):
        return fq0_ref[bh0 + g, i] - flast_ref[bh0 + g, jnp.maximum(j, 0)] + c_bound >= EXP_UNDERFLOW

    def meta_live(g, i):
        head = pl.program_id(1) * HEAD_GROUP + g
        return fq0_ref[bh0 + g, i] - fmeta_ref[head] + c_bound >= EXP_UNDERFLOW

    def sublane_partial_sum(p):
        return jnp.sum(p.reshape(p.shape[0] // SUBLANES, SUBLANES, p.shape[1]), axis=0)

    def bounded_tile(k_blk, v_t, q_t):
        p = jnp.exp2(_dot(k_blk, q_t))
        return sublane_partial_sum(p), _dot(v_t, p.astype(BF16))

    def bounded_q_blocks(i_first):
        chains = [(r, g) for r in range(Q_PER_STEP) for g in heads]
        blk = [i_first + r for r in range(Q_PER_STEP)]
        q0 = [pl.multiple_of(i * Q_BLOCK, Q_BLOCK) for i in blk]
        p0 = [pl.multiple_of(jnp.maximum(i - 1, 0) * K_BLOCK, K_BLOCK) for i in blk]
        first_mask = jnp.where(i_first >= 1, 0.0, MASKED)

        def weights(group):
            q_t = [qT_ref[0, g, :, pl.ds(q0[r], Q_BLOCK)] for r, g in group]
            s_d = [_dot(k_ref[0, g, pl.ds(q0[r], K_BLOCK), :], q) for (r, g), q in zip(group, q_t)]
            s_p = [_dot(k_ref[0, g, pl.ds(p0[r], K_BLOCK), :], q) for (r, g), q in zip(group, q_t)]
            return ([jnp.exp2(jnp.where(causal, s, MASKED)) for s in s_d],
                    [jnp.exp2(s + first_mask if r == 0 else s) for s, (r, g) in zip(s_p, group)])

        def values(group, p):
            for (r, g), p_d, p_p in zip(group, *p):
                c = r * HEAD_GROUP + g
                l_ref[c] = sublane_partial_sum(p_d) + sublane_partial_sum(p_p)
                acc_ref[c] = (_dot(vT_ref[0, g, :, pl.ds(q0[r], K_BLOCK)], p_d.astype(BF16))
                              + _dot(vT_ref[0, g, :, pl.ds(p0[r], K_BLOCK)], p_p.astype(BF16)))

        groups = [chains[n:n + CHAIN_GROUP] for n in range(0, len(chains), CHAIN_GROUP)]
        pending = []
        for n, group in enumerate(groups):
            pending.append((group, weights(group)))
            if n >= VALUE_DOT_LAG:
                values(*pending.pop(0))
        for item in pending:
            values(*item)

        def any_head(test):
            hit = test(0)
            for g in heads[1:]:
                hit = jnp.logical_or(hit, test(g))
            return hit

        def walk(r):
            i = blk[r]

            def cond(j):
                return jnp.logical_and(j >= 0, any_head(lambda g: block_live(g, i, j)))

            def body(j):
                k0 = pl.multiple_of(j * K_BLOCK, K_BLOCK)
                for g in heads:
                    c = r * HEAD_GROUP + g
                    l_j, acc_j = bounded_tile(k_ref[0, g, pl.ds(k0, K_BLOCK), :],
                                              vT_ref[0, g, :, pl.ds(k0, K_BLOCK)],
                                              qT_ref[0, g, :, pl.ds(q0[r], Q_BLOCK)])
                    l_ref[c] += l_j
                    acc_ref[c] += acc_j
                return j - 1

            lax.while_loop(cond, body, i - 2)

            @pl.when(any_head(lambda g: meta_live(g, i)))
            def _():
                for g in heads:
                    c = r * HEAD_GROUP + g
                    l_m, acc_m = bounded_tile(km_ref[g], vmT_ref[g], qT_ref[0, g, :, pl.ds(q0[r], Q_BLOCK)])
                    l_ref[c] += l_m
                    acc_ref[c] += acc_m

        walks = [jnp.logical_or(jnp.logical_and(i >= 2, any_head(lambda g, i=i: block_live(g, i, i - 2))),
                                any_head(lambda g, i=i: meta_live(g, i))) for i in blk]
        any_walk = walks[0]
        for hit in walks[1:]:
            any_walk = jnp.logical_or(any_walk, hit)

        @pl.when(any_walk)
        def _():
            for r in range(Q_PER_STEP):
                walk(r)

        for c, (r, g) in enumerate(chains):
            l_tot = jnp.sum(l_ref[c], axis=0, keepdims=True)
            o_ref[0, g, :, pl.ds(q0[r], Q_BLOCK)] = (acc_ref[c] / l_tot).astype(o_ref.dtype)

    @pl.when(c_bound <= BOUNDED_LOGIT_MAX)
    def _():
        def step(n, carry):
            bounded_q_blocks(n * Q_PER_STEP)
            return carry

        lax.fori_loop(0, nq // Q_PER_STEP, step, 0)

    def online_q_block(g, i):
        q0 = pl.multiple_of(i * Q_BLOCK, Q_BLOCK)
        q_t = qT_ref[0, g, :, pl.ds(q0, Q_BLOCK)]
        s_m = _dot(km_ref[g], q_t)
        m = jnp.max(s_m, axis=0, keepdims=True)
        p_m = jnp.exp2(s_m - m)
        l = jnp.sum(p_m, axis=0, keepdims=True)
        acc = _dot(vmT_ref[g], p_m.astype(BF16))

        def online_step(state, k0, masked):
            m, l, acc = state
            s = _dot(k_ref[0, g, pl.ds(k0, K_BLOCK), :], q_t)
            if masked:
                s = jnp.where(causal, s, MASKED)
            m_new = jnp.maximum(m, jnp.max(s, axis=0, keepdims=True))
            alpha = jnp.exp2(m - m_new)
            p = jnp.exp2(s - m_new)
            l = alpha * l + jnp.sum(p, axis=0, keepdims=True)
            acc = alpha * acc + _dot(vT_ref[0, g, :, pl.ds(k0, K_BLOCK)], p.astype(BF16))
            return m_new, l, acc

        state = online_step((m, l, acc), q0, True)

        def cond(carry):
            return jnp.logical_and(carry[0] >= 0, block_live(g, i, carry[0]))

        def body(carry):
            j = carry[0]
            return (j - 1,) + online_step(carry[1:], pl.multiple_of(j * K_BLOCK, K_BLOCK), False)

        _, m, l, acc = lax.while_loop(cond, body, (i - 1,) + state)
        o_ref[0, g, :, pl.ds(q0, Q_BLOCK)] = (acc / l).astype(o_ref.dtype)

    @pl.when(c_bound > BOUNDED_LOGIT_MAX)
    def _():
        def head_loop(g, carry):
            def step(i, inner):
                online_q_block(g, i)
                return inner

            lax.fori_loop(0, nq, step, 0)
            return carry

        lax.fori_loop(0, HEAD_GROUP, head_loop, 0)


def _fox_attention(qT, k, vT, k_meta, vT_meta, fq0, flast, fmeta, c_bound):
    nb, _, _, seq = qT.shape
    assert seq % (Q_PER_STEP * Q_BLOCK) == 0 and Q_BLOCK == K_BLOCK
    smem = pl.BlockSpec(memory_space=pltpu.SMEM)
    hg = HEAD_GROUP
    return pl.pallas_call(
        _fox_attention_kernel,
        out_shape=jax.ShapeDtypeStruct((nb, HEADS, HEAD_DIM, seq), BF16),
        grid=(nb, HEADS // hg),
        in_specs=[
            smem, smem, smem, smem,
            pl.BlockSpec((1, hg, AUG_DIM, seq), lambda b, h: (b, h, 0, 0)),
            pl.BlockSpec((1, hg, seq, AUG_DIM), lambda b, h: (b, h, 0, 0)),
            pl.BlockSpec((1, hg, HEAD_DIM, seq), lambda b, h: (b, h, 0, 0)),
            pl.BlockSpec((hg, N_META, AUG_DIM), lambda b, h: (h, 0, 0)),
            pl.BlockSpec((hg, HEAD_DIM, N_META), lambda b, h: (h, 0, 0)),
        ],
        out_specs=pl.BlockSpec((1, hg, HEAD_DIM, seq), lambda b, h: (b, h, 0, 0)),
        scratch_shapes=[pltpu.VMEM((Q_PER_STEP * hg, HEAD_DIM, Q_BLOCK), F32),
                        pltpu.VMEM((Q_PER_STEP * hg, SUBLANES, Q_BLOCK), F32)],
        name="fox_attention",
        compiler_params=pltpu.CompilerParams(
            dimension_semantics=("arbitrary", "arbitrary"), vmem_limit_bytes=VMEM_LIMIT_BYTES),
    )(fq0, flast, fmeta, c_bound, qT, k, vT, k_meta, vT_meta)


def _mix_out_ffn2_kernel(h1_ref, oT_ref, oconv_ref,
                         gattn_ref, woa_ref, woc_ref,
                         g2_ref, win_hbm, wout_hbm, gfin_ref,
                         out_ref,
                         xn_ref, acc_ref, act_ref, win_ref, wout_ref, stage_in, stage_out, sem):
    def tile(prepare):
        o_t = oT_ref[0].astype(F32)
        ms = jnp.mean(o_t * o_t, axis=0, keepdims=True)
        o_n = (o_t * lax.rsqrt(ms + EPS) * gattn_ref[...]).T.astype(BF16)
        mix = _dot(o_n, woa_ref[...]) + _dot(oconv_ref[0], woc_ref[...])
        h2 = h1_ref[0] + mix
        xn_ref[...] = _rmsnorm_rows(h2, g2_ref[...]).astype(BF16)
        _swiglu(xn_ref, win_ref, wout_ref, acc_ref, act_ref, h2, prepare)
        out_ref[0] = _rmsnorm_rows(acc_ref[...], gfin_ref[...])

    first = jnp.logical_and(pl.program_id(0) == 0, pl.program_id(1) == 0)

    @pl.when(first)
    def _():
        tile(_stream_ffn_weights(win_hbm, wout_hbm, win_ref, wout_ref, stage_in, stage_out, sem))

    @pl.when(jnp.logical_not(first))
    def _():
        tile(None)


def _mix_out_ffn2(h1, o_t, oconv, p, tm):
    nb, seq, _ = h1.shape
    gattn_t = jnp.broadcast_to(p["gattn"][:, None], (D_ATTN, tm))
    in_specs = [
        pl.BlockSpec((1, tm, D_MODEL), lambda b, t: (b, t, 0)),
        pl.BlockSpec((1, D_ATTN, tm), lambda b, t: (b, 0, t)),
        pl.BlockSpec((1, tm, D_CONV), lambda b, t: (b, t, 0)),
        _const_spec((D_ATTN, tm)), _const_spec((D_ATTN, D_MODEL)), _const_spec((D_CONV, D_MODEL)),
        _const_spec((1, D_MODEL)),
        pl.BlockSpec(memory_space=pl.ANY), pl.BlockSpec(memory_space=pl.ANY),
        _const_spec((1, D_MODEL)),
    ]
    return pl.pallas_call(
        _mix_out_ffn2_kernel,
        out_shape=jax.ShapeDtypeStruct((nb, seq, D_MODEL), F32),
        grid=(nb, seq // tm),
        in_specs=in_specs,
        out_specs=pl.BlockSpec((1, tm, D_MODEL), lambda b, t: (b, t, 0)),
        scratch_shapes=[pltpu.VMEM((tm, D_MODEL), BF16), pltpu.VMEM((tm, D_MODEL), F32),
                        pltpu.VMEM((tm, D_FF), BF16),
                        pltpu.VMEM((D_MODEL, 2 * D_FF), BF16), pltpu.VMEM((D_FF, D_MODEL), BF16),
                        pltpu.VMEM((STAGE_SLOTS, 2, D_MODEL, FF_CHUNK), F32),
                        pltpu.VMEM((STAGE_SLOTS, FF_CHUNK, D_MODEL), F32),
                        pltpu.SemaphoreType.DMA((STAGE_SLOTS, 3))],
        name="mix_out_ffn2",
        compiler_params=pltpu.CompilerParams(
            dimension_semantics=("arbitrary", "arbitrary"), vmem_limit_bytes=VMEM_LIMIT_BYTES),
    )(h1, o_t, oconv, gattn_t, p["woa"], p["woc"], p["g2"], p["win2"], p["wout2"], p["gfin"])


def kernel(x, meta_tokens, ffn1_norm, ffn1_w_in, ffn1_w_out, mix_norm, w_mix_in, b_forget, q_norm, k_norm, conv_w, attn_out_norm, conv_out_norm, w_mix_out, ffn2_norm, ffn2_w_in, ffn2_w_out, final_norm):
    nb, seq, _ = x.shape
    wmix = w_mix_in[0]
    n_qkv = 3 * D_ATTN
    p = {
        "g1": ffn1_norm, "win1": ffn1_w_in[0], "wout1": ffn1_w_out[0],
        "gmix": mix_norm,
        "wfkT": jnp.concatenate([jnp.pad(wmix[:, n_qkv:n_qkv + HEADS].T, ((0, F_ROWS - HEADS), (0, 0))),
                                 wmix[:, D_ATTN:2 * D_ATTN].T], axis=0).astype(BF16),
        "wqT": wmix[:, :D_ATTN].T.astype(BF16),
        "wvT": wmix[:, 2 * D_ATTN:n_qkv].T.astype(BF16),
        "wcu": wmix[:, n_qkv + HEADS + D_CONV:].astype(BF16),
        "wb": wmix[:, n_qkv + HEADS:n_qkv + HEADS + D_CONV].astype(BF16),
        "bf": jnp.pad(b_forget[0], (0, F_ROWS - HEADS))[:, None],
        "gq": q_norm[0], "gk": k_norm[0],
        "cw": jnp.pad(conv_w[0], ((0, SUBLANES - CONV_WIDTH), (0, 0))),
        "gconv": conv_out_norm,
        "gattn": attn_out_norm[0],
        "woa": w_mix_out[0, :D_ATTN].astype(BF16), "woc": w_mix_out[0, D_ATTN:].astype(BF16),
        "g2": ffn2_norm, "win2": ffn2_w_in[0], "wout2": ffn2_w_out[0],
        "gfin": final_norm,
    }

    meta = jnp.pad(meta_tokens.astype(x.dtype), ((0, META_TILE - N_META), (0, 0)))
    h1, qT, k, vT, f_cum, oconv, k_meta, vT_meta, f_meta = _ffn1_mix_in(x, meta, p, TOKEN_TILE)

    fq0 = f_cum[:, :, 0::Q_BLOCK].reshape(nb * HEADS, seq // Q_BLOCK)
    flast = f_cum[:, :, K_BLOCK - 1::K_BLOCK].reshape(nb * HEADS, seq // K_BLOCK)
    c_bound = (2.0 * HEAD_DIM ** 0.5 * SCORE_BOUND_SLACK * jnp.max(jnp.abs(q_norm)) * jnp.max(jnp.abs(k_norm))
               + SCORE_BOUND_MARGIN).reshape(1).astype(F32)

    o_t = _fox_attention(qT, k, vT, k_meta, vT_meta, fq0, flast, f_meta[:HEADS, 0], c_bound)
    o_t = o_t.reshape(nb, D_ATTN, seq)
    return _mix_out_ffn2(h1, o_t, oconv, p, TOKEN_TILE)
```

```python
from typing import Any, NamedTuple

import jax
import jax.numpy as jnp
from jax import lax
from jax.experimental import pallas as pl
from jax.experimental.pallas import tpu as pltpu

D_MODEL = 1024
N_META = 16
D_ATTN = 512
D_CONV = 512
HEADS = 8
HEAD_DIM = 64
CONV_WIDTH = 3
D_FF = 2816
EPS = 1e-6

F32 = jnp.float32
BF16 = jnp.bfloat16

LANES = 128
SUBLANES = 8
MXU_DIM = 256
VMEM_LIMIT_BYTES = 60000 * 1024

TOKEN_TILE = 512
META_TILE = LANES
FF_CHUNK = MXU_DIM
N_FF_CHUNKS = D_FF // FF_CHUNK
Q_BLOCK = MXU_DIM
K_BLOCK = MXU_DIM
AUG_DIM = LANES
HEAD_GROUP = 4
Q_PER_STEP = 8
CHAIN_GROUP = 2
VALUE_DOT_LAG = 2
MASKED = -1e30
LOG2E = 1.4426950408889634
F_ROWS = 2 * SUBLANES
BIAS_PIECES = 3
SCORE_BOUND_SLACK = 1.02
SCORE_BOUND_MARGIN = 1.0
MACARON_SCALE = 0.5
STAGE_SLOTS = 4

EXP_UNDERFLOW = -104.0
BOUNDED_LOGIT_MAX = 120.0

assert D_FF % FF_CHUNK == 0 and FF_CHUNK % LANES == 0 and HEADS % HEAD_GROUP == 0


def _dot(a, b):
    return jnp.dot(a, b, preferred_element_type=F32)


def _dot_nt(a, b):
    return lax.dot_general(a, b, (((1,), (1,)), ((), ())), preferred_element_type=F32)


def _rmsnorm_rows(x, gain):
    ms = jnp.mean(x * x, axis=-1, keepdims=True)
    return x * lax.rsqrt(ms + EPS) * gain


def _split3(x):
    hi = x.astype(BF16)
    r1 = x - hi.astype(F32)
    mid = r1.astype(BF16)
    lo = (r1 - mid.astype(F32)).astype(BF16)
    return hi, mid, lo


def _stream_ffn_weights(win_hbm, wout_hbm, win_ref, wout_ref, stage_in, stage_out, sem):
    n_slots = stage_out.shape[0]

    def copies(c):
        slot, lo = c % n_slots, c * FF_CHUNK
        return (pltpu.make_async_copy(win_hbm.at[:, pl.ds(lo, FF_CHUNK)], stage_in.at[slot, 0], sem.at[slot, 0]),
                pltpu.make_async_copy(win_hbm.at[:, pl.ds(D_FF + lo, FF_CHUNK)], stage_in.at[slot, 1],
                                      sem.at[slot, 1]),
                pltpu.make_async_copy(wout_hbm.at[pl.ds(lo, FF_CHUNK)], stage_out.at[slot], sem.at[slot, 2]))

    def start(c):
        for copy in copies(c):
            copy.start()

    for c in range(min(n_slots, N_FF_CHUNKS)):
        start(c)

    def prepare(c):
        slot, lo, hi = c % n_slots, c * FF_CHUNK, (c + 1) * FF_CHUNK
        for copy in copies(c):
            copy.wait()
        win_ref[:, lo:hi] = stage_in[slot, 0].astype(BF16)
        win_ref[:, D_FF + lo:D_FF + hi] = stage_in[slot, 1].astype(BF16)
        wout_ref[lo:hi, :] = (stage_out[slot] * MACARON_SCALE).astype(BF16)
        if c + n_slots < N_FF_CHUNKS:
            start(c + n_slots)

    return prepare


def _swiglu(xn_ref, win_ref, wout_ref, acc_ref, act_ref, residual, prepare=None):
    for c in range(N_FF_CHUNKS):
        lo, hi = c * FF_CHUNK, (c + 1) * FF_CHUNK
        if prepare is not None:
            prepare(c)
        xn = xn_ref[...]
        g = _dot(xn, win_ref[:, lo:hi])
        u = _dot(xn, win_ref[:, D_FF + lo:D_FF + hi])
        act_ref[:, lo:hi] = (g * jax.nn.sigmoid(g) * u).astype(BF16)
    acc_ref[...] = residual + _dot(act_ref[...], wout_ref[...])


class _MixWeights(NamedTuple):
    g1: Any
    win: Any
    wout: Any
    gmix: Any
    wfkT: Any
    wqT: Any
    wvT: Any
    wcu: Any
    wb: Any
    bf: Any
    gq: Any
    gk: Any
    cw: Any
    gconv: Any


def _lane_cumsum(x):
    lane = lax.broadcasted_iota(jnp.int32, x.shape, 1)
    shift = 1
    while shift < x.shape[1]:
        x = x + jnp.where(lane >= shift, pltpu.roll(x, shift, axis=1), 0.0)
        shift *= 2
    return x


class _TileOut(NamedTuple):
    h1: Any
    q: Any
    k: Any
    v: Any
    f: Any
    oconv: Any


def _token_tile(x, n_valid, w, xn_ref, acc_ref, act_ref, zc_ref, fc_ref, out, prepare=None):
    tm = x.shape[0]

    xn_ref[...] = _rmsnorm_rows(x, w.g1[...]).astype(BF16)
    _swiglu(xn_ref, w.win, w.wout, acc_ref, act_ref, x, prepare)
    h1 = acc_ref[...]
    if out.h1 is not None:
        out.h1[0] = h1

    xn2 = _rmsnorm_rows(h1, w.gmix[...]).astype(BF16)
    fkT = _dot_nt(w.wfkT[...], xn2)
    cu = _dot(xn2, w.wcu[...])

    fl = fkT[0:F_ROWS] + w.bf[...]
    logf = jnp.minimum(fl, 0.0) - jnp.log(1.0 + jnp.exp(-jnp.abs(fl)))
    if n_valid < tm:
        logf = jnp.where(lax.broadcasted_iota(jnp.int32, logf.shape, 1) < n_valid, logf, 0.0)
    csum = _lane_cumsum(logf)
    f_all = csum + jnp.concatenate([fc_ref[...]] * (tm // LANES), axis=1)
    fc_ref[...] = jnp.broadcast_to(f_all[:, tm - 1:tm], fc_ref.shape)
    if out.f is not None:
        out.f[0] = f_all[:HEADS]

    f_hi, f_mid, f_lo = (p.astype(F32) for p in _split3(f_all * LOG2E))
    row = lax.broadcasted_iota(jnp.int32, (SUBLANES, tm), 0)
    ones_mid = jnp.where(row < 2 * BIAS_PIECES, 1.0, 0.0)

    def bias_rows(pieces, first_row, sign):
        rows = ones_mid
        for i, piece in enumerate(pieces):
            rows = jnp.where(row == first_row + i, sign * piece, rows)
        return rows
    pad_rows = jnp.zeros((AUG_DIM - HEAD_DIM - SUBLANES, tm), F32)

    def head_pieces(h):
        return (jnp.broadcast_to(f_hi[h:h + 1], (SUBLANES, tm)),
                jnp.broadcast_to(f_mid[h:h + 1], (SUBLANES, tm)),
                jnp.broadcast_to(f_lo[h:h + 1], (SUBLANES, tm)))

    def head_rmsnorm(x_t, gain_t):
        x3 = x_t.reshape(HEADS, HEAD_DIM, tm)
        return x3 * lax.rsqrt(jnp.mean(x3 * x3, axis=1, keepdims=True) + EPS) * gain_t[None]

    kn = head_rmsnorm(fkT[F_ROWS:F_ROWS + D_ATTN], w.gk[:, 0:tm])
    if out.q is not None:
        qT = _dot_nt(w.wqT[...], xn2)
    if out.oconv is not None:
        gate_b = _dot(xn2, w.wb[...])
    for h in range(HEADS):
        aug_k = bias_rows(head_pieces(h), BIAS_PIECES, -1.0)
        k_aug_t = jnp.concatenate([kn[h], aug_k, pad_rows], axis=0)
        out.k(h, k_aug_t.T.astype(BF16))

    vT = _dot_nt(w.wvT[...], xn2)
    z = cu[:, 0:D_CONV] * cu[:, D_CONV:2 * D_CONV]
    if out.oconv is not None:
        zc = zc_ref[...]
        rowz = lax.broadcasted_iota(jnp.int32, z.shape, 0)
        prev1 = jnp.broadcast_to(zc[SUBLANES - 1:SUBLANES], z.shape)
        prev2 = jnp.broadcast_to(zc[SUBLANES - 2:SUBLANES - 1], z.shape)
        z1 = jnp.where(rowz == 0, prev1, pltpu.roll(z, 1, axis=0))
        z2 = jnp.where(rowz == 0, prev2, jnp.where(rowz == 1, prev1, pltpu.roll(z, 2, axis=0)))
        cw = w.cw[...]
        y = cw[0:1] * z2 + cw[1:2] * z1 + cw[2:3] * z
        out.oconv[0] = _rmsnorm_rows(gate_b * y, w.gconv[...]).astype(BF16)
    zc_ref[...] = z[n_valid - SUBLANES:n_valid]

    if out.q is not None:
        qn = head_rmsnorm(qT, w.gq[:, 0:tm] * (HEAD_DIM ** -0.5 * LOG2E))
    v3 = vT.reshape(HEADS, HEAD_DIM, tm)
    for h in range(HEADS):
        if out.q is not None:
            aug_q = bias_rows(head_pieces(h), 0, 1.0)
            out.q[0, h] = jnp.concatenate([qn[h], aug_q, pad_rows], axis=0).astype(BF16)
        out.v(h, v3[h].astype(BF16))


def _ffn1_mix_in_kernel(
        x_ref, meta_ref,
        g1_ref, win_hbm, wout_hbm,
        gmix_ref, wfkT_ref, wqT_ref, wvT_ref, wcu_ref, wb_ref, bf_ref,
        gq_ref, gk_ref, cw_ref, gconv_ref,
        h1_ref, qT_ref, k_ref, vT_ref, f_ref, oconv_ref, km_ref, vmT_ref, fmeta_ref,
        xn_ref, acc_ref, act_ref, zc_ref, fc_ref, zc0_ref, fc0_ref,
        win_ref, wout_ref, stage_in, stage_out, sem):
    t = pl.program_id(1)
    w = _MixWeights(g1_ref, win_ref, wout_ref, gmix_ref, wfkT_ref, wqT_ref, wvT_ref, wcu_ref, wb_ref,
                    bf_ref, gq_ref, gk_ref, cw_ref, gconv_ref)

    def put_k(h, k_aug):
        k_ref[0, h] = k_aug

    def put_v(h, v_t):
        vT_ref[0, h] = v_t

    def put_meta_k(h, k_aug):
        km_ref[h] = k_aug[0:N_META]

    def put_meta_v(h, v_t):
        vmT_ref[h] = v_t[:, 0:N_META]

    @pl.when(jnp.logical_and(pl.program_id(0) == 0, t == 0))
    def _():
        prepare = _stream_ffn_weights(win_hbm, wout_hbm, win_ref, wout_ref, stage_in, stage_out, sem)
        rows = meta_ref.shape[0]
        zc_ref[...] = jnp.zeros_like(zc_ref)
        fc_ref[...] = jnp.zeros_like(fc_ref)
        _token_tile(meta_ref[...], N_META, w, xn_ref.at[pl.ds(0, rows)], acc_ref.at[pl.ds(0, rows)],
                    act_ref.at[pl.ds(0, rows)], zc_ref, fc_ref,
                    _TileOut(None, None, put_meta_k, put_meta_v, None, None), prepare)
        zc0_ref[...] = zc_ref[...]
        fc0_ref[...] = fc_ref[...]
        fmeta_ref[...] = fc_ref[...]

    @pl.when(t == 0)
    def _():
        zc_ref[...] = zc0_ref[...]
        fc_ref[...] = fc0_ref[...]

    _token_tile(x_ref[0], x_ref.shape[1], w, xn_ref, acc_ref, act_ref, zc_ref, fc_ref,
                _TileOut(h1_ref, qT_ref, put_k, put_v, f_ref, oconv_ref))


def _ffn_weight_scratch(slots):
    return [pltpu.VMEM((D_MODEL, 2 * D_FF), BF16), pltpu.VMEM((D_FF, D_MODEL), BF16),
            pltpu.VMEM((slots, 2, D_MODEL, FF_CHUNK), F32),
            pltpu.VMEM((slots, FF_CHUNK, D_MODEL), F32),
            pltpu.SemaphoreType.DMA((slots, 3))]


def _const_spec(shape):
    nd = len(shape)
    return pl.BlockSpec(shape, lambda *_: (0,) * nd, pipeline_mode=pl.Buffered(1))


def _ffn1_mix_in(x, meta, p, tm):
    nb, seq, _ = x.shape
    nt = seq // tm
    assert tm % LANES == 0 and meta.shape[0] <= tm
    gq_t = jnp.broadcast_to(p["gq"][:, None], (HEAD_DIM, tm))
    gk_t = jnp.broadcast_to(p["gk"][:, None], (HEAD_DIM, tm))

    def tile3(last):
        return pl.BlockSpec((1, tm, last), lambda b, t: (b, t, 0))

    in_specs = [
        tile3(D_MODEL), _const_spec(meta.shape),
        _const_spec((1, D_MODEL)),
        pl.BlockSpec(memory_space=pl.ANY), pl.BlockSpec(memory_space=pl.ANY),
        _const_spec((1, D_MODEL)), _const_spec((F_ROWS + D_ATTN, D_MODEL)),
        _const_spec((D_ATTN, D_MODEL)), _const_spec((D_ATTN, D_MODEL)),
        _const_spec((D_MODEL, 2 * D_CONV)), _const_spec((D_MODEL, D_CONV)),
        _const_spec((F_ROWS, 1)),
        _const_spec((HEAD_DIM, tm)), _const_spec((HEAD_DIM, tm)),
        _const_spec((SUBLANES, D_CONV)), _const_spec((1, D_CONV)),
    ]
    out_shape = [
        jax.ShapeDtypeStruct((nb, seq, D_MODEL), F32),
        jax.ShapeDtypeStruct((nb, HEADS, AUG_DIM, seq), BF16),
        jax.ShapeDtypeStruct((nb, HEADS, seq, AUG_DIM), BF16),
        jax.ShapeDtypeStruct((nb, HEADS, HEAD_DIM, seq), BF16),
        jax.ShapeDtypeStruct((nb, HEADS, seq), F32),
        jax.ShapeDtypeStruct((nb, seq, D_CONV), BF16),
        jax.ShapeDtypeStruct((HEADS, N_META, AUG_DIM), BF16),
        jax.ShapeDtypeStruct((HEADS, HEAD_DIM, N_META), BF16),
        jax.ShapeDtypeStruct((F_ROWS, LANES), F32),
    ]
    out_specs = [
        tile3(D_MODEL),
        pl.BlockSpec((1, HEADS, AUG_DIM, tm), lambda b, t: (b, 0, 0, t)),
        pl.BlockSpec((1, HEADS, tm, AUG_DIM), lambda b, t: (b, 0, t, 0)),
        pl.BlockSpec((1, HEADS, HEAD_DIM, tm), lambda b, t: (b, 0, 0, t)),
        pl.BlockSpec((1, HEADS, tm), lambda b, t: (b, 0, t)),
        tile3(D_CONV),
        pl.BlockSpec((HEADS, N_META, AUG_DIM), lambda b, t: (0, 0, 0)),
        pl.BlockSpec((HEADS, HEAD_DIM, N_META), lambda b, t: (0, 0, 0)),
        pl.BlockSpec((F_ROWS, LANES), lambda b, t: (0, 0)),
    ]
    scratch_shapes = [
        pltpu.VMEM((tm, D_MODEL), BF16),
        pltpu.VMEM((tm, D_MODEL), F32),
        pltpu.VMEM((tm, D_FF), BF16),
        pltpu.VMEM((SUBLANES, D_CONV), F32),
        pltpu.VMEM((F_ROWS, LANES), F32),
        pltpu.VMEM((SUBLANES, D_CONV), F32),
        pltpu.VMEM((F_ROWS, LANES), F32),
    ] + _ffn_weight_scratch(STAGE_SLOTS - 1)
    return pl.pallas_call(
        _ffn1_mix_in_kernel, out_shape=out_shape, grid=(nb, nt), in_specs=in_specs, out_specs=out_specs,
        scratch_shapes=scratch_shapes, name="ffn1_mix_in",
        compiler_params=pltpu.CompilerParams(
            dimension_semantics=("arbitrary", "arbitrary"), vmem_limit_bytes=VMEM_LIMIT_BYTES),
    )(x, meta,
      p["g1"], p["win1"], p["wout1"],
      p["gmix"], p["wfkT"], p["wqT"], p["wvT"], p["wcu"], p["wb"], p["bf"],
      gq_t, gk_t, p["cw"], p["gconv"])


def _fox_attention_kernel(fq0_ref, flast_ref, fmeta_ref, cb_ref,
                          qT_ref, k_ref, vT_ref, km_ref, vmT_ref,
                          o_ref,
                          acc_ref, l_ref):
    bh0 = pl.program_id(0) * HEADS + pl.program_id(1) * HEAD_GROUP
    heads = range(HEAD_GROUP)
    nq = qT_ref.shape[3] // Q_BLOCK
    c_bound = cb_ref[0]
    key_idx = lax.broadcasted_iota(jnp.int32, (K_BLOCK, Q_BLOCK), 0)
    qry_idx = lax.broadcasted_iota(jnp.int32, (K_BLOCK, Q_BLOCK), 1)
    causal = key_idx <= qry_idx

    def block_live(g, i, j):
        return fq0_ref[bh0 + g, i] - flast_ref[bh0 + g, jnp.maximum(j, 0)] + c_bound >= EXP_UNDERFLOW

    def meta_live(g, i):
        head = pl.program_id(1) * HEAD_GROUP + g
        return fq0_ref[bh0 + g, i] - fmeta_ref[head] + c_bound >= EXP_UNDERFLOW

    def sublane_partial_sum(p):
        return jnp.sum(p.reshape(p.shape[0] // SUBLANES, SUBLANES, p.shape[1]), axis=0)

    def bounded_tile(k_blk, v_t, q_t):
        p = jnp.exp2(_dot(k_blk, q_t))
        return sublane_partial_sum(p), _dot(v_t, p.astype(BF16))

    def bounded_q_blocks(i_first):
        chains = [(r, g) for r in range(Q_PER_STEP) for g in heads]
        blk = [i_first + r for r in range(Q_PER_STEP)]
        q0 = [pl.multiple_of(i * Q_BLOCK, Q_BLOCK) for i in blk]
        p0 = [pl.multiple_of(jnp.maximum(i - 1, 0) * K_BLOCK, K_BLOCK) for i in blk]
        first_mask = jnp.where(i_first >= 1, 0.0, MASKED)

        def weights(group):
            q_t = [qT_ref[0, g, :, pl.ds(q0[r], Q_BLOCK)] for r, g in group]
            s_d = [_dot(k_ref[0, g, pl.ds(q0[r], K_BLOCK), :], q) for (r, g), q in zip(group, q_t)]
            s_p = [_dot(k_ref[0, g, pl.ds(p0[r], K_BLOCK), :], q) for (r, g), q in zip(group, q_t)]
            return ([jnp.exp2(jnp.where(causal, s, MASKED)) for s in s_d],
                    [jnp.exp2(s + first_mask if r == 0 else s) for s, (r, g) in zip(s_p, group)])

        def values(group, p):
            for (r, g), p_d, p_p in zip(group, *p):
                c = r * HEAD_GROUP + g
                l_ref[c] = sublane_partial_sum(p_d) + sublane_partial_sum(p_p)
                acc_ref[c] = (_dot(vT_ref[0, g, :, pl.ds(q0[r], K_BLOCK)], p_d.astype(BF16))
                              + _dot(vT_ref[0, g, :, pl.ds(p0[r], K_BLOCK)], p_p.astype(BF16)))

        groups = [chains[n:n + CHAIN_GROUP] for n in range(0, len(chains), CHAIN_GROUP)]
        pending = []
        for n, group in enumerate(groups):
            pending.append((group, weights(group)))
            if n >= VALUE_DOT_LAG:
                values(*pending.pop(0))
        for item in pending:
            values(*item)

        def any_head(test):
            hit = test(0)
            for g in heads[1:]:
                hit = jnp.logical_or(hit, test(g))
            return hit

        def walk(r):
            i = blk[r]

            def cond(j):
                return jnp.logical_and(j >= 0, any_head(lambda g: block_live(g, i, j)))

            def body(j):
                k0 = pl.multiple_of(j * K_BLOCK, K_BLOCK)
                for g in heads:
                    c = r * HEAD_GROUP + g
                    l_j, acc_j = bounded_tile(k_ref[0, g, pl.ds(k0, K_BLOCK), :],
                                              vT_ref[0, g, :, pl.ds(k0, K_BLOCK)],
                                              qT_ref[0, g, :, pl.ds(q0[r], Q_BLOCK)])
                    l_ref[c] += l_j
                    acc_ref[c] += acc_j
                return j - 1

            lax.while_loop(cond, body, i - 2)

            @pl.when(any_head(lambda g: meta_live(g, i)))
            def _():
                for g in heads:
                    c = r * HEAD_GROUP + g
                    l_m, acc_m = bounded_tile(km_ref[g], vmT_ref[g], qT_ref[0, g, :, pl.ds(q0[r], Q_BLOCK)])
                    l_ref[c] += l_m
                    acc_ref[c] += acc_m

        walks = [jnp.logical_or(jnp.logical_and(i >= 2, any_head(lambda g, i=i: block_live(g, i, i - 2))),
                                any_head(lambda g, i=i: meta_live(g, i))) for i in blk]
        any_walk = walks[0]
        for hit in walks[1:]:
            any_walk = jnp.logical_or(any_walk, hit)

        @pl.when(any_walk)
        def _():
            for r in range(Q_PER_STEP):
                walk(r)

        for c, (r, g) in enumerate(chains):
            l_tot = jnp.sum(l_ref[c], axis=0, keepdims=True)
            o_ref[0, g, :, pl.ds(q0[r], Q_BLOCK)] = (acc_ref[c] / l_tot).astype(o_ref.dtype)

    @pl.when(c_bound <= BOUNDED_LOGIT_MAX)
    def _():
        def step(n, carry):
            bounded_q_blocks(n * Q_PER_STEP)
            return carry

        lax.fori_loop(0, nq // Q_PER_STEP, step, 0)

    def online_q_block(g, i):
        q0 = pl.multiple_of(i * Q_BLOCK, Q_BLOCK)
        q_t = qT_ref[0, g, :, pl.ds(q0, Q_BLOCK)]
        s_m = _dot(km_ref[g], q_t)
        m = jnp.max(s_m, axis=0, keepdims=True)
        p_m = jnp.exp2(s_m - m)
        l = jnp.sum(p_m, axis=0, keepdims=True)
        acc = _dot(vmT_ref[g], p_m.astype(BF16))

        def online_step(state, k0, masked):
            m, l, acc = state
            s = _dot(k_ref[0, g, pl.ds(k0, K_BLOCK), :], q_t)
            if masked:
                s = jnp.where(causal, s, MASKED)
            m_new = jnp.maximum(m, jnp.max(s, axis=0, keepdims=True))
            alpha = jnp.exp2(m - m_new)
            p = jnp.exp2(s - m_new)
            l = alpha * l + jnp.sum(p, axis=0, keepdims=True)
            acc = alpha * acc + _dot(vT_ref[0, g, :, pl.ds(k0, K_BLOCK)], p.astype(BF16))
            return m_new, l, acc

        state = online_step((m, l, acc), q0, True)

        def cond(carry):
            return jnp.logical_and(carry[0] >= 0, block_live(g, i, carry[0]))

        def body(carry):
            j = carry[0]
            return (j - 1,) + online_step(carry[1:], pl.multiple_of(j * K_BLOCK, K_BLOCK), False)

        _, m, l, acc = lax.while_loop(cond, body, (i - 1,) + state)
        o_ref[0, g, :, pl.ds(q0, Q_BLOCK)] = (acc / l).astype(o_ref.dtype)

    @pl.when(c_bound > BOUNDED_LOGIT_MAX)
    def _():
        def head_loop(g, carry):
            def step(i, inner):
                online_q_block(g, i)
                return inner

            lax.fori_loop(0, nq, step, 0)
            return carry

        lax.fori_loop(0, HEAD_GROUP, head_loop, 0)


def _fox_attention(qT, k, vT, k_meta, vT_meta, fq0, flast, fmeta, c_bound):
    nb, _, _, seq = qT.shape
    assert seq % (Q_PER_STEP * Q_BLOCK) == 0 and Q_BLOCK == K_BLOCK
    smem = pl.BlockSpec(memory_space=pltpu.SMEM)
    hg = HEAD_GROUP
    return pl.pallas_call(
        _fox_attention_kernel,
        out_shape=jax.ShapeDtypeStruct((nb, HEADS, HEAD_DIM, seq), BF16),
        grid=(nb, HEADS // hg),
        in_specs=[
            smem, smem, smem, smem,
            pl.BlockSpec((1, hg, AUG_DIM, seq), lambda b, h: (b, h, 0, 0)),
            pl.BlockSpec((1, hg, seq, AUG_DIM), lambda b, h: (b, h, 0, 0)),
            pl.BlockSpec((1, hg, HEAD_DIM, seq), lambda b, h: (b, h, 0, 0)),
            pl.BlockSpec((hg, N_META, AUG_DIM), lambda b, h: (h, 0, 0)),
            pl.BlockSpec((hg, HEAD_DIM, N_META), lambda b, h: (h, 0, 0)),
        ],
        out_specs=pl.BlockSpec((1, hg, HEAD_DIM, seq), lambda b, h: (b, h, 0, 0)),
        scratch_shapes=[pltpu.VMEM((Q_PER_STEP * hg, HEAD_DIM, Q_BLOCK), F32),
                        pltpu.VMEM((Q_PER_STEP * hg, SUBLANES, Q_BLOCK), F32)],
        name="fox_attention",
        compiler_params=pltpu.CompilerParams(
            dimension_semantics=("arbitrary", "arbitrary"), vmem_limit_bytes=VMEM_LIMIT_BYTES),
    )(fq0, flast, fmeta, c_bound, qT, k, vT, k_meta, vT_meta)


def _mix_out_ffn2_kernel(h1_ref, oT_ref, oconv_ref,
                         gattn_ref, woa_ref, woc_ref,
                         g2_ref, win_hbm, wout_hbm, gfin_ref,
                         out_ref,
                         xn_ref, acc_ref, act_ref, win_ref, wout_ref, stage_in, stage_out, sem):
    def tile(prepare):
        o_t = oT_ref[0].astype(F32)
        ms = jnp.mean(o_t * o_t, axis=0, keepdims=True)
        o_n = (o_t * lax.rsqrt(ms + EPS) * gattn_ref[...]).T.astype(BF16)
        mix = _dot(o_n, woa_ref[...]) + _dot(oconv_ref[0], woc_ref[...])
        h2 = h1_ref[0] + mix
        xn_ref[...] = _rmsnorm_rows(h2, g2_ref[...]).astype(BF16)
        _swiglu(xn_ref, win_ref, wout_ref, acc_ref, act_ref, h2, prepare)
        out_ref[0] = _rmsnorm_rows(acc_ref[...], gfin_ref[...])

    first = jnp.logical_and(pl.program_id(0) == 0, pl.program_id(1) == 0)

    @pl.when(first)
    def _():
        tile(_stream_ffn_weights(win_hbm, wout_hbm, win_ref, wout_ref, stage_in, stage_out, sem))

    @pl.when(jnp.logical_not(first))
    def _():
        tile(None)


def _mix_out_ffn2(h1, o_t, oconv, p, tm):
    nb, seq, _ = h1.shape
    gattn_t = jnp.broadcast_to(p["gattn"][:, None], (D_ATTN, tm))
    in_specs = [
        pl.BlockSpec((1, tm, D_MODEL), lambda b, t: (b, t, 0)),
        pl.BlockSpec((1, D_ATTN, tm), lambda b, t: (b, 0, t)),
        pl.BlockSpec((1, tm, D_CONV), lambda b, t: (b, t, 0)),
        _const_spec((D_ATTN, tm)), _const_spec((D_ATTN, D_MODEL)), _const_spec((D_CONV, D_MODEL)),
        _const_spec((1, D_MODEL)),
        pl.BlockSpec(memory_space=pl.ANY), pl.BlockSpec(memory_space=pl.ANY),
        _const_spec((1, D_MODEL)),
    ]
    return pl.pallas_call(
        _mix_out_ffn2_kernel,
        out_shape=jax.ShapeDtypeStruct((nb, seq, D_MODEL), F32),
        grid=(nb, seq // tm),
        in_specs=in_specs,
        out_specs=pl.BlockSpec((1, tm, D_MODEL), lambda b, t: (b, t, 0)),
        scratch_shapes=[pltpu.VMEM((tm, D_MODEL), BF16), pltpu.VMEM((tm, D_MODEL), F32),
                        pltpu.VMEM((tm, D_FF), BF16)] + _ffn_weight_scratch(STAGE_SLOTS),
        name="mix_out_ffn2",
        compiler_params=pltpu.CompilerParams(
            dimension_semantics=("arbitrary", "arbitrary"), vmem_limit_bytes=VMEM_LIMIT_BYTES),
    )(h1, o_t, oconv, gattn_t, p["woa"], p["woc"], p["g2"], p["win2"], p["wout2"], p["gfin"])


def kernel(x, meta_tokens, ffn1_norm, ffn1_w_in, ffn1_w_out, mix_norm, w_mix_in, b_forget, q_norm, k_norm, conv_w, attn_out_norm, conv_out_norm, w_mix_out, ffn2_norm, ffn2_w_in, ffn2_w_out, final_norm):
    nb, seq, _ = x.shape
    wmix = w_mix_in[0]
    n_qkv = 3 * D_ATTN
    p = {
        "g1": ffn1_norm, "win1": ffn1_w_in[0], "wout1": ffn1_w_out[0],
        "gmix": mix_norm,
        "wfkT": jnp.concatenate([jnp.pad(wmix[:, n_qkv:n_qkv + HEADS].T, ((0, F_ROWS - HEADS), (0, 0))),
                                 wmix[:, D_ATTN:2 * D_ATTN].T], axis=0).astype(BF16),
        "wqT": wmix[:, :D_ATTN].T.astype(BF16),
        "wvT": wmix[:, 2 * D_ATTN:n_qkv].T.astype(BF16),
        "wcu": wmix[:, n_qkv + HEADS + D_CONV:].astype(BF16),
        "wb": wmix[:, n_qkv + HEADS:n_qkv + HEADS + D_CONV].astype(BF16),
        "bf": jnp.pad(b_forget[0], (0, F_ROWS - HEADS))[:, None],
        "gq": q_norm[0], "gk": k_norm[0],
        "cw": jnp.pad(conv_w[0], ((0, SUBLANES - CONV_WIDTH), (0, 0))),
        "gconv": conv_out_norm,
        "gattn": attn_out_norm[0],
        "woa": w_mix_out[0, :D_ATTN].astype(BF16), "woc": w_mix_out[0, D_ATTN:].astype(BF16),
        "g2": ffn2_norm, "win2": ffn2_w_in[0], "wout2": ffn2_w_out[0],
        "gfin": final_norm,
    }

    meta = jnp.pad(meta_tokens.astype(x.dtype), ((0, META_TILE - N_META), (0, 0)))
    h1, qT, k, vT, f_cum, oconv, k_meta, vT_meta, f_meta = _ffn1_mix_in(x, meta, p, TOKEN_TILE)

    fq0 = f_cum[:, :, 0::Q_BLOCK].reshape(nb * HEADS, seq // Q_BLOCK)
    flast = f_cum[:, :, K_BLOCK - 1::K_BLOCK].reshape(nb * HEADS, seq // K_BLOCK)
    c_bound = (2.0 * HEAD_DIM ** 0.5 * SCORE_BOUND_SLACK * jnp.max(jnp.abs(q_norm)) * jnp.max(jnp.abs(k_norm))
               + SCORE_BOUND_MARGIN).reshape(1).astype(F32)

    o_t = _fox_attention(qT, k, vT, k_meta, vT_meta, fq0, flast, f_meta[:HEADS, 0], c_bound)
    o_t = o_t.reshape(nb, D_ATTN, seq)
    return _mix_out_ffn2(h1, o_t, oconv, p, TOKEN_TILE)
```
